```python
import jax, jax.numpy as jnp
from jax import lax
import numpy as np

D_MODEL = 1024
BATCH = 4
SEQ = 4096
DEPTH = 2

MLA_HEADS = 8
MLA_Q_RANK = 256
MLA_KV_RANK = 128
MLA_NOPE = 64
MLA_ROPE = 32
MLA_V = 64
MOBA_HEADS = 8
MOBA_HEAD_DIM = 64
MOBA_BLOCK = 256
MOBA_TOPK = 3
MOBA_Q_CHUNK = 32
RET_HEADS = 4
RET_QK_DIM = 64
RET_V_DIM = 128
RET_CHUNK = 128
RET_THETA = 10000.0
ROPE_THETA = 500000.0
PARTIAL_ROPE_DIV = 4
ATTN_Q_BLOCK = 128
D_FF = 4 * D_MODEL
NORM_EPS = 1e-6
GN_EPS = 1e-5
NEG_INF = -1e30

MOBA_W = MOBA_HEADS * MOBA_HEAD_DIM
RET_QK_W = RET_HEADS * RET_QK_DIM
RET_V_W = RET_HEADS * RET_V_DIM
MLA_OUT_W = MLA_HEADS * MLA_V
IN_SPLIT_SIZES = (MLA_Q_RANK, MLA_KV_RANK, MLA_ROPE,
                  MOBA_W, MOBA_W, MOBA_W,
                  RET_QK_W, RET_QK_W, RET_V_W, RET_V_W,
                  D_MODEL, D_MODEL, D_MODEL)
IN_COLS = sum(IN_SPLIT_SIZES)
IN_SPLIT_IDX = tuple(int(i) for i in np.cumsum(IN_SPLIT_SIZES)[:-1])

kernel_name = "hybrid_mla_moba_retention_block"


def rms_norm(x, g, eps=NORM_EPS):
    xf = x.astype(jnp.float32)
    y = xf * lax.rsqrt(jnp.mean(xf * xf, axis=-1, keepdims=True) + eps)
    return (y * g.astype(jnp.float32)).astype(x.dtype)


def rope_tables(seq, dim, theta):
    inv = 1.0 / (theta ** (jnp.arange(0, dim, 2, dtype=jnp.float32) / dim))
    ang = jnp.arange(seq, dtype=jnp.float32)[:, None] * inv[None, :]
    return jnp.cos(ang), jnp.sin(ang)


def apply_rope(x, cos, sin):
    half = x.shape[-1] // 2
    x1, x2 = x[..., :half], x[..., half:]
    c, s = cos.astype(x.dtype), sin.astype(x.dtype)
    return jnp.concatenate([x1 * c - x2 * s, x1 * s + x2 * c], axis=-1)


def causal_attention(q, k, v, scale):
    B, H, S, _ = q.shape
    kpos = jnp.arange(S)

    def block(i):
        start = i * ATTN_Q_BLOCK
        qb = lax.dynamic_slice_in_dim(q, start, ATTN_Q_BLOCK, axis=2)
        s = jnp.einsum('bhqd,bhkd->bhqk', qb, k).astype(jnp.float32) * scale
        qpos = start + jnp.arange(ATTN_Q_BLOCK)
        s = jnp.where(kpos[None, :] <= qpos[:, None], s, NEG_INF)
        p = jax.nn.softmax(s, axis=-1).astype(v.dtype)
        return jnp.einsum('bhqk,bhkd->bhqd', p, v)

    o = lax.map(block, jnp.arange(S // ATTN_Q_BLOCK))
    return o.transpose(1, 2, 0, 3, 4).reshape(B, H, S, v.shape[-1])


def mla_branch(cq, ckv, k_rope, q_norm_g, kv_norm_g, w_q_up, w_kv_up, cos, sin):
    B, S, _ = cq.shape
    cq = rms_norm(cq, q_norm_g)
    ckv = rms_norm(ckv, kv_norm_g)
    q = (cq @ w_q_up).reshape(B, S, MLA_HEADS, MLA_NOPE + MLA_ROPE).transpose(0, 2, 1, 3)
    q = jnp.concatenate([q[..., :MLA_NOPE], apply_rope(q[..., MLA_NOPE:], cos, sin)], axis=-1)
    kv = (ckv @ w_kv_up).reshape(B, S, MLA_HEADS, MLA_NOPE + MLA_V).transpose(0, 2, 1, 3)
    k_pe = apply_rope(k_rope, cos, sin)[:, None]
    k = jnp.concatenate([kv[..., :MLA_NOPE],
                         jnp.broadcast_to(k_pe, (B, MLA_HEADS, S, MLA_ROPE))], axis=-1)
    v = kv[..., MLA_NOPE:]
    o = causal_attention(q, k, v, (MLA_NOPE + MLA_ROPE) ** -0.5)
    return o.transpose(0, 2, 1, 3).reshape(B, S, MLA_OUT_W)


def moba_branch(q, k, v, cos, sin):
    B, S, _ = q.shape
    H, Dh, BS, QC = MOBA_HEADS, MOBA_HEAD_DIM, MOBA_BLOCK, MOBA_Q_CHUNK
    rd = Dh // PARTIAL_ROPE_DIV

    def heads(t):
        return t.reshape(B, S, H, Dh).transpose(0, 2, 1, 3)

    def partial_rope(t):
        return jnp.concatenate([apply_rope(t[..., :rd], cos, sin), t[..., rd:]], axis=-1)

    q, k, v = partial_rope(heads(q)), partial_rope(heads(k)), heads(v)
    nb = -(-S // BS)
    pad = nb * BS - S
    padw = ((0, 0), (0, 0), (0, pad), (0, 0))
    q, k, v = jnp.pad(q, padw), jnp.pad(k, padw), jnp.pad(v, padw)
    kb = k.reshape(B, H, nb, BS, Dh)
    vb = v.reshape(B, H, nb, BS, Dh)
    k_mean = jnp.mean(kb.astype(jnp.float32), axis=3)
    topk = min(MOBA_TOPK, nb)
    scale = Dh ** -0.5
    bi = jnp.arange(B)[:, None, None, None]
    hi = jnp.arange(H)[None, :, None, None]

    def chunk(c):
        start = c * QC
        blk = start // BS
        qc = lax.dynamic_slice_in_dim(q, start, QC, axis=2)
        gate = jnp.einsum('bhqd,bhnd->bhqn', qc.astype(jnp.float32), k_mean)
        gate = jnp.where(jnp.arange(nb) < blk, gate, NEG_INF)
        _, idx = lax.top_k(gate, topk)
        sel_valid = jnp.arange(topk) < blk
        k_sel = kb[bi, hi, idx]
        v_sel = vb[bi, hi, idx]
        s_sel = jnp.einsum('bhqd,bhqnkd->bhqnk', qc, k_sel).astype(jnp.float32) * scale
        s_sel = jnp.where(sel_valid[:, None], s_sel, NEG_INF).reshape(B, H, QC, topk * BS)
        k_own = lax.dynamic_index_in_dim(kb, blk, axis=2, keepdims=False)
        v_own = lax.dynamic_index_in_dim(vb, blk, axis=2, keepdims=False)
        s_own = jnp.einsum('bhqd,bhkd->bhqk', qc, k_own).astype(jnp.float32) * scale
        qpos = start + jnp.arange(QC)
        kpos = blk * BS + jnp.arange(BS)
        s_own = jnp.where(kpos[None, :] <= qpos[:, None], s_own, NEG_INF)
        p = jax.nn.softmax(jnp.concatenate([s_sel, s_own], axis=-1), axis=-1).astype(v.dtype)
        o_sel = jnp.einsum('bhqn,bhqnd->bhqd', p[..., :topk * BS],
                           v_sel.reshape(B, H, QC, topk * BS, Dh))
        o_own = jnp.einsum('bhqk,bhkd->bhqd', p[..., topk * BS:], v_own)
        return o_sel + o_own

    o = lax.map(chunk, jnp.arange(nb * BS // QC))
    o = o.transpose(1, 2, 0, 3, 4).reshape(B, H, nb * BS, Dh)[:, :, :S]
    return o.transpose(0, 2, 1, 3).reshape(B, S, MOBA_W)


def retention_branch(q, k, v, g, gn_w, gn_b, cos, sin):
    B, S, _ = q.shape
    H, Dk, Dv, C = RET_HEADS, RET_QK_DIM, RET_V_DIM, RET_CHUNK
    nc = S // C
    f32 = jnp.float32
    q = apply_rope(q.astype(f32).reshape(B, S, H, Dk).transpose(0, 2, 1, 3), cos, sin)
    k = apply_rope(k.astype(f32).reshape(B, S, H, Dk).transpose(0, 2, 1, 3), cos, sin) * (Dk ** -0.5)
    v = v.astype(f32).reshape(B, S, H, Dv).transpose(0, 2, 1, 3)
    log_gamma = jnp.log(1.0 - 2.0 ** (-5.0 - jnp.arange(H, dtype=f32)))
    pos = jnp.arange(C, dtype=f32)
    diff = pos[:, None] - pos[None, :]
    decay_in = jnp.where(diff >= 0, jnp.exp(log_gamma[:, None, None] * diff), 0.0)
    xi = jnp.exp(log_gamma[:, None] * (pos + 1.0))
    zeta = jnp.exp(log_gamma[:, None] * (C - 1.0 - pos))
    gamma_c = jnp.exp(log_gamma * C)[None, :, None, None]
    qc = q.reshape(B, H, nc, C, Dk)
    kc = k.reshape(B, H, nc, C, Dk)
    vc = v.reshape(B, H, nc, C, Dv)
    scores = jnp.einsum('bhnid,bhnjd->bhnij', qc, kc) * decay_in[:, None]
    inner = jnp.einsum('bhnij,bhnje->bhnie', scores, vc)
    kv_chunk = jnp.einsum('bhnjd,bhnje->bhnde', kc * zeta[:, None, :, None], vc)

    def step(R, kv):
        return gamma_c * R + kv, R

    _, R_prev = lax.scan(step, jnp.zeros((B, H, Dk, Dv), f32), kv_chunk.transpose(2, 0, 1, 3, 4))
    R_prev = R_prev.transpose(1, 2, 0, 3, 4)
    cross = jnp.einsum('bhnid,bhnde->bhnie', qc * xi[:, None, :, None], R_prev)
    o = (inner + cross).reshape(B, H, S, Dv)
    mu = jnp.mean(o, axis=-1, keepdims=True)
    var = jnp.mean(jnp.square(o - mu), axis=-1, keepdims=True)
    o = (o - mu) * lax.rsqrt(var + GN_EPS)
    o = o.transpose(0, 2, 1, 3).reshape(B, S, RET_V_W) * gn_w.astype(f32) + gn_b.astype(f32)
    return jax.nn.silu(g) * o.astype(g.dtype)


def hybrid_mixer(h, w_in, mla_q_norm, mla_kv_norm, mla_w_q_up, mla_w_kv_up, ret_gn_w, ret_gn_b,
                 w_branch_mla, w_branch_moba, w_branch_ret, w_out, ropes):
    (cos_mla, sin_mla), (cos_moba, sin_moba), (cos_ret, sin_ret) = ropes
    p = h @ w_in
    (cq, ckv, k_rope, mq, mk, mv, rq, rk, rv, rg, g_mla, g_moba, g_ret) = jnp.split(p, IN_SPLIT_IDX, axis=-1)
    o_mla = mla_branch(cq, ckv, k_rope, mla_q_norm, mla_kv_norm, mla_w_q_up, mla_w_kv_up, cos_mla, sin_mla)
    o_moba = moba_branch(mq, mk, mv, cos_moba, sin_moba)
    o_ret = retention_branch(rq, rk, rv, rg, ret_gn_w, ret_gn_b, cos_ret, sin_ret)
    merged = (jax.nn.sigmoid(g_mla) * (o_mla @ w_branch_mla)
              + jax.nn.sigmoid(g_moba) * (o_moba @ w_branch_moba)
              + jax.nn.sigmoid(g_ret) * (o_ret @ w_branch_ret))
    return merged @ w_out


def squared_relu_mlp(h, w_up, w_down):
    return jnp.square(jax.nn.relu(h @ w_up)) @ w_down


def setup_inputs(seed: int = 0) -> dict:
    key = jax.random.key(seed)
    ks = jax.random.split(key, 18)

    def dense(k, shape):
        return jax.random.normal(k, shape, jnp.float32) * (shape[-2] ** -0.5)

    def gain(k, shape):
        return 1.0 + 0.02 * jax.random.normal(k, shape, jnp.float32)

    L = DEPTH
    return {
        "x": jax.random.normal(ks[0], (BATCH, SEQ, D_MODEL), jnp.float32),
        "attn_norm": gain(ks[1], (L, D_MODEL)),
        "w_in": dense(ks[2], (L, D_MODEL, IN_COLS)),
        "mla_q_norm": gain(ks[3], (L, MLA_Q_RANK)),
        "mla_kv_norm": gain(ks[4], (L, MLA_KV_RANK)),
        "mla_w_q_up": dense(ks[5], (L, MLA_Q_RANK, MLA_HEADS * (MLA_NOPE + MLA_ROPE))),
        "mla_w_kv_up": dense(ks[6], (L, MLA_KV_RANK, MLA_HEADS * (MLA_NOPE + MLA_V))),
        "ret_gn_w": gain(ks[7], (L, RET_V_W)),
        "ret_gn_b": 0.02 * jax.random.normal(ks[8], (L, RET_V_W), jnp.float32),
        "w_branch_mla": dense(ks[9], (L, MLA_OUT_W, D_MODEL)),
        "w_branch_moba": dense(ks[10], (L, MOBA_W, D_MODEL)),
        "w_branch_ret": dense(ks[11], (L, RET_V_W, D_MODEL)),
        "w_out": dense(ks[12], (L, D_MODEL, D_MODEL)),
        "mlp_norm": gain(ks[13], (L, D_MODEL)),
        "w_mlp_up": dense(ks[14], (L, D_MODEL, D_FF)),
        "w_mlp_down": dense(ks[15], (L, D_FF, D_MODEL)),
        "final_norm": gain(ks[16], (D_MODEL,)),
    }


def reference(x, attn_norm, w_in, mla_q_norm, mla_kv_norm, mla_w_q_up, mla_w_kv_up, ret_gn_w, ret_gn_b,
              w_branch_mla, w_branch_moba, w_branch_ret, w_out, mlp_norm, w_mlp_up, w_mlp_down, final_norm):
    S = x.shape[1]
    ropes = (rope_tables(S, MLA_ROPE, ROPE_THETA),
             rope_tables(S, MOBA_HEAD_DIM // PARTIAL_ROPE_DIV, ROPE_THETA),
             rope_tables(S, RET_QK_DIM, RET_THETA))
    for l in range(DEPTH):
        h = rms_norm(x, attn_norm[l])
        x = x + hybrid_mixer(h, w_in[l], mla_q_norm[l], mla_kv_norm[l], mla_w_q_up[l], mla_w_kv_up[l],
                             ret_gn_w[l], ret_gn_b[l], w_branch_mla[l], w_branch_moba[l], w_branch_ret[l],
                             w_out[l], ropes)
        h = rms_norm(x, mlp_norm[l])
        x = x + squared_relu_mlp(h, w_mlp_up[l], w_mlp_down[l])
    return rms_norm(x, final_norm)
```

```python
import functools

import jax
import jax.numpy as jnp
from jax import lax
from jax.experimental import pallas as pl
from jax.experimental.pallas import tpu as pltpu

F32 = jnp.float32
BF16 = jnp.bfloat16

D_MODEL = 1024
MLA_HEADS = 8
MLA_Q_RANK = 256
MLA_KV_RANK = 128
MLA_NOPE = 64
MLA_ROPE = 32
MLA_V = 64
MOBA_HEADS = 8
MOBA_HEAD_DIM = 64
MOBA_BLOCK = 256
MOBA_TOPK = 3
RET_HEADS = 4
RET_QK_DIM = 64
RET_V_DIM = 128
RET_CHUNK = 128
RET_THETA = 10000.0
ROPE_THETA = 500000.0
PARTIAL_ROPE_DIV = 4
D_FF = 4 * D_MODEL
NORM_EPS = 1e-6
GN_EPS = 1e-5
NEG_INF = -1e30

LANES = 128
HEAD_PAIR = 2
VMEM_LIMIT = 48 * 1024 * 1024

GROUP = 512
G_GATES = 0
G_MLA = 6
G_MOBA_Q, G_MOBA_K, G_MOBA_V = 7, 8, 9
G_RET_QK, G_RET_V, G_RET_G = 10, 11, 12
N_GROUPS = 13
IN_COLS_PADDED = N_GROUPS * GROUP


def _nt_dot(a, b):
    return lax.dot_general(a, b, (((1,), (1,)), ((), ())), preferred_element_type=F32)


def _rms(x, g):
    return x * lax.rsqrt(jnp.mean(x * x, axis=-1, keepdims=True) + NORM_EPS) * g


def _rope_lanes(y, tab_ref, half):
    return (y * tab_ref[0]
            + pltpu.roll(y, LANES - half, 1) * tab_ref[1]
            + pltpu.roll(y, half, 1) * tab_ref[2])


def _in_proj_kernel(x_ref, g_ref, w_ref, moba_tab, ret_tab, o_ref, h_ref):
    j = pl.program_id(1)

    @pl.when(j == 0)
    def _():
        h_ref[...] = _rms(x_ref[...], g_ref[...]).astype(BF16)

    acc = jnp.dot(h_ref[...], w_ref[...], preferred_element_type=F32)
    is_moba = jnp.logical_or(j == G_MOBA_Q, j == G_MOBA_K)
    is_ret = j == G_RET_QK

    @pl.when(is_moba)
    def _():
        for r in range(GROUP // LANES):
            sl = slice(r * LANES, (r + 1) * LANES)
            o_ref[:, sl] = _rope_lanes(acc[:, sl], moba_tab, MOBA_HEAD_DIM // PARTIAL_ROPE_DIV // 2).astype(BF16)

    @pl.when(is_ret)
    def _():
        for r in range(GROUP // LANES):
            sl = slice(r * LANES, (r + 1) * LANES)
            y = _rope_lanes(acc[:, sl], ret_tab, RET_QK_DIM // 2)
            if r >= GROUP // LANES // 2:
                y = y * (RET_QK_DIM ** -0.5)
            o_ref[:, sl] = y.astype(BF16)

    @pl.when(jnp.logical_not(jnp.logical_or(is_moba, is_ret)))
    def _():
        o_ref[...] = acc.astype(BF16)


def _in_proj(x2, g, w, moba_tab, ret_tab, seq, tm):
    tokens = x2.shape[0]
    s_tiles = seq // tm
    return pl.pallas_call(
        _in_proj_kernel,
        grid=(tokens // tm, N_GROUPS),
        in_specs=[
            pl.BlockSpec((tm, D_MODEL), lambda i, j: (i, 0)),
            pl.BlockSpec((1, D_MODEL), lambda i, j: (0, 0)),
            pl.BlockSpec((D_MODEL, GROUP), lambda i, j: (0, j)),
            pl.BlockSpec((3, tm, LANES), lambda i, j: (0, i % s_tiles, 0)),
            pl.BlockSpec((3, tm, LANES), lambda i, j: (0, i % s_tiles, 0)),
        ],
        out_specs=pl.BlockSpec((tm, GROUP), lambda i, j: (i, j)),
        out_shape=jax.ShapeDtypeStruct((tokens, IN_COLS_PADDED), BF16),
        scratch_shapes=[pltpu.VMEM((tm, D_MODEL), BF16)],
        compiler_params=pltpu.CompilerParams(
            dimension_semantics=("parallel", "arbitrary"), vmem_limit_bytes=VMEM_LIMIT),
        name="in_proj",
    )(x2, g, w, moba_tab, ret_tab)


def _mla_prep_kernel(p_ref, qg_ref, kvg_ref, wq_ref, wk_ref, wv_ref, tab, q_ref, k_ref, v_ref):
    blk = p_ref[...]
    cq = blk[:, :MLA_Q_RANK].astype(F32)
    ckv = blk[:, MLA_Q_RANK:MLA_Q_RANK + MLA_KV_RANK].astype(F32)
    kr = blk[:, MLA_Q_RANK + MLA_KV_RANK:].astype(F32)
    cqn = _rms(cq, qg_ref[...]).astype(BF16)
    ckvn = _rms(ckv, kvg_ref[...]).astype(BF16)
    q = jnp.dot(cqn, wq_ref[...], preferred_element_type=F32)
    k = jnp.dot(ckvn, wk_ref[...], preferred_element_type=F32)
    v = jnp.dot(ckvn, wv_ref[...], preferred_element_type=F32)
    kpe = _rope_lanes(kr, tab, MLA_ROPE // 2)
    scale = (MLA_NOPE + MLA_ROPE) ** -0.5
    for h in range(MLA_HEADS):
        sl = slice(h * LANES, (h + 1) * LANES)
        q_ref[0, h] = (_rope_lanes(q[:, sl], tab, MLA_ROPE // 2) * scale).astype(BF16)
        k_ref[0, h] = (k[:, sl] + kpe).astype(BF16)
    v_ref[0] = v.astype(BF16)


def _mla_prep(p3, qg, kvg, wq, wk, wv, tab, tm):
    batch, seq, _ = p3.shape
    hs = jax.ShapeDtypeStruct((batch, MLA_HEADS, seq, LANES), BF16)
    const = lambda shape: pl.BlockSpec(shape, lambda b, i: (0,) * len(shape))
    return pl.pallas_call(
        _mla_prep_kernel,
        grid=(batch, seq // tm),
        in_specs=[
            pl.BlockSpec((None, tm, GROUP), lambda b, i: (b, i, G_MLA)),
            const((1, MLA_Q_RANK)), const((1, MLA_KV_RANK)),
            const((MLA_Q_RANK, MLA_HEADS * LANES)), const((MLA_KV_RANK, MLA_HEADS * LANES)),
            const((MLA_KV_RANK, MLA_HEADS * MLA_V)),
            pl.BlockSpec((3, tm, LANES), lambda b, i: (0, i, 0)),
        ],
        out_specs=[
            pl.BlockSpec((1, MLA_HEADS, tm, LANES), lambda b, i: (b, 0, i, 0)),
            pl.BlockSpec((1, MLA_HEADS, tm, LANES), lambda b, i: (b, 0, i, 0)),
            pl.BlockSpec((1, tm, MLA_HEADS * MLA_V), lambda b, i: (b, i, 0)),
        ],
        out_shape=[hs, hs, jax.ShapeDtypeStruct((batch, seq, MLA_HEADS * MLA_V), BF16)],
        compiler_params=pltpu.CompilerParams(
            dimension_semantics=("parallel", "parallel"), vmem_limit_bytes=VMEM_LIMIT),
        name="mla_prep",
    )(p3, qg, kvg, wq, wk, wv, tab)


def _softmax_first(s, v):
    m = jnp.max(s, axis=-1, keepdims=True)
    p = jnp.exp(s - m)
    l = jnp.sum(p, axis=-1, keepdims=True)
    acc = jnp.dot(p.astype(BF16), v, preferred_element_type=F32)
    return m, l, acc


def _softmax_next(carry, s, v):
    m, l, acc = carry
    m_new = jnp.maximum(m, jnp.max(s, axis=-1, keepdims=True))
    alpha = jnp.exp(m - m_new)
    p = jnp.exp(s - m_new)
    l = alpha * l + jnp.sum(p, axis=-1, keepdims=True)
    acc = alpha * acc + jnp.dot(p.astype(BF16), v, preferred_element_type=F32)
    return m_new, l, acc


def _causal_mask(t):
    row = lax.broadcasted_iota(jnp.int32, (t, t), 0)
    col = lax.broadcasted_iota(jnp.int32, (t, t), 1)
    return col <= row


def _mla_attn_kernel(q_ref, k_ref, v_ref, o_ref, *, t):
    qi = pl.program_id(2)
    causal = _causal_mask(t)
    diag = pl.multiple_of(qi * t, t)
    outs = []
    for hh in range(HEAD_PAIR):
        q = q_ref[0, hh]

        def step(kj, carry, q=q, hh=hh):
            start = pl.multiple_of(kj * t, t)
            s = _nt_dot(q, k_ref[0, hh, pl.ds(start, t), :])
            return _softmax_next(carry, s, v_ref[0, pl.ds(start, t), :])

        s = jnp.where(causal, _nt_dot(q, k_ref[0, hh, pl.ds(diag, t), :]), NEG_INF)
        carry = _softmax_first(s, v_ref[0, pl.ds(diag, t), :])
        m, l, acc = lax.fori_loop(0, qi, step, carry)
        outs.append(acc / l)
    lane = lax.broadcasted_iota(jnp.int32, (t, LANES), 1)
    o_ref[0] = jnp.where(lane < MLA_V, outs[0], outs[1]).astype(BF16)


def _mla_attn(q, k, v, t):
    batch, heads, seq, _ = q.shape
    return pl.pallas_call(
        functools.partial(_mla_attn_kernel, t=t),
        grid=(batch, heads // HEAD_PAIR, seq // t),
        in_specs=[
            pl.BlockSpec((1, HEAD_PAIR, t, LANES), lambda b, h, i: (b, h, i, 0)),
            pl.BlockSpec((1, HEAD_PAIR, seq, LANES), lambda b, h, i: (b, h, 0, 0)),
            pl.BlockSpec((1, seq, LANES), lambda b, h, i: (b, 0, h)),
        ],
        out_specs=pl.BlockSpec((1, t, LANES), lambda b, h, i: (b, i, h)),
        out_shape=jax.ShapeDtypeStruct((batch, seq, heads * MLA_V), BF16),
        compiler_params=pltpu.CompilerParams(
            dimension_semantics=("parallel", "parallel", "arbitrary"), vmem_limit_bytes=VMEM_LIMIT),
        name="mla_attn",
    )(q, k, v)


def _moba_kernel(q_ref, k_ref, v_ref, o_ref, kmean_ref, *, nb):
    t = MOBA_BLOCK
    qi = pl.program_id(2)

    @pl.when(qi == 0)
    def _():
        kmean_ref[...] = jnp.zeros_like(kmean_ref)
        for j in range(nb):
            kb = k_ref[0, j * t:(j + 1) * t, :].astype(F32)
            kmean_ref[j:j + 1, :] = jnp.mean(kb, axis=0, keepdims=True)

    causal = _causal_mask(t)
    lane = lax.broadcasted_iota(jnp.int32, (t, LANES), 1)
    lane_k = lax.broadcasted_iota(jnp.int32, (LANES, LANES), 1)
    blk_id = lax.broadcasted_iota(jnp.int32, (nb, t), 0)
    diag = pl.multiple_of(qi * t, t)
    n_sel = jnp.minimum(qi, MOBA_TOPK).astype(F32)
    q_pair = q_ref[0]
    outs = []
    for hh in range(HEAD_PAIR):
        in_head = (lane >= hh * MOBA_HEAD_DIM) & (lane < (hh + 1) * MOBA_HEAD_DIM)
        q_gate = jnp.where(in_head, q_pair, jnp.zeros_like(q_pair))
        q = q_gate * (MOBA_HEAD_DIM ** -0.5)
        in_head_k = (lane_k >= hh * MOBA_HEAD_DIM) & (lane_k < (hh + 1) * MOBA_HEAD_DIM)
        km = jnp.where(in_head_k, kmean_ref[...], 0.0)
        gate = lax.dot_general(km, q_gate.astype(F32), (((1,), (1,)), ((), ())),
                               precision=lax.Precision.HIGHEST, preferred_element_type=F32)[:nb]
        gate = jnp.where(blk_id < qi, gate, NEG_INF)
        rank = jnp.zeros((nb, t), F32)
        for jp in range(nb):
            gj = gate[jp:jp + 1, :]
            beats = (gj > gate) | ((gj == gate) & (blk_id > jp))
            rank = rank + jnp.where(beats, 1.0, 0.0)
        sel_t = jnp.where((rank < n_sel) & (blk_id < qi), 1.0, 0.0)
        sel = jnp.concatenate([sel_t, jnp.zeros((LANES - nb, t), F32)], axis=0).T

        def step(kj, carry, q=q, sel=sel):
            start = pl.multiple_of(kj * t, t)
            picked = jnp.sum(jnp.where(lane == kj, sel, 0.0), axis=-1, keepdims=True) > 0.5
            s = jnp.where(picked, _nt_dot(q, k_ref[0, pl.ds(start, t), :]), NEG_INF)
            return _softmax_next(carry, s, v_ref[0, pl.ds(start, t), :])

        s = jnp.where(causal, _nt_dot(q, k_ref[0, pl.ds(diag, t), :]), NEG_INF)
        carry = _softmax_first(s, v_ref[0, pl.ds(diag, t), :])
        m, l, acc = lax.fori_loop(0, qi, step, carry)
        outs.append(acc / l)
    o_ref[0] = jnp.where(lane < MOBA_HEAD_DIM, outs[0], outs[1]).astype(BF16)


def _moba_attn(p3):
    batch, seq, _ = p3.shape
    t = MOBA_BLOCK
    nb = seq // t
    per_group = GROUP // LANES
    return pl.pallas_call(
        functools.partial(_moba_kernel, nb=nb),
        grid=(batch, MOBA_HEADS // HEAD_PAIR, nb),
        in_specs=[
            pl.BlockSpec((1, t, LANES), lambda b, h, i: (b, i, G_MOBA_Q * per_group + h)),
            pl.BlockSpec((1, seq, LANES), lambda b, h, i: (b, 0, G_MOBA_K * per_group + h)),
            pl.BlockSpec((1, seq, LANES), lambda b, h, i: (b, 0, G_MOBA_V * per_group + h)),
        ],
        out_specs=pl.BlockSpec((1, t, LANES), lambda b, h, i: (b, i, h)),
        out_shape=jax.ShapeDtypeStruct((batch, seq, MOBA_HEADS * MOBA_HEAD_DIM), BF16),
        scratch_shapes=[pltpu.VMEM((LANES, LANES), F32)],
        compiler_params=pltpu.CompilerParams(
            dimension_semantics=("parallel", "parallel", "arbitrary"), vmem_limit_bytes=VMEM_LIMIT),
        name="moba_attn",
    )(p3, p3, p3)


def _retention_kernel(qk_ref, v_ref, g_ref, decay_ref, xi_ref, zeta_ref, gamma_ref, gnw_ref, gnb_ref,
                      o_ref, state_ref, *, chunks):
    c = RET_CHUNK

    @pl.when(pl.program_id(1) == 0)
    def _():
        state_ref[...] = jnp.zeros_like(state_ref)

    lane = lax.broadcasted_iota(jnp.int32, (c, LANES), 1)
    k_off = RET_HEADS * RET_QK_DIM
    for n in range(chunks):
        rows = slice(n * c, (n + 1) * c)
        for h in range(RET_HEADS):
            pair, half = divmod(h, HEAD_PAIR)
            in_head = (lane >= half * RET_QK_DIM) & (lane < (half + 1) * RET_QK_DIM)
            q = jnp.where(in_head, qk_ref[0, rows, pair * LANES:(pair + 1) * LANES].astype(F32), 0.0)
            k = jnp.where(in_head, qk_ref[0, rows, k_off + pair * LANES:k_off + (pair + 1) * LANES].astype(F32), 0.0)
            vs = slice(h * RET_V_DIM, (h + 1) * RET_V_DIM)
            v = v_ref[0, rows, vs]
            state = state_ref[h]
            scores = _nt_dot(q.astype(BF16), k.astype(BF16)) * decay_ref[h]
            inner = jnp.dot(scores.astype(BF16), v, preferred_element_type=F32)
            cross = jnp.dot((q * xi_ref[h]).astype(BF16), state.astype(BF16), preferred_element_type=F32)
            kv = lax.dot_general((k * zeta_ref[h]).astype(BF16), v, (((0,), (0,)), ((), ())),
                                 preferred_element_type=F32)
            state_ref[h] = gamma_ref[h] * state + kv
            o = inner + cross
            mu = jnp.mean(o, axis=-1, keepdims=True)
            d = o - mu
            var = jnp.mean(d * d, axis=-1, keepdims=True)
            o = d * lax.rsqrt(var + GN_EPS) * gnw_ref[:, vs] + gnb_ref[:, vs]
            g = g_ref[0, rows, vs].astype(F32)
            o_ref[0, rows, vs] = (g * jax.nn.sigmoid(g) * o).astype(BF16)


def _retention(p3, decay, xi, zeta, gamma, gnw, gnb, tr):
    batch, seq, _ = p3.shape
    width = RET_HEADS * RET_V_DIM
    const = lambda shape: pl.BlockSpec(shape, lambda b, i: (0,) * len(shape))
    return pl.pallas_call(
        functools.partial(_retention_kernel, chunks=tr // RET_CHUNK),
        grid=(batch, seq // tr),
        in_specs=[
            pl.BlockSpec((1, tr, GROUP), lambda b, i: (b, i, G_RET_QK)),
            pl.BlockSpec((1, tr, GROUP), lambda b, i: (b, i, G_RET_V)),
            pl.BlockSpec((1, tr, GROUP), lambda b, i: (b, i, G_RET_G)),
            const((RET_HEADS, RET_CHUNK, RET_CHUNK)),
            const((RET_HEADS, RET_CHUNK, LANES)), const((RET_HEADS, RET_CHUNK, LANES)),
            const((RET_HEADS, 1, LANES)),
            const((1, width)), const((1, width)),
        ],
        out_specs=pl.BlockSpec((1, tr, width), lambda b, i: (b, i, 0)),
        out_shape=jax.ShapeDtypeStruct((batch, seq, width), BF16),
        scratch_shapes=[pltpu.VMEM((RET_HEADS, LANES, RET_V_DIM), F32)],
        compiler_params=pltpu.CompilerParams(
            dimension_semantics=("parallel", "arbitrary"), vmem_limit_bytes=VMEM_LIMIT),
        name="retention",
    )(p3, p3, p3, decay, xi, zeta, gamma, gnw, gnb)


def _merge_kernel(x_ref, gates_ref, oa_ref, ob_ref, oc_ref, wa_ref, wb_ref, wc_ref, wo_ref, o_ref):
    merged = None
    for idx, (o_in, w) in enumerate(((oa_ref, wa_ref), (ob_ref, wb_ref), (oc_ref, wc_ref))):
        gate = gates_ref[:, idx * D_MODEL:(idx + 1) * D_MODEL].astype(F32)
        term = jax.nn.sigmoid(gate) * jnp.dot(o_in[...], w[...], preferred_element_type=F32)
        merged = term if merged is None else merged + term
    o_ref[...] = x_ref[...] + jnp.dot(merged.astype(BF16), wo_ref[...], preferred_element_type=F32)


def _merge(x2, p2, oa, ob, oc, wa, wb, wc, wo, tm):
    tokens = x2.shape[0]
    row = lambda w: pl.BlockSpec((tm, w), lambda i: (i, 0))
    const = lambda shape: pl.BlockSpec(shape, lambda i: (0, 0))
    return pl.pallas_call(
        _merge_kernel,
        grid=(tokens // tm,),
        in_specs=[row(D_MODEL), row(3 * D_MODEL), row(GROUP), row(GROUP), row(GROUP),
                  const((GROUP, D_MODEL)), const((GROUP, D_MODEL)), const((GROUP, D_MODEL)),
                  const((D_MODEL, D_MODEL))],
        out_specs=row(D_MODEL),
        out_shape=jax.ShapeDtypeStruct((tokens, D_MODEL), F32),
        compiler_params=pltpu.CompilerParams(
            dimension_semantics=("parallel",), vmem_limit_bytes=VMEM_LIMIT),
        name="merge",
    )(x2, p2, oa, ob, oc, wa, wb, wc, wo)


def _mlp_kernel(x_ref, g_ref, wu_ref, wd_ref, fg_ref, o_ref, h_ref, *, final_norm):
    f = pl.program_id(1)

    @pl.when(f == 0)
    def _():
        x = x_ref[...]
        h_ref[...] = _rms(x, g_ref[...]).astype(BF16)
        o_ref[...] = x

    u = jnp.maximum(jnp.dot(h_ref[...], wu_ref[...], preferred_element_type=F32), 0.0)
    o_ref[...] += jnp.dot((u * u).astype(BF16), wd_ref[...], preferred_element_type=F32)

    if final_norm:
        @pl.when(f == pl.num_programs(1) - 1)
        def _():
            o_ref[...] = _rms(o_ref[...], fg_ref[...])


def _mlp(x2, g, wu, wd, fg, tm, tf, final_norm):
    tokens = x2.shape[0]
    return pl.pallas_call(
        functools.partial(_mlp_kernel, final_norm=final_norm),
        grid=(tokens // tm, D_FF // tf),
        in_specs=[
            pl.BlockSpec((tm, D_MODEL), lambda i, f: (i, 0)),
            pl.BlockSpec((1, D_MODEL), lambda i, f: (0, 0)),
            pl.BlockSpec((D_MODEL, tf), lambda i, f: (0, f)),
            pl.BlockSpec((tf, D_MODEL), lambda i, f: (f, 0)),
            pl.BlockSpec((1, D_MODEL), lambda i, f: (0, 0)),
        ],
        out_specs=pl.BlockSpec((tm, D_MODEL), lambda i, f: (i, 0)),
        out_shape=jax.ShapeDtypeStruct((tokens, D_MODEL), F32),
        scratch_shapes=[pltpu.VMEM((tm, D_MODEL), BF16)],
        compiler_params=pltpu.CompilerParams(
            dimension_semantics=("parallel", "arbitrary"), vmem_limit_bytes=VMEM_LIMIT),
        name="mlp",
    )(x2, g, wu, wd, fg)


def _rope_cos_sin(seq, dim, theta):
    inv = 1.0 / (theta ** (jnp.arange(0, dim, 2, dtype=F32) / dim))
    ang = jnp.arange(seq, dtype=F32)[:, None] * inv[None, :]
    return jnp.cos(ang), jnp.sin(ang)


def _rope_tables(seq, dim, theta, period, offset):
    cos, sin = _rope_cos_sin(seq, dim, theta)
    half = dim // 2
    lo, hi = slice(offset, offset + half), slice(offset + half, offset + dim)
    c = jnp.ones((seq, period), F32).at[:, lo].set(cos).at[:, hi].set(cos)
    s_next = jnp.zeros((seq, period), F32).at[:, lo].set(-sin)
    s_prev = jnp.zeros((seq, period), F32).at[:, hi].set(sin)
    return jnp.tile(jnp.stack([c, s_next, s_prev]), (1, 1, LANES // period))


def _retention_tables():
    c = RET_CHUNK
    log_gamma = jnp.log(1.0 - 2.0 ** (-5.0 - jnp.arange(RET_HEADS, dtype=F32)))
    pos = jnp.arange(c, dtype=F32)
    diff = pos[:, None] - pos[None, :]
    decay = jnp.where(diff >= 0, jnp.exp(log_gamma[:, None, None] * diff), 0.0)
    xi = jnp.exp(log_gamma[:, None] * (pos + 1.0))
    zeta = jnp.exp(log_gamma[:, None] * (c - 1.0 - pos))
    gamma_c = jnp.exp(log_gamma * c)
    bcast = lambda t: jnp.broadcast_to(t[..., None], t.shape + (LANES,))
    return decay, bcast(xi), bcast(zeta), bcast(gamma_c[:, None])


def _pad_in_proj_weight(w):
    sizes = (MLA_Q_RANK, MLA_KV_RANK, MLA_ROPE, 512, 512, 512, 256, 256, 512, 512, D_MODEL, D_MODEL, D_MODEL)
    parts, start = [], 0
    for sz in sizes:
        parts.append(w[:, start:start + sz])
        start += sz
    cq, ckv, kr, mq, mk, mv, rq, rk, rv, rg, g_mla, g_moba, g_ret = parts
    zeros = lambda n: jnp.zeros((w.shape[0], n), w.dtype)
    kr_pad = jnp.concatenate([zeros(MLA_NOPE), kr, zeros(LANES - MLA_NOPE - MLA_ROPE)], axis=1)
    return jnp.concatenate([g_mla, g_moba, g_ret, cq, ckv, kr_pad, mq, mk, mv, rq, rk, rv, rg],
                           axis=1).astype(BF16)


def _pad_mla_weights(w_q_up, w_kv_up):
    d_qk = MLA_NOPE + MLA_ROPE
    wq = w_q_up.reshape(MLA_Q_RANK, MLA_HEADS, d_qk)
    wq = jnp.pad(wq, ((0, 0), (0, 0), (0, LANES - d_qk))).reshape(MLA_Q_RANK, MLA_HEADS * LANES)
    wkv = w_kv_up.reshape(MLA_KV_RANK, MLA_HEADS, MLA_NOPE + MLA_V)
    wk = jnp.pad(wkv[:, :, :MLA_NOPE], ((0, 0), (0, 0), (0, LANES - MLA_NOPE)))
    wk = wk.reshape(MLA_KV_RANK, MLA_HEADS * LANES)
    wv = wkv[:, :, MLA_NOPE:].reshape(MLA_KV_RANK, MLA_HEADS * MLA_V)
    return wq.astype(BF16), wk.astype(BF16), wv.astype(BF16)


def _tile_sizes(seq):
    pick = lambda want: want if seq % want == 0 else seq
    return dict(in_proj=pick(1024), prep=pick(512), attn=pick(512), ret=pick(512),
                merge=pick(512), mlp=pick(1024), mlp_ff=512)


def kernel(x, attn_norm, w_in, mla_q_norm, mla_kv_norm, mla_w_q_up, mla_w_kv_up, ret_gn_w, ret_gn_b,
           w_branch_mla, w_branch_moba, w_branch_ret, w_out, mlp_norm, w_mlp_up, w_mlp_down, final_norm):
    batch, seq, d = x.shape
    depth = w_in.shape[0]
    tiles = _tile_sizes(seq)
    mla_tab = _rope_tables(seq, MLA_ROPE, ROPE_THETA, LANES, MLA_NOPE)
    moba_tab = _rope_tables(seq, MOBA_HEAD_DIM // PARTIAL_ROPE_DIV, ROPE_THETA, MOBA_HEAD_DIM, 0)
    ret_tab = _rope_tables(seq, RET_QK_DIM, RET_THETA, RET_QK_DIM, 0)
    decay, xi, zeta, gamma_c = _retention_tables()
    row = lambda v: v.reshape(1, -1)

    x2 = x.reshape(batch * seq, d)
    for l in range(depth):
        w_in_p = _pad_in_proj_weight(w_in[l])
        wq, wk, wv = _pad_mla_weights(mla_w_q_up[l], mla_w_kv_up[l])
        p2 = _in_proj(x2, row(attn_norm[l]), w_in_p, moba_tab, ret_tab, seq, tiles["in_proj"])
        p3 = p2.reshape(batch, seq, IN_COLS_PADDED)
        q_mla, k_mla, v_mla = _mla_prep(p3, row(mla_q_norm[l]), row(mla_kv_norm[l]), wq, wk, wv,
                                        mla_tab, tiles["prep"])
        o_mla = _mla_attn(q_mla, k_mla, v_mla, tiles["attn"])
        o_moba = _moba_attn(p3)
        o_ret = _retention(p3, decay, xi, zeta, gamma_c, row(ret_gn_w[l]), row(ret_gn_b[l]), tiles["ret"])
        flat = lambda o: o.reshape(batch * seq, GROUP)
        x2 = _merge(x2, p2, flat(o_mla), flat(o_moba), flat(o_ret),
                    w_branch_mla[l].astype(BF16), w_branch_moba[l].astype(BF16),
                    w_branch_ret[l].astype(BF16), w_out[l].astype(BF16), tiles["merge"])
        x2 = _mlp(x2, row(mlp_norm[l]), w_mlp_up[l].astype(BF16), w_mlp_down[l].astype(BF16),
                  row(final_norm), tiles["mlp"], tiles["mlp_ff"], final_norm=(l == depth - 1))
    return x2.reshape(batch, seq, d)
```

```python
import functools

import jax
import jax.numpy as jnp
from jax import lax
from jax.experimental import pallas as pl
from jax.experimental.pallas import tpu as pltpu

F32 = jnp.float32
BF16 = jnp.bfloat16

D_MODEL = 1024
MLA_HEADS = 8
MLA_Q_RANK = 256
MLA_KV_RANK = 128
MLA_NOPE = 64
MLA_ROPE = 32
MLA_V = 64
MOBA_HEADS = 8
MOBA_HEAD_DIM = 64
MOBA_BLOCK = 256
MOBA_TOPK = 3
RET_HEADS = 4
RET_QK_DIM = 64
RET_V_DIM = 128
RET_CHUNK = 128
RET_THETA = 10000.0
ROPE_THETA = 500000.0
PARTIAL_ROPE_DIV = 4
D_FF = 4 * D_MODEL
NORM_EPS = 1e-6
GN_EPS = 1e-5
NEG_INF = -1e30
LOG2_E = 1.4426950408889634

LANES = 128
SUBLANES = 8
HEAD_PAIR = 2
SCORE_DEPTH = 3
VMEM_LIMIT = 48 * 1024 * 1024

GROUP = 512
G_GATES = 0
G_MLA = 6
G_MOBA_Q, G_MOBA_K, G_MOBA_V = 7, 8, 9
G_RET_QK, G_RET_V, G_RET_G = 10, 11, 12
N_GROUPS = 13
IN_COLS_PADDED = N_GROUPS * GROUP


def _nt_dot(a, b):
    return lax.dot_general(a, b, (((1,), (1,)), ((), ())), preferred_element_type=F32)


def _rms(x, g):
    return x * lax.rsqrt(jnp.mean(x * x, axis=-1, keepdims=True) + NORM_EPS) * g


def _rope_lanes(y, tab_ref, half):
    return (y * tab_ref[0]
            + pltpu.roll(y, LANES - half, 1) * tab_ref[1]
            + pltpu.roll(y, half, 1) * tab_ref[2])


def _in_proj_kernel(x_ref, g_ref, w_ref, moba_tab, ret_tab, o_ref, h_ref):
    j = pl.program_id(1)

    @pl.when(j == 0)
    def _():
        h_ref[...] = _rms(x_ref[...], g_ref[...]).astype(BF16)

    acc = jnp.dot(h_ref[...], w_ref[...], preferred_element_type=F32)
    is_moba = jnp.logical_or(j == G_MOBA_Q, j == G_MOBA_K)
    is_ret = j == G_RET_QK

    @pl.when(is_moba)
    def _():
        for r in range(GROUP // LANES):
            sl = slice(r * LANES, (r + 1) * LANES)
            o_ref[:, sl] = _rope_lanes(acc[:, sl], moba_tab, MOBA_HEAD_DIM // PARTIAL_ROPE_DIV // 2).astype(BF16)

    @pl.when(is_ret)
    def _():
        for r in range(GROUP // LANES):
            sl = slice(r * LANES, (r + 1) * LANES)
            y = _rope_lanes(acc[:, sl], ret_tab, RET_QK_DIM // 2)
            if r >= GROUP // LANES // 2:
                y = y * (RET_QK_DIM ** -0.5)
            o_ref[:, sl] = y.astype(BF16)

    @pl.when(jnp.logical_not(jnp.logical_or(is_moba, is_ret)))
    def _():
        o_ref[...] = acc.astype(BF16)


def _in_proj(x2, g, w, moba_tab, ret_tab, seq, tm):
    tokens = x2.shape[0]
    s_tiles = seq // tm
    return pl.pallas_call(
        _in_proj_kernel,
        grid=(tokens // tm, N_GROUPS),
        in_specs=[
            pl.BlockSpec((tm, D_MODEL), lambda i, j: (i, 0)),
            pl.BlockSpec((1, D_MODEL), lambda i, j: (0, 0)),
            pl.BlockSpec((D_MODEL, GROUP), lambda i, j: (0, j)),
            pl.BlockSpec((3, tm, LANES), lambda i, j: (0, i % s_tiles, 0)),
            pl.BlockSpec((3, tm, LANES), lambda i, j: (0, i % s_tiles, 0)),
        ],
        out_specs=pl.BlockSpec((tm, GROUP), lambda i, j: (i, j)),
        out_shape=jax.ShapeDtypeStruct((tokens, IN_COLS_PADDED), BF16),
        scratch_shapes=[pltpu.VMEM((tm, D_MODEL), BF16)],
        compiler_params=pltpu.CompilerParams(
            dimension_semantics=("parallel", "arbitrary"), vmem_limit_bytes=VMEM_LIMIT),
        name="in_proj",
    )(x2, g, w, moba_tab, ret_tab)


def _mla_prep_kernel(p_ref, qg_ref, kvg_ref, wq_ref, wk_ref, wv_ref, tab, q_ref, k_ref, v_ref):
    blk = p_ref[...]
    cq = blk[:, :MLA_Q_RANK].astype(F32)
    ckv = blk[:, MLA_Q_RANK:MLA_Q_RANK + MLA_KV_RANK].astype(F32)
    kr = blk[:, MLA_Q_RANK + MLA_KV_RANK:].astype(F32)
    cqn = _rms(cq, qg_ref[...]).astype(BF16)
    ckvn = _rms(ckv, kvg_ref[...]).astype(BF16)
    q = jnp.dot(cqn, wq_ref[...], preferred_element_type=F32)
    k = jnp.dot(ckvn, wk_ref[...], preferred_element_type=F32)
    v = jnp.dot(ckvn, wv_ref[...], preferred_element_type=F32)
    kpe = _rope_lanes(kr, tab, MLA_ROPE // 2)
    scale = (MLA_NOPE + MLA_ROPE) ** -0.5 * LOG2_E
    for h in range(MLA_HEADS):
        sl = slice(h * LANES, (h + 1) * LANES)
        q_ref[0, h] = (_rope_lanes(q[:, sl], tab, MLA_ROPE // 2) * scale).astype(BF16)
        k_ref[0, h] = (k[:, sl] + kpe).astype(BF16)
    v_ref[0] = v.astype(BF16)


def _mla_prep(p3, qg, kvg, wq, wk, wv, tab, tm):
    batch, seq, _ = p3.shape
    hs = jax.ShapeDtypeStruct((batch, MLA_HEADS, seq, LANES), BF16)
    const = lambda shape: pl.BlockSpec(shape, lambda b, i: (0,) * len(shape))
    return pl.pallas_call(
        _mla_prep_kernel,
        grid=(batch, seq // tm),
        in_specs=[
            pl.BlockSpec((None, tm, GROUP), lambda b, i: (b, i, G_MLA)),
            const((1, MLA_Q_RANK)), const((1, MLA_KV_RANK)),
            const((MLA_Q_RANK, MLA_HEADS * LANES)), const((MLA_KV_RANK, MLA_HEADS * LANES)),
            const((MLA_KV_RANK, MLA_HEADS * MLA_V)),
            pl.BlockSpec((3, tm, LANES), lambda b, i: (0, i, 0)),
        ],
        out_specs=[
            pl.BlockSpec((1, MLA_HEADS, tm, LANES), lambda b, i: (b, 0, i, 0)),
            pl.BlockSpec((1, MLA_HEADS, tm, LANES), lambda b, i: (b, 0, i, 0)),
            pl.BlockSpec((1, tm, MLA_HEADS * MLA_V), lambda b, i: (b, i, 0)),
        ],
        out_shape=[hs, hs, jax.ShapeDtypeStruct((batch, seq, MLA_HEADS * MLA_V), BF16)],
        compiler_params=pltpu.CompilerParams(
            dimension_semantics=("parallel", "parallel"), vmem_limit_bytes=VMEM_LIMIT),
        name="mla_prep",
    )(p3, qg, kvg, wq, wk, wv, tab)


_HEADS = range(HEAD_PAIR)
_ROWS = range(2)


def _ones_row(h, head_dim):
    return head_dim if h == 0 else 0


def _fill_v_transposed(v_ref, vt_ref, t, head_dim):
    sub = lax.broadcasted_iota(jnp.int32, (LANES, t), 0)
    for j in range(vt_ref.shape[1]):
        vt = v_ref[0, j * t:(j + 1) * t, :].astype(F32).T
        for h in _HEADS:
            own = (sub >= h * head_dim) & (sub < (h + 1) * head_dim)
            fill = jnp.where(sub == _ones_row(h, head_dim), 1.0, 0.0)
            vt_ref[h, j] = jnp.where(own, vt, fill).astype(BF16)


def _fill_causal_cap(cap_ref, t):
    key = lax.broadcasted_iota(jnp.int32, (t, t), 0)
    query = lax.broadcasted_iota(jnp.int32, (t, t), 1)
    cap_ref[...] = jnp.where(key <= query, jnp.inf, NEG_INF)


def _attend_rows(g, n_tiles, t, load_q, load_k, vt_ref, write_out, scratch, head_dim, row_cap=None):
    causal_cap_ref, bufs, m_ref, acc_ref = scratch
    depth = bufs.shape[0]
    assert (n_tiles - 1) % depth == 0
    q_tile = [g, n_tiles - 1 - g]
    m_ref[...] = jnp.full_like(m_ref, -jnp.inf)
    acc_ref[...] = jnp.zeros_like(acc_ref)

    def locate(n):
        row = (n > q_tile[0]).astype(jnp.int32)
        kj = jnp.where(row == 0, q_tile[0] - n, n - (q_tile[0] + 1))
        return row, jnp.where(row == 0, q_tile[0], q_tile[1]), kj

    def issue_scores(n, slot):
        row, qt, kj = locate(jnp.minimum(n, n_tiles))
        ks, qs = load_k(kj), load_q(row, qt)
        for h in _HEADS:
            bufs[slot, h] = _nt_dot(ks[h], qs[h])

    def softmax_step(n, slot, diagonal=False):
        row, qt, kj = locate(n)
        chain = [HEAD_PAIR * row + h for h in _HEADS]
        s = [bufs[slot, h] for h in _HEADS]
        if diagonal:
            s = [jnp.minimum(s[h], causal_cap_ref[...]) for h in _HEADS]
        elif row_cap is not None:
            s = [jnp.minimum(s[h], row_cap(row, h, kj)) for h in _HEADS]
        m_old = [m_ref[chain[h]] for h in _HEADS]
        m_new = [jnp.maximum(m_old[h], jnp.max(s[h], axis=0, keepdims=True)) for h in _HEADS]
        alpha = [jnp.exp2(m_old[h] - m_new[h]) for h in _HEADS]
        p = [jnp.exp2(s[h] - m_new[h]).astype(BF16) for h in _HEADS]
        pv = [jnp.dot(vt_ref[h, kj], p[h], preferred_element_type=F32) for h in _HEADS]
        for h in _HEADS:
            m_ref[chain[h]] = m_new[h]
            acc_ref[chain[h]] = alpha[h] * acc_ref[chain[h]] + pv[h]

    def multi_step(d, carry):
        first = depth * d + 1
        issue_scores(first + depth - 1, 0)
        for i in range(depth):
            softmax_step(first + i, (1 + i) % depth)
            if i < depth - 1:
                issue_scores(first + i + depth, (1 + i) % depth)
        return carry

    for n in range(depth):
        issue_scores(n, n)
    softmax_step(0, 0, diagonal=True)
    lax.fori_loop(0, (n_tiles - 1) // depth, multi_step, 0)
    softmax_step(n_tiles, n_tiles % depth, diagonal=True)

    sub = lax.broadcasted_iota(jnp.int32, acc_ref.shape[1:], 0)
    for row in _ROWS:
        outs = []
        for h in _HEADS:
            acc = acc_ref[HEAD_PAIR * row + h]
            ones_row = _ones_row(h, head_dim)
            outs.append(acc * (1.0 / acc[ones_row:ones_row + 1, :]))
        ot = jnp.where(sub < head_dim, outs[0], outs[1])
        write_out(q_tile[row], ot.T.astype(BF16))


def _attn_scratch(seq, t):
    chains = HEAD_PAIR * len(_ROWS)
    return [pltpu.VMEM((HEAD_PAIR, seq // t, LANES, t), BF16),
            pltpu.VMEM((t, t), F32),
            pltpu.VMEM((SCORE_DEPTH, HEAD_PAIR, t, t), F32),
            pltpu.VMEM((chains, 1, t), F32),
            pltpu.VMEM((chains, LANES, t), F32)]


def _tile_rows(i, t):
    return pl.ds(pl.multiple_of(i * t, t), t)


def _mla_attn_kernel(q_ref, k_ref, v_ref, o_ref, vt_ref, *scratch, t, n_tiles):
    g = pl.program_id(2)

    @pl.when(g == 0)
    def _():
        _fill_v_transposed(v_ref, vt_ref, t, MLA_V)
        _fill_causal_cap(scratch[0], t)

    def load_q(row, qt):
        return [q_ref[0, h, _tile_rows(qt, t), :] for h in _HEADS]

    def load_k(kj):
        return [k_ref[0, h, _tile_rows(kj, t), :] for h in _HEADS]

    def write_out(qt, tile):
        o_ref[0, _tile_rows(qt, t), :] = tile

    _attend_rows(g, n_tiles, t, load_q, load_k, vt_ref, write_out, scratch, MLA_V)


def _mla_attn(q, k, v, t):
    batch, heads, seq, _ = q.shape
    n_tiles = seq // t
    assert n_tiles % 2 == 0
    return pl.pallas_call(
        functools.partial(_mla_attn_kernel, t=t, n_tiles=n_tiles),
        grid=(batch, heads // HEAD_PAIR, n_tiles // 2),
        in_specs=[
            pl.BlockSpec((1, HEAD_PAIR, seq, LANES), lambda b, h, g: (b, h, 0, 0)),
            pl.BlockSpec((1, HEAD_PAIR, seq, LANES), lambda b, h, g: (b, h, 0, 0)),
            pl.BlockSpec((1, seq, LANES), lambda b, h, g: (b, 0, h)),
        ],
        out_specs=pl.BlockSpec((1, seq, LANES), lambda b, h, g: (b, 0, h)),
        out_shape=jax.ShapeDtypeStruct((batch, seq, heads * MLA_V), BF16),
        scratch_shapes=_attn_scratch(seq, t),
        compiler_params=pltpu.CompilerParams(
            dimension_semantics=("parallel", "parallel", "arbitrary"), vmem_limit_bytes=VMEM_LIMIT),
        name="mla_attn",
    )(q, k, v)


def _moba_kernel(q_ref, k_ref, v_ref, o_ref, kmean_ref, qh_ref, cap_ref, vt_ref, *scratch, nb):
    t = MOBA_BLOCK
    g = pl.program_id(2)

    @pl.when(g == 0)
    def _():
        _fill_v_transposed(v_ref, vt_ref, t, MOBA_HEAD_DIM)
        _fill_causal_cap(scratch[0], t)
        kmean_ref[...] = jnp.zeros_like(kmean_ref)
        for j in range(nb):
            kb = k_ref[0, j * t:(j + 1) * t, :].astype(F32)
            kmean_ref[j:j + 1, :] = jnp.mean(kb, axis=0, keepdims=True)

    lane = lax.broadcasted_iota(jnp.int32, (t, LANES), 1)
    lane_k = lax.broadcasted_iota(jnp.int32, (LANES, LANES), 1)
    blk_id = lax.broadcasted_iota(jnp.int32, (nb, t), 0)
    nb_rows = -(-nb // SUBLANES) * SUBLANES
    for row, qi in enumerate((g, nb - 1 - g)):
        n_sel = jnp.minimum(qi, MOBA_TOPK).astype(F32)
        q_pair = q_ref[0, _tile_rows(qi, t), :]
        for hh in _HEADS:
            in_head = (lane >= hh * MOBA_HEAD_DIM) & (lane < (hh + 1) * MOBA_HEAD_DIM)
            q_gate = jnp.where(in_head, q_pair, jnp.zeros_like(q_pair))
            qh_ref[row, hh] = (q_gate.astype(F32) * (MOBA_HEAD_DIM ** -0.5 * LOG2_E)).astype(BF16)
            in_head_k = (lane_k >= hh * MOBA_HEAD_DIM) & (lane_k < (hh + 1) * MOBA_HEAD_DIM)
            km = jnp.where(in_head_k, kmean_ref[...], 0.0)[:nb_rows]
            gate = lax.dot_general(km, q_gate.astype(F32), (((1,), (1,)), ((), ())),
                                   precision=lax.Precision.HIGHEST, preferred_element_type=F32)[:nb]
            gate = jnp.where(blk_id < qi, gate, NEG_INF)
            rank = jnp.zeros((nb, t), F32)
            for jp in range(nb):
                gj = gate[jp:jp + 1, :]
                beats = (gj > gate) | ((gj == gate) & (blk_id > jp))
                rank = rank + jnp.where(beats, 1.0, 0.0)
            keep = ((rank < n_sel) & (blk_id < qi)) | (blk_id == qi)
            cap = jnp.where(keep, jnp.inf, NEG_INF)
            for j in range(nb):
                cap_ref[row, hh, j] = cap[j:j + 1, :]

    def load_q(row, qt):
        return [qh_ref[row, h] for h in _HEADS]

    def load_k(kj):
        return [k_ref[0, _tile_rows(kj, t), :]] * HEAD_PAIR

    def write_out(qt, tile):
        o_ref[0, _tile_rows(qt, t), :] = tile

    _attend_rows(g, nb, t, load_q, load_k, vt_ref, write_out, scratch, MOBA_HEAD_DIM,
                 row_cap=lambda row, h, kj: cap_ref[row, h, kj])


def _moba_attn(p3):
    batch, seq, _ = p3.shape
    t = MOBA_BLOCK
    nb = seq // t
    assert nb % 2 == 0
    per_group = GROUP // LANES
    return pl.pallas_call(
        functools.partial(_moba_kernel, nb=nb),
        grid=(batch, MOBA_HEADS // HEAD_PAIR, nb // 2),
        in_specs=[
            pl.BlockSpec((1, seq, LANES), lambda b, h, g: (b, 0, G_MOBA_Q * per_group + h)),
            pl.BlockSpec((1, seq, LANES), lambda b, h, g: (b, 0, G_MOBA_K * per_group + h)),
            pl.BlockSpec((1, seq, LANES), lambda b, h, g: (b, 0, G_MOBA_V * per_group + h)),
        ],
        out_specs=pl.BlockSpec((1, seq, LANES), lambda b, h, g: (b, 0, h)),
        out_shape=jax.ShapeDtypeStruct((batch, seq, MOBA_HEADS * MOBA_HEAD_DIM), BF16),
        scratch_shapes=[pltpu.VMEM((LANES, LANES), F32),
                        pltpu.VMEM((len(_ROWS), HEAD_PAIR, t, LANES), BF16),
                        pltpu.VMEM((len(_ROWS), HEAD_PAIR, nb, 1, t), F32)]
                       + _attn_scratch(seq, t),
        compiler_params=pltpu.CompilerParams(
            dimension_semantics=("parallel", "parallel", "arbitrary"), vmem_limit_bytes=VMEM_LIMIT),
        name="moba_attn",
    )(p3, p3, p3)


def _retention_kernel(qk_ref, v_ref, g_ref, decay_ref, xi_ref, zeta_ref, gamma_ref, gnw_ref, gnb_ref,
                      o_ref, state_ref, *, chunks):
    c = RET_CHUNK

    @pl.when(pl.program_id(1) == 0)
    def _():
        state_ref[...] = jnp.zeros_like(state_ref)

    lane = lax.broadcasted_iota(jnp.int32, (c, LANES), 1)
    k_off = RET_HEADS * RET_QK_DIM
    for n in range(chunks):
        rows = slice(n * c, (n + 1) * c)
        for h in range(RET_HEADS):
            pair, half = divmod(h, HEAD_PAIR)
            in_head = (lane >= half * RET_QK_DIM) & (lane < (half + 1) * RET_QK_DIM)
            q = jnp.where(in_head, qk_ref[0, rows, pair * LANES:(pair + 1) * LANES].astype(F32), 0.0)
            k = jnp.where(in_head, qk_ref[0, rows, k_off + pair * LANES:k_off + (pair + 1) * LANES].astype(F32), 0.0)
            vs = slice(h * RET_V_DIM, (h + 1) * RET_V_DIM)
            v = v_ref[0, rows, vs]
            state = state_ref[h]
            scores = _nt_dot(q.astype(BF16), k.astype(BF16)) * decay_ref[h]
            inner = jnp.dot(scores.astype(BF16), v, preferred_element_type=F32)
            cross = jnp.dot((q * xi_ref[h]).astype(BF16), state.astype(BF16), preferred_element_type=F32)
            kv = lax.dot_general((k * zeta_ref[h]).astype(BF16), v, (((0,), (0,)), ((), ())),
                                 preferred_element_type=F32)
            state_ref[h] = gamma_ref[h] * state + kv
            o = inner + cross
            mu = jnp.mean(o, axis=-1, keepdims=True)
            d = o - mu
            var = jnp.mean(d * d, axis=-1, keepdims=True)
            o = d * lax.rsqrt(var + GN_EPS) * gnw_ref[:, vs] + gnb_ref[:, vs]
            g = g_ref[0, rows, vs].astype(F32)
            o_ref[0, rows, vs] = (g * jax.nn.sigmoid(g) * o).astype(BF16)


def _retention(p3, decay, xi, zeta, gamma, gnw, gnb, tr):
    batch, seq, _ = p3.shape
    width = RET_HEADS * RET_V_DIM
    const = lambda shape: pl.BlockSpec(shape, lambda b, i: (0,) * len(shape))
    return pl.pallas_call(
        functools.partial(_retention_kernel, chunks=tr // RET_CHUNK),
        grid=(batch, seq // tr),
        in_specs=[
            pl.BlockSpec((1, tr, GROUP), lambda b, i: (b, i, G_RET_QK)),
            pl.BlockSpec((1, tr, GROUP), lambda b, i: (b, i, G_RET_V)),
            pl.BlockSpec((1, tr, GROUP), lambda b, i: (b, i, G_RET_G)),
            const((RET_HEADS, RET_CHUNK, RET_CHUNK)),
            const((RET_HEADS, RET_CHUNK, LANES)), const((RET_HEADS, RET_CHUNK, LANES)),
            const((RET_HEADS, 1, LANES)),
            const((1, width)), const((1, width)),
        ],
        out_specs=pl.BlockSpec((1, tr, width), lambda b, i: (b, i, 0)),
        out_shape=jax.ShapeDtypeStruct((batch, seq, width), BF16),
        scratch_shapes=[pltpu.VMEM((RET_HEADS, LANES, RET_V_DIM), F32)],
        compiler_params=pltpu.CompilerParams(
            dimension_semantics=("parallel", "arbitrary"), vmem_limit_bytes=VMEM_LIMIT),
        name="retention",
    )(p3, p3, p3, decay, xi, zeta, gamma, gnw, gnb)


def _merge_kernel(x_ref, gates_ref, oa_ref, ob_ref, oc_ref, wa_ref, wb_ref, wc_ref, wo_ref, o_ref):
    merged = None
    for idx, (o_in, w) in enumerate(((oa_ref, wa_ref), (ob_ref, wb_ref), (oc_ref, wc_ref))):
        gate = gates_ref[:, idx * D_MODEL:(idx + 1) * D_MODEL].astype(F32)
        term = jax.nn.sigmoid(gate) * jnp.dot(o_in[...], w[...], preferred_element_type=F32)
        merged = term if merged is None else merged + term
    o_ref[...] = x_ref[...] + jnp.dot(merged.astype(BF16), wo_ref[...], preferred_element_type=F32)


def _merge(x2, p2, oa, ob, oc, wa, wb, wc, wo, tm):
    tokens = x2.shape[0]
    row = lambda w: pl.BlockSpec((tm, w), lambda i: (i, 0))
    const = lambda shape: pl.BlockSpec(shape, lambda i: (0, 0))
    return pl.pallas_call(
        _merge_kernel,
        grid=(tokens // tm,),
        in_specs=[row(D_MODEL), row(3 * D_MODEL), row(GROUP), row(GROUP), row(GROUP),
                  const((GROUP, D_MODEL)), const((GROUP, D_MODEL)), const((GROUP, D_MODEL)),
                  const((D_MODEL, D_MODEL))],
        out_specs=row(D_MODEL),
        out_shape=jax.ShapeDtypeStruct((tokens, D_MODEL), F32),
        compiler_params=pltpu.CompilerParams(
            dimension_semantics=("parallel",), vmem_limit_bytes=VMEM_LIMIT),
        name="merge",
    )(x2, p2, oa, ob, oc, wa, wb, wc, wo)


def _mlp_kernel(x_ref, g_ref, wu_ref, wd_ref, fg_ref, o_ref, h_ref, *, final_norm):
    f = pl.program_id(1)

    @pl.when(f == 0)
    def _():
        x = x_ref[...]
        h_ref[...] = _rms(x, g_ref[...]).astype(BF16)
        o_ref[...] = x

    u = jnp.maximum(jnp.dot(h_ref[...], wu_ref[...], preferred_element_type=F32), 0.0)
    o_ref[...] += jnp.dot((u * u).astype(BF16), wd_ref[...], preferred_element_type=F32)

    if final_norm:
        @pl.when(f == pl.num_programs(1) - 1)
        def _():
            o_ref[...] = _rms(o_ref[...], fg_ref[...])


def _mlp(x2, g, wu, wd, fg, tm, tf, final_norm):
    tokens = x2.shape[0]
    return pl.pallas_call(
        functools.partial(_mlp_kernel, final_norm=final_norm),
        grid=(tokens // tm, D_FF // tf),
        in_specs=[
            pl.BlockSpec((tm, D_MODEL), lambda i, f: (i, 0)),
            pl.BlockSpec((1, D_MODEL), lambda i, f: (0, 0)),
            pl.BlockSpec((D_MODEL, tf), lambda i, f: (0, f)),
            pl.BlockSpec((tf, D_MODEL), lambda i, f: (f, 0)),
            pl.BlockSpec((1, D_MODEL), lambda i, f: (0, 0)),
        ],
        out_specs=pl.BlockSpec((tm, D_MODEL), lambda i, f: (i, 0)),
        out_shape=jax.ShapeDtypeStruct((tokens, D_MODEL), F32),
        scratch_shapes=[pltpu.VMEM((tm, D_MODEL), BF16)],
        compiler_params=pltpu.CompilerParams(
            dimension_semantics=("parallel", "arbitrary"), vmem_limit_bytes=VMEM_LIMIT),
        name="mlp",
    )(x2, g, wu, wd, fg)


def _rope_cos_sin(seq, dim, theta):
    inv = 1.0 / (theta ** (jnp.arange(0, dim, 2, dtype=F32) / dim))
    ang = jnp.arange(seq, dtype=F32)[:, None] * inv[None, :]
    return jnp.cos(ang), jnp.sin(ang)


def _rope_tables(seq, dim, theta, period, offset):
    cos, sin = _rope_cos_sin(seq, dim, theta)
    half = dim // 2
    lo, hi = slice(offset, offset + half), slice(offset + half, offset + dim)
    c = jnp.ones((seq, period), F32).at[:, lo].set(cos).at[:, hi].set(cos)
    s_next = jnp.zeros((seq, period), F32).at[:, lo].set(-sin)
    s_prev = jnp.zeros((seq, period), F32).at[:, hi].set(sin)
    return jnp.tile(jnp.stack([c, s_next, s_prev]), (1, 1, LANES // period))


def _retention_tables():
    c = RET_CHUNK
    log_gamma = jnp.log(1.0 - 2.0 ** (-5.0 - jnp.arange(RET_HEADS, dtype=F32)))
    pos = jnp.arange(c, dtype=F32)
    diff = pos[:, None] - pos[None, :]
    decay = jnp.where(diff >= 0, jnp.exp(log_gamma[:, None, None] * diff), 0.0)
    xi = jnp.exp(log_gamma[:, None] * (pos + 1.0))
    zeta = jnp.exp(log_gamma[:, None] * (c - 1.0 - pos))
    gamma_c = jnp.exp(log_gamma * c)
    bcast = lambda t: jnp.broadcast_to(t[..., None], t.shape + (LANES,))
    return decay, bcast(xi), bcast(zeta), bcast(gamma_c[:, None])


def _pad_in_proj_weight(w):
    sizes = (MLA_Q_RANK, MLA_KV_RANK, MLA_ROPE, 512, 512, 512, 256, 256, 512, 512, D_MODEL, D_MODEL, D_MODEL)
    parts, start = [], 0
    for sz in sizes:
        parts.append(w[:, start:start + sz])
        start += sz
    cq, ckv, kr, mq, mk, mv, rq, rk, rv, rg, g_mla, g_moba, g_ret = parts
    zeros = lambda n: jnp.zeros((w.shape[0], n), w.dtype)
    kr_pad = jnp.concatenate([zeros(MLA_NOPE), kr, zeros(LANES - MLA_NOPE - MLA_ROPE)], axis=1)
    return jnp.concatenate([g_mla, g_moba, g_ret, cq, ckv, kr_pad, mq, mk, mv, rq, rk, rv, rg],
                           axis=1).astype(BF16)


def _pad_mla_weights(w_q_up, w_kv_up):
    d_qk = MLA_NOPE + MLA_ROPE
    wq = w_q_up.reshape(MLA_Q_RANK, MLA_HEADS, d_qk)
    wq = jnp.pad(wq, ((0, 0), (0, 0), (0, LANES - d_qk))).reshape(MLA_Q_RANK, MLA_HEADS * LANES)
    wkv = w_kv_up.reshape(MLA_KV_RANK, MLA_HEADS, MLA_NOPE + MLA_V)
    wk = jnp.pad(wkv[:, :, :MLA_NOPE], ((0, 0), (0, 0), (0, LANES - MLA_NOPE)))
    wk = wk.reshape(MLA_KV_RANK, MLA_HEADS * LANES)
    wv = wkv[:, :, MLA_NOPE:].reshape(MLA_KV_RANK, MLA_HEADS * MLA_V)
    return wq.astype(BF16), wk.astype(BF16), wv.astype(BF16)


def _tile_sizes(seq):
    pick = lambda want: want if seq % want == 0 else seq
    return dict(in_proj=pick(1024), prep=pick(512), attn=pick(256), ret=pick(512),
                merge=pick(512), mlp=pick(1024), mlp_ff=512)


def kernel(x, attn_norm, w_in, mla_q_norm, mla_kv_norm, mla_w_q_up, mla_w_kv_up, ret_gn_w, ret_gn_b,
           w_branch_mla, w_branch_moba, w_branch_ret, w_out, mlp_norm, w_mlp_up, w_mlp_down, final_norm):
    batch, seq, d = x.shape
    depth = w_in.shape[0]
    tiles = _tile_sizes(seq)
    mla_tab = _rope_tables(seq, MLA_ROPE, ROPE_THETA, LANES, MLA_NOPE)
    moba_tab = _rope_tables(seq, MOBA_HEAD_DIM // PARTIAL_ROPE_DIV, ROPE_THETA, MOBA_HEAD_DIM, 0)
    ret_tab = _rope_tables(seq, RET_QK_DIM, RET_THETA, RET_QK_DIM, 0)
    decay, xi, zeta, gamma_c = _retention_tables()
    row = lambda v: v.reshape(1, -1)

    x2 = x.reshape(batch * seq, d)
    for l in range(depth):
        w_in_p = _pad_in_proj_weight(w_in[l])
        wq, wk, wv = _pad_mla_weights(mla_w_q_up[l], mla_w_kv_up[l])
        p2 = _in_proj(x2, row(attn_norm[l]), w_in_p, moba_tab, ret_tab, seq, tiles["in_proj"])
        p3 = p2.reshape(batch, seq, IN_COLS_PADDED)
        q_mla, k_mla, v_mla = _mla_prep(p3, row(mla_q_norm[l]), row(mla_kv_norm[l]), wq, wk, wv,
                                        mla_tab, tiles["prep"])
        o_mla = _mla_attn(q_mla, k_mla, v_mla, tiles["attn"])
        o_moba = _moba_attn(p3)
        o_ret = _retention(p3, decay, xi, zeta, gamma_c, row(ret_gn_w[l]), row(ret_gn_b[l]), tiles["ret"])
        flat = lambda o: o.reshape(batch * seq, GROUP)
        x2 = _merge(x2, p2, flat(o_mla), flat(o_moba), flat(o_ret),
                    w_branch_mla[l].astype(BF16), w_branch_moba[l].astype(BF16),
                    w_branch_ret[l].astype(BF16), w_out[l].astype(BF16), tiles["merge"])
        x2 = _mlp(x2, row(mlp_norm[l]), w_mlp_up[l].astype(BF16), w_mlp_down[l].astype(BF16),
                  row(final_norm), tiles["mlp"], tiles["mlp_ff"], final_norm=(l == depth - 1))
    return x2.reshape(batch, seq, d)
```

```python
import functools

import jax
import jax.numpy as jnp
from jax import lax
from jax.experimental import pallas as pl
from jax.experimental.pallas import tpu as pltpu

F32 = jnp.float32
BF16 = jnp.bfloat16

D_MODEL = 1024
MLA_HEADS = 8
MLA_Q_RANK = 256
MLA_KV_RANK = 128
MLA_NOPE = 64
MLA_ROPE = 32
MLA_V = 64
MOBA_HEADS = 8
MOBA_HEAD_DIM = 64
MOBA_BLOCK = 256
MOBA_TOPK = 3
RET_HEADS = 4
RET_QK_DIM = 64
RET_V_DIM = 128
RET_CHUNK = 128
RET_THETA = 10000.0
ROPE_THETA = 500000.0
PARTIAL_ROPE_DIV = 4
D_FF = 4 * D_MODEL
NORM_EPS = 1e-6
GN_EPS = 1e-5
NEG_INF = -1e30
LOG2_E = 1.4426950408889634

LANES = 128
SUBLANES = 8
HEAD_PAIR = 2
SCORE_DEPTH = 3
VMEM_LIMIT = 48 * 1024 * 1024

GROUP = 512
G_GATES = 0
G_MLA = 6
G_MOBA_Q, G_MOBA_K, G_MOBA_V = 7, 8, 9
G_RET_QK, G_RET_V, G_RET_G = 10, 11, 12
N_GROUPS = 13
IN_COLS_PADDED = N_GROUPS * GROUP


def _nt_dot(a, b):
    return lax.dot_general(a, b, (((1,), (1,)), ((), ())), preferred_element_type=F32)


def _rms(x, g):
    return x * lax.rsqrt(jnp.mean(x * x, axis=-1, keepdims=True) + NORM_EPS) * g


def _rope_lanes(y, tab_ref, half):
    return (y * tab_ref[0]
            + pltpu.roll(y, LANES - half, 1) * tab_ref[1]
            + pltpu.roll(y, half, 1) * tab_ref[2])


def _in_proj_kernel(x_ref, g_ref, w_ref, moba_tab, ret_tab, o_ref, h_ref):
    h_ref[...] = _rms(x_ref[...], g_ref[...]).astype(BF16)
    lane_tiles = GROUP // LANES
    for j in range(N_GROUPS):
        acc = jnp.dot(h_ref[...], w_ref[:, j * GROUP:(j + 1) * GROUP], preferred_element_type=F32)
        for r in range(lane_tiles):
            y = acc[:, r * LANES:(r + 1) * LANES]
            if j in (G_MOBA_Q, G_MOBA_K):
                y = _rope_lanes(y, moba_tab, MOBA_HEAD_DIM // PARTIAL_ROPE_DIV // 2)
            elif j == G_RET_QK:
                y = _rope_lanes(y, ret_tab, RET_QK_DIM // 2)
                if r >= lane_tiles // 2:
                    y = y * (RET_QK_DIM ** -0.5)
            o_ref[:, j * GROUP + r * LANES:j * GROUP + (r + 1) * LANES] = y.astype(BF16)


def _in_proj(x2, g, w, moba_tab, ret_tab, seq, tm):
    tokens = x2.shape[0]
    s_tiles = seq // tm
    return pl.pallas_call(
        _in_proj_kernel,
        grid=(tokens // tm,),
        in_specs=[
            pl.BlockSpec((tm, D_MODEL), lambda i: (i, 0)),
            pl.BlockSpec((1, D_MODEL), lambda i: (0, 0)),
            pl.BlockSpec((D_MODEL, IN_COLS_PADDED), lambda i: (0, 0), pipeline_mode=pl.Buffered(1)),
            pl.BlockSpec((3, tm, LANES), lambda i: (0, i % s_tiles, 0)),
            pl.BlockSpec((3, tm, LANES), lambda i: (0, i % s_tiles, 0)),
        ],
        out_specs=pl.BlockSpec((tm, IN_COLS_PADDED), lambda i: (i, 0)),
        out_shape=jax.ShapeDtypeStruct((tokens, IN_COLS_PADDED), BF16),
        scratch_shapes=[pltpu.VMEM((tm, D_MODEL), BF16)],
        compiler_params=pltpu.CompilerParams(
            dimension_semantics=("parallel",), vmem_limit_bytes=VMEM_LIMIT),
        name="in_proj",
    )(x2, g, w, moba_tab, ret_tab)


def _mla_prep_kernel(p_ref, qg_ref, kvg_ref, wq_ref, wk_ref, wv_ref, tab, q_ref, k_ref, v_ref):
    blk = p_ref[...]
    cq = blk[:, :MLA_Q_RANK].astype(F32)
    ckv = blk[:, MLA_Q_RANK:MLA_Q_RANK + MLA_KV_RANK].astype(F32)
    kr = blk[:, MLA_Q_RANK + MLA_KV_RANK:].astype(F32)
    cqn = _rms(cq, qg_ref[...]).astype(BF16)
    ckvn = _rms(ckv, kvg_ref[...]).astype(BF16)
    q = jnp.dot(cqn, wq_ref[...], preferred_element_type=F32)
    k = jnp.dot(ckvn, wk_ref[...], preferred_element_type=F32)
    v = jnp.dot(ckvn, wv_ref[...], preferred_element_type=F32)
    kpe = _rope_lanes(kr, tab, MLA_ROPE // 2)
    scale = (MLA_NOPE + MLA_ROPE) ** -0.5 * LOG2_E
    for h in range(MLA_HEADS):
        sl = slice(h * LANES, (h + 1) * LANES)
        q_ref[0, h] = (_rope_lanes(q[:, sl], tab, MLA_ROPE // 2) * scale).astype(BF16)
        k_ref[0, h] = (k[:, sl] + kpe).astype(BF16)
    v_ref[0] = v.astype(BF16)


def _mla_prep(p3, qg, kvg, wq, wk, wv, tab, tm):
    batch, seq, _ = p3.shape
    hs = jax.ShapeDtypeStruct((batch, MLA_HEADS, seq, LANES), BF16)
    const = lambda shape: pl.BlockSpec(shape, lambda b, i: (0,) * len(shape))
    return pl.pallas_call(
        _mla_prep_kernel,
        grid=(batch, seq // tm),
        in_specs=[
            pl.BlockSpec((None, tm, GROUP), lambda b, i: (b, i, G_MLA)),
            const((1, MLA_Q_RANK)), const((1, MLA_KV_RANK)),
            const((MLA_Q_RANK, MLA_HEADS * LANES)), const((MLA_KV_RANK, MLA_HEADS * LANES)),
            const((MLA_KV_RANK, MLA_HEADS * MLA_V)),
            pl.BlockSpec((3, tm, LANES), lambda b, i: (0, i, 0)),
        ],
        out_specs=[
            pl.BlockSpec((1, MLA_HEADS, tm, LANES), lambda b, i: (b, 0, i, 0)),
            pl.BlockSpec((1, MLA_HEADS, tm, LANES), lambda b, i: (b, 0, i, 0)),
            pl.BlockSpec((1, tm, MLA_HEADS * MLA_V), lambda b, i: (b, i, 0)),
        ],
        out_shape=[hs, hs, jax.ShapeDtypeStruct((batch, seq, MLA_HEADS * MLA_V), BF16)],
        compiler_params=pltpu.CompilerParams(
            dimension_semantics=("parallel", "parallel"), vmem_limit_bytes=VMEM_LIMIT),
        name="mla_prep",
    )(p3, qg, kvg, wq, wk, wv, tab)


_HEADS = range(HEAD_PAIR)
_ROWS = range(2)


def _ones_row(h, head_dim):
    return head_dim if h == 0 else 0


def _fill_v_transposed(v_ref, vt_ref, t, head_dim):
    sub = lax.broadcasted_iota(jnp.int32, (LANES, t), 0)
    for j in range(vt_ref.shape[1]):
        vt = v_ref[0, j * t:(j + 1) * t, :].astype(F32).T
        for h in _HEADS:
            own = (sub >= h * head_dim) & (sub < (h + 1) * head_dim)
            fill = jnp.where(sub == _ones_row(h, head_dim), 1.0, 0.0)
            vt_ref[h, j] = jnp.where(own, vt, fill).astype(BF16)


def _fill_causal_cap(cap_ref, t):
    key = lax.broadcasted_iota(jnp.int32, (t, t), 0)
    query = lax.broadcasted_iota(jnp.int32, (t, t), 1)
    cap_ref[...] = jnp.where(key <= query, jnp.inf, NEG_INF)


def _attend_rows(g, n_tiles, t, load_qt, load_k, vt_ref, write_out, scratch, head_dim, row_cap=None):
    causal_cap_ref, bufs, m_ref, acc_ref = scratch
    depth = bufs.shape[0]
    assert (n_tiles - 1) % depth == 0
    q_tile = [g, n_tiles - 1 - g]
    m_ref[...] = jnp.full_like(m_ref, -jnp.inf)
    acc_ref[...] = jnp.zeros_like(acc_ref)

    def locate(n):
        row = (n > q_tile[0]).astype(jnp.int32)
        return row, jnp.where(row == 0, q_tile[0] - n, n - (q_tile[0] + 1))

    def issue_scores(n, slot):
        row, kj = locate(jnp.minimum(n, n_tiles))
        ks, qts = load_k(kj), load_qt(row)
        for h in _HEADS:
            bufs[slot, h] = jnp.dot(ks[h], qts[h], preferred_element_type=F32)

    def softmax_step(n, slot, diagonal=False):
        row, kj = locate(n)
        chain = [HEAD_PAIR * row + h for h in _HEADS]
        s = [bufs[slot, h] for h in _HEADS]
        if diagonal:
            s = [jnp.minimum(s[h], causal_cap_ref[...]) for h in _HEADS]
        elif row_cap is not None:
            s = [jnp.minimum(s[h], row_cap(row, h, kj)) for h in _HEADS]
        m_old = [m_ref[chain[h]] for h in _HEADS]
        m_new = [jnp.maximum(m_old[h], jnp.max(s[h], axis=0, keepdims=True)) for h in _HEADS]
        alpha = [jnp.exp2(m_old[h] - m_new[h]) for h in _HEADS]
        p = [jnp.exp2(s[h] - m_new[h]).astype(BF16) for h in _HEADS]
        pv = [jnp.dot(vt_ref[h, kj], p[h], preferred_element_type=F32) for h in _HEADS]
        for h in _HEADS:
            m_ref[chain[h]] = m_new[h]
            acc_ref[chain[h]] = alpha[h] * acc_ref[chain[h]] + pv[h]

    def multi_step(d, carry):
        first = depth * d + 1
        issue_scores(first + depth - 1, 0)
        for i in range(depth):
            softmax_step(first + i, (1 + i) % depth)
            if i < depth - 1:
                issue_scores(first + i + depth, (1 + i) % depth)
        return carry

    for n in range(depth):
        issue_scores(n, n)
    softmax_step(0, 0, diagonal=True)
    lax.fori_loop(0, (n_tiles - 1) // depth, multi_step, 0)
    softmax_step(n_tiles, n_tiles % depth, diagonal=True)

    sub = lax.broadcasted_iota(jnp.int32, acc_ref.shape[1:], 0)
    for row in _ROWS:
        outs = []
        for h in _HEADS:
            acc = acc_ref[HEAD_PAIR * row + h]
            ones_row = _ones_row(h, head_dim)
            outs.append(acc * (1.0 / acc[ones_row:ones_row + 1, :]))
        ot = jnp.where(sub < head_dim, outs[0], outs[1])
        write_out(q_tile[row], ot.T.astype(BF16))


def _attn_scratch(seq, t):
    chains = HEAD_PAIR * len(_ROWS)
    return [pltpu.VMEM((HEAD_PAIR, seq // t, LANES, t), BF16),
            pltpu.VMEM((t, t), F32),
            pltpu.VMEM((SCORE_DEPTH, HEAD_PAIR, t, t), F32),
            pltpu.VMEM((chains, 1, t), F32),
            pltpu.VMEM((chains, LANES, t), F32)]


def _tile_rows(i, t):
    return pl.ds(pl.multiple_of(i * t, t), t)


def _mla_attn_kernel(q_ref, k_ref, v_ref, o_ref, qt_ref, vt_ref, *scratch, t, n_tiles):
    g = pl.program_id(2)

    @pl.when(g == 0)
    def _():
        _fill_v_transposed(v_ref, vt_ref, t, MLA_V)
        _fill_causal_cap(scratch[0], t)

    for row, qi in enumerate((g, n_tiles - 1 - g)):
        for h in _HEADS:
            qt_ref[row, h] = q_ref[0, h, _tile_rows(qi, t), :].astype(F32).T.astype(BF16)

    def load_qt(row):
        return [qt_ref[row, h] for h in _HEADS]

    def load_k(kj):
        return [k_ref[0, h, _tile_rows(kj, t), :] for h in _HEADS]

    def write_out(qt, tile):
        o_ref[0, _tile_rows(qt, t), :] = tile

    _attend_rows(g, n_tiles, t, load_qt, load_k, vt_ref, write_out, scratch, MLA_V)


def _mla_attn(q, k, v, t):
    batch, heads, seq, _ = q.shape
    n_tiles = seq // t
    assert n_tiles % 2 == 0
    return pl.pallas_call(
        functools.partial(_mla_attn_kernel, t=t, n_tiles=n_tiles),
        grid=(batch, heads // HEAD_PAIR, n_tiles // 2),
        in_specs=[
            pl.BlockSpec((1, HEAD_PAIR, seq, LANES), lambda b, h, g: (b, h, 0, 0)),
            pl.BlockSpec((1, HEAD_PAIR, seq, LANES), lambda b, h, g: (b, h, 0, 0)),
            pl.BlockSpec((1, seq, LANES), lambda b, h, g: (b, 0, h)),
        ],
        out_specs=pl.BlockSpec((1, seq, LANES), lambda b, h, g: (b, 0, h)),
        out_shape=jax.ShapeDtypeStruct((batch, seq, heads * MLA_V), BF16),
        scratch_shapes=[pltpu.VMEM((len(_ROWS), HEAD_PAIR, LANES, t), BF16)]
                       + _attn_scratch(seq, t),
        compiler_params=pltpu.CompilerParams(
            dimension_semantics=("parallel", "parallel", "arbitrary"), vmem_limit_bytes=VMEM_LIMIT),
        name="mla_attn",
    )(q, k, v)


def _moba_kernel(q_ref, k_ref, v_ref, o_ref, kmean_ref, qh_ref, cap_ref, vt_ref, *scratch, nb):
    t = MOBA_BLOCK
    g = pl.program_id(2)

    @pl.when(g == 0)
    def _():
        _fill_v_transposed(v_ref, vt_ref, t, MOBA_HEAD_DIM)
        _fill_causal_cap(scratch[0], t)
        kmean_ref[...] = jnp.zeros_like(kmean_ref)
        for j in range(nb):
            kb = k_ref[0, j * t:(j + 1) * t, :].astype(F32)
            kmean_ref[j:j + 1, :] = jnp.mean(kb, axis=0, keepdims=True)

    lane = lax.broadcasted_iota(jnp.int32, (t, LANES), 1)
    lane_k = lax.broadcasted_iota(jnp.int32, (LANES, LANES), 1)
    blk_id = lax.broadcasted_iota(jnp.int32, (nb, t), 0)
    nb_rows = -(-nb // SUBLANES) * SUBLANES
    for row, qi in enumerate((g, nb - 1 - g)):
        n_sel = jnp.minimum(qi, MOBA_TOPK).astype(F32)
        q_pair = q_ref[0, _tile_rows(qi, t), :]
        for hh in _HEADS:
            in_head = (lane >= hh * MOBA_HEAD_DIM) & (lane < (hh + 1) * MOBA_HEAD_DIM)
            qt_gate = jnp.where(in_head, q_pair, jnp.zeros_like(q_pair)).astype(F32).T
            qh_ref[row, hh] = (qt_gate * (MOBA_HEAD_DIM ** -0.5 * LOG2_E)).astype(BF16)
            in_head_k = (lane_k >= hh * MOBA_HEAD_DIM) & (lane_k < (hh + 1) * MOBA_HEAD_DIM)
            km = jnp.where(in_head_k, kmean_ref[...], 0.0)[:nb_rows]
            gate = jnp.dot(km, qt_gate, precision=lax.Precision.HIGHEST, preferred_element_type=F32)[:nb]
            gate = jnp.where(blk_id < qi, gate, NEG_INF)
            rank = jnp.zeros((nb, t), F32)
            for jp in range(nb):
                gj = gate[jp:jp + 1, :]
                beats = (gj > gate) | ((gj == gate) & (blk_id > jp))
                rank = rank + jnp.where(beats, 1.0, 0.0)
            keep = ((rank < n_sel) & (blk_id < qi)) | (blk_id == qi)
            cap = jnp.where(keep, jnp.inf, NEG_INF)
            for j in range(nb):
                cap_ref[row, hh, j] = cap[j:j + 1, :]

    def load_qt(row):
        return [qh_ref[row, h] for h in _HEADS]

    def load_k(kj):
        return [k_ref[0, _tile_rows(kj, t), :]] * HEAD_PAIR

    def write_out(qt, tile):
        o_ref[0, _tile_rows(qt, t), :] = tile

    _attend_rows(g, nb, t, load_qt, load_k, vt_ref, write_out, scratch, MOBA_HEAD_DIM,
                 row_cap=lambda row, h, kj: cap_ref[row, h, kj])


def _moba_attn(p3):
    batch, seq, _ = p3.shape
    t = MOBA_BLOCK
    nb = seq // t
    assert nb % 2 == 0
    per_group = GROUP // LANES
    return pl.pallas_call(
        functools.partial(_moba_kernel, nb=nb),
        grid=(batch, MOBA_HEADS // HEAD_PAIR, nb // 2),
        in_specs=[
            pl.BlockSpec((1, seq, LANES), lambda b, h, g: (b, 0, G_MOBA_Q * per_group + h)),
            pl.BlockSpec((1, seq, LANES), lambda b, h, g: (b, 0, G_MOBA_K * per_group + h)),
            pl.BlockSpec((1, seq, LANES), lambda b, h, g: (b, 0, G_MOBA_V * per_group + h)),
        ],
        out_specs=pl.BlockSpec((1, seq, LANES), lambda b, h, g: (b, 0, h)),
        out_shape=jax.ShapeDtypeStruct((batch, seq, MOBA_HEADS * MOBA_HEAD_DIM), BF16),
        scratch_shapes=[pltpu.VMEM((LANES, LANES), F32),
                        pltpu.VMEM((len(_ROWS), HEAD_PAIR, LANES, t), BF16),
                        pltpu.VMEM((len(_ROWS), HEAD_PAIR, nb, 1, t), F32)]
                       + _attn_scratch(seq, t),
        compiler_params=pltpu.CompilerParams(
            dimension_semantics=("parallel", "parallel", "arbitrary"), vmem_limit_bytes=VMEM_LIMIT),
        name="moba_attn",
    )(p3, p3, p3)


def _retention_kernel(qk_ref, v_ref, g_ref, decay_ref, xi_ref, zeta_ref, gamma_ref, gnw_ref, gnb_ref,
                      o_ref, state_ref, *, chunks):
    c = RET_CHUNK

    @pl.when(pl.program_id(1) == 0)
    def _():
        state_ref[...] = jnp.zeros_like(state_ref)

    lane = lax.broadcasted_iota(jnp.int32, (c, LANES), 1)
    k_off = RET_HEADS * RET_QK_DIM
    for n in range(chunks):
        rows = slice(n * c, (n + 1) * c)
        for h in range(RET_HEADS):
            pair, half = divmod(h, HEAD_PAIR)
            in_head = (lane >= half * RET_QK_DIM) & (lane < (half + 1) * RET_QK_DIM)
            q = jnp.where(in_head, qk_ref[0, rows, pair * LANES:(pair + 1) * LANES].astype(F32), 0.0)
            k = jnp.where(in_head, qk_ref[0, rows, k_off + pair * LANES:k_off + (pair + 1) * LANES].astype(F32), 0.0)
            vs = slice(h * RET_V_DIM, (h + 1) * RET_V_DIM)
            v = v_ref[0, rows, vs]
            state = state_ref[h]
            scores = _nt_dot(q.astype(BF16), k.astype(BF16)) * decay_ref[h]
            inner = jnp.dot(scores.astype(BF16), v, preferred_element_type=F32)
            cross = jnp.dot((q * xi_ref[h]).astype(BF16), state.astype(BF16), preferred_element_type=F32)
            kv = lax.dot_general((k * zeta_ref[h]).astype(BF16), v, (((0,), (0,)), ((), ())),
                                 preferred_element_type=F32)
            state_ref[h] = gamma_ref[h] * state + kv
            o = inner + cross
            mu = jnp.mean(o, axis=-1, keepdims=True)
            d = o - mu
            var = jnp.mean(d * d, axis=-1, keepdims=True)
            o = d * lax.rsqrt(var + GN_EPS) * gnw_ref[:, vs] + gnb_ref[:, vs]
            g = g_ref[0, rows, vs].astype(F32)
            o_ref[0, rows, vs] = (g * jax.nn.sigmoid(g) * o).astype(BF16)


def _retention(p3, decay, xi, zeta, gamma, gnw, gnb, tr):
    batch, seq, _ = p3.shape
    width = RET_HEADS * RET_V_DIM
    const = lambda shape: pl.BlockSpec(shape, lambda b, i: (0,) * len(shape))
    return pl.pallas_call(
        functools.partial(_retention_kernel, chunks=tr // RET_CHUNK),
        grid=(batch, seq // tr),
        in_specs=[
            pl.BlockSpec((1, tr, GROUP), lambda b, i: (b, i, G_RET_QK)),
            pl.BlockSpec((1, tr, GROUP), lambda b, i: (b, i, G_RET_V)),
            pl.BlockSpec((1, tr, GROUP), lambda b, i: (b, i, G_RET_G)),
            const((RET_HEADS, RET_CHUNK, RET_CHUNK)),
            const((RET_HEADS, RET_CHUNK, LANES)), const((RET_HEADS, RET_CHUNK, LANES)),
            const((RET_HEADS, 1, LANES)),
            const((1, width)), const((1, width)),
        ],
        out_specs=pl.BlockSpec((1, tr, width), lambda b, i: (b, i, 0)),
        out_shape=jax.ShapeDtypeStruct((batch, seq, width), BF16),
        scratch_shapes=[pltpu.VMEM((RET_HEADS, LANES, RET_V_DIM), F32)],
        compiler_params=pltpu.CompilerParams(
            dimension_semantics=("parallel", "arbitrary"), vmem_limit_bytes=VMEM_LIMIT),
        name="retention",
    )(p3, p3, p3, decay, xi, zeta, gamma, gnw, gnb)


def _merge_kernel(x_ref, gates_ref, oa_ref, ob_ref, oc_ref, wa_ref, wb_ref, wc_ref, wo_ref, o_ref):
    merged = None
    for idx, (o_in, w) in enumerate(((oa_ref, wa_ref), (ob_ref, wb_ref), (oc_ref, wc_ref))):
        gate = gates_ref[:, idx * D_MODEL:(idx + 1) * D_MODEL].astype(F32)
        term = jax.nn.sigmoid(gate) * jnp.dot(o_in[...], w[...], preferred_element_type=F32)
        merged = term if merged is None else merged + term
    o_ref[...] = x_ref[...] + jnp.dot(merged.astype(BF16), wo_ref[...], preferred_element_type=F32)


def _merge(x2, p2, oa, ob, oc, wa, wb, wc, wo, tm):
    tokens = x2.shape[0]
    row = lambda w: pl.BlockSpec((tm, w), lambda i: (i, 0))
    const = lambda shape: pl.BlockSpec(shape, lambda i: (0, 0))
    return pl.pallas_call(
        _merge_kernel,
        grid=(tokens // tm,),
        in_specs=[row(D_MODEL), row(3 * D_MODEL), row(GROUP), row(GROUP), row(GROUP),
                  const((GROUP, D_MODEL)), const((GROUP, D_MODEL)), const((GROUP, D_MODEL)),
                  const((D_MODEL, D_MODEL))],
        out_specs=row(D_MODEL),
        out_shape=jax.ShapeDtypeStruct((tokens, D_MODEL), F32),
        compiler_params=pltpu.CompilerParams(
            dimension_semantics=("parallel",), vmem_limit_bytes=VMEM_LIMIT),
        name="merge",
    )(x2, p2, oa, ob, oc, wa, wb, wc, wo)


def _mlp_kernel(x_ref, g_ref, wu_ref, wd_ref, fg_ref, o_ref, h_ref, *, final_norm):
    f = pl.program_id(1)

    @pl.when(f == 0)
    def _():
        x = x_ref[...]
        h_ref[...] = _rms(x, g_ref[...]).astype(BF16)
        o_ref[...] = x

    u = jnp.maximum(jnp.dot(h_ref[...], wu_ref[...], preferred_element_type=F32), 0.0)
    o_ref[...] += jnp.dot((u * u).astype(BF16), wd_ref[...], preferred_element_type=F32)

    if final_norm:
        @pl.when(f == pl.num_programs(1) - 1)
        def _():
            o_ref[...] = _rms(o_ref[...], fg_ref[...])


def _mlp(x2, g, wu, wd, fg, tm, tf, final_norm):
    tokens = x2.shape[0]
    return pl.pallas_call(
        functools.partial(_mlp_kernel, final_norm=final_norm),
        grid=(tokens // tm, D_FF // tf),
        in_specs=[
            pl.BlockSpec((tm, D_MODEL), lambda i, f: (i, 0)),
            pl.BlockSpec((1, D_MODEL), lambda i, f: (0, 0)),
            pl.BlockSpec((D_MODEL, tf), lambda i, f: (0, f)),
            pl.BlockSpec((tf, D_MODEL), lambda i, f: (f, 0)),
            pl.BlockSpec((1, D_MODEL), lambda i, f: (0, 0)),
        ],
        out_specs=pl.BlockSpec((tm, D_MODEL), lambda i, f: (i, 0)),
        out_shape=jax.ShapeDtypeStruct((tokens, D_MODEL), F32),
        scratch_shapes=[pltpu.VMEM((tm, D_MODEL), BF16)],
        compiler_params=pltpu.CompilerParams(
            dimension_semantics=("parallel", "arbitrary"), vmem_limit_bytes=VMEM_LIMIT),
        name="mlp",
    )(x2, g, wu, wd, fg)


def _rope_cos_sin(seq, dim, theta):
    inv = 1.0 / (theta ** (jnp.arange(0, dim, 2, dtype=F32) / dim))
    ang = jnp.arange(seq, dtype=F32)[:, None] * inv[None, :]
    return jnp.cos(ang), jnp.sin(ang)


def _rope_tables(seq, dim, theta, period, offset):
    cos, sin = _rope_cos_sin(seq, dim, theta)
    half = dim // 2
    lo, hi = slice(offset, offset + half), slice(offset + half, offset + dim)
    c = jnp.ones((seq, period), F32).at[:, lo].set(cos).at[:, hi].set(cos)
    s_next = jnp.zeros((seq, period), F32).at[:, lo].set(-sin)
    s_prev = jnp.zeros((seq, period), F32).at[:, hi].set(sin)
    return jnp.tile(jnp.stack([c, s_next, s_prev]), (1, 1, LANES // period))


def _retention_tables():
    c = RET_CHUNK
    log_gamma = jnp.log(1.0 - 2.0 ** (-5.0 - jnp.arange(RET_HEADS, dtype=F32)))
    pos = jnp.arange(c, dtype=F32)
    diff = pos[:, None] - pos[None, :]
    decay = jnp.where(diff >= 0, jnp.exp(log_gamma[:, None, None] * diff), 0.0)
    xi = jnp.exp(log_gamma[:, None] * (pos + 1.0))
    zeta = jnp.exp(log_gamma[:, None] * (c - 1.0 - pos))
    gamma_c = jnp.exp(log_gamma * c)
    bcast = lambda t: jnp.broadcast_to(t[..., None], t.shape + (LANES,))
    return decay, bcast(xi), bcast(zeta), bcast(gamma_c[:, None])


def _pad_in_proj_weight(w):
    sizes = (MLA_Q_RANK, MLA_KV_RANK, MLA_ROPE, 512, 512, 512, 256, 256, 512, 512, D_MODEL, D_MODEL, D_MODEL)
    parts, start = [], 0
    for sz in sizes:
        parts.append(w[:, start:start + sz])
        start += sz
    cq, ckv, kr, mq, mk, mv, rq, rk, rv, rg, g_mla, g_moba, g_ret = parts
    zeros = lambda n: jnp.zeros((w.shape[0], n), w.dtype)
    kr_pad = jnp.concatenate([zeros(MLA_NOPE), kr, zeros(LANES - MLA_NOPE - MLA_ROPE)], axis=1)
    return jnp.concatenate([g_mla, g_moba, g_ret, cq, ckv, kr_pad, mq, mk, mv, rq, rk, rv, rg],
                           axis=1).astype(BF16)


def _pad_mla_weights(w_q_up, w_kv_up):
    d_qk = MLA_NOPE + MLA_ROPE
    wq = w_q_up.reshape(MLA_Q_RANK, MLA_HEADS, d_qk)
    wq = jnp.pad(wq, ((0, 0), (0, 0), (0, LANES - d_qk))).reshape(MLA_Q_RANK, MLA_HEADS * LANES)
    wkv = w_kv_up.reshape(MLA_KV_RANK, MLA_HEADS, MLA_NOPE + MLA_V)
    wk = jnp.pad(wkv[:, :, :MLA_NOPE], ((0, 0), (0, 0), (0, LANES - MLA_NOPE)))
    wk = wk.reshape(MLA_KV_RANK, MLA_HEADS * LANES)
    wv = wkv[:, :, MLA_NOPE:].reshape(MLA_KV_RANK, MLA_HEADS * MLA_V)
    return wq.astype(BF16), wk.astype(BF16), wv.astype(BF16)


def _tile_sizes(seq):
    pick = lambda want: want if seq % want == 0 else seq
    return dict(in_proj=pick(512), prep=pick(512), attn=pick(256), ret=pick(512),
                merge=pick(512), mlp=pick(1024), mlp_ff=512)


def kernel(x, attn_norm, w_in, mla_q_norm, mla_kv_norm, mla_w_q_up, mla_w_kv_up, ret_gn_w, ret_gn_b,
           w_branch_mla, w_branch_moba, w_branch_ret, w_out, mlp_norm, w_mlp_up, w_mlp_down, final_norm):
    batch, seq, d = x.shape
    depth = w_in.shape[0]
    tiles = _tile_sizes(seq)
    mla_tab = _rope_tables(seq, MLA_ROPE, ROPE_THETA, LANES, MLA_NOPE)
    moba_tab = _rope_tables(seq, MOBA_HEAD_DIM // PARTIAL_ROPE_DIV, ROPE_THETA, MOBA_HEAD_DIM, 0)
    ret_tab = _rope_tables(seq, RET_QK_DIM, RET_THETA, RET_QK_DIM, 0)
    decay, xi, zeta, gamma_c = _retention_tables()
    row = lambda v: v.reshape(1, -1)

    x2 = x.reshape(batch * seq, d)
    for l in range(depth):
        w_in_p = _pad_in_proj_weight(w_in[l])
        wq, wk, wv = _pad_mla_weights(mla_w_q_up[l], mla_w_kv_up[l])
        p2 = _in_proj(x2, row(attn_norm[l]), w_in_p, moba_tab, ret_tab, seq, tiles["in_proj"])
        p3 = p2.reshape(batch, seq, IN_COLS_PADDED)
        q_mla, k_mla, v_mla = _mla_prep(p3, row(mla_q_norm[l]), row(mla_kv_norm[l]), wq, wk, wv,
                                        mla_tab, tiles["prep"])
        o_mla = _mla_attn(q_mla, k_mla, v_mla, tiles["attn"])
        o_moba = _moba_attn(p3)
        o_ret = _retention(p3, decay, xi, zeta, gamma_c, row(ret_gn_w[l]), row(ret_gn_b[l]), tiles["ret"])
        flat = lambda o: o.reshape(batch * seq, GROUP)
        x2 = _merge(x2, p2, flat(o_mla), flat(o_moba), flat(o_ret),
                    w_branch_mla[l].astype(BF16), w_branch_moba[l].astype(BF16),
                    w_branch_ret[l].astype(BF16), w_out[l].astype(BF16), tiles["merge"])
        x2 = _mlp(x2, row(mlp_norm[l]), w_mlp_up[l].astype(BF16), w_mlp_down[l].astype(BF16),
                  row(final_norm), tiles["mlp"], tiles["mlp_ff"], final_norm=(l == depth - 1))
    return x2.reshape(batch, seq, d)
```

```python
import functools

import jax
import jax.numpy as jnp
from jax import lax
from jax.experimental import pallas as pl
from jax.experimental.pallas import tpu as pltpu

F32 = jnp.float32
BF16 = jnp.bfloat16

D_MODEL = 1024
MLA_HEADS = 8
MLA_Q_RANK = 256
MLA_KV_RANK = 128
MLA_NOPE = 64
MLA_ROPE = 32
MLA_V = 64
MOBA_HEADS = 8
MOBA_HEAD_DIM = 64
MOBA_BLOCK = 256
MOBA_TOPK = 3
RET_HEADS = 4
RET_QK_DIM = 64
RET_V_DIM = 128
RET_CHUNK = 128
RET_THETA = 10000.0
ROPE_THETA = 500000.0
PARTIAL_ROPE_DIV = 4
D_FF = 4 * D_MODEL
NORM_EPS = 1e-6
GN_EPS = 1e-5
NEG_INF = -1e30
LOG2_E = 1.4426950408889634

LANES = 128
SUBLANES = 8
HEAD_PAIR = 2
SCORE_DEPTH = 3
VMEM_LIMIT = 48 * 1024 * 1024

GROUP = 512
G_GATES = 0
G_MLA = 6
G_MOBA_Q, G_MOBA_K, G_MOBA_V = 7, 8, 9
G_RET_QK, G_RET_V, G_RET_G = 10, 11, 12
N_GROUPS = 13
IN_COLS_PADDED = N_GROUPS * GROUP


def _nt_dot(a, b):
    return lax.dot_general(a, b, (((1,), (1,)), ((), ())), preferred_element_type=F32)


def _rms(x, g):
    return x * lax.rsqrt(jnp.mean(x * x, axis=-1, keepdims=True) + NORM_EPS) * g


def _rope_lanes(y, tab_ref, half):
    return (y * tab_ref[0]
            + pltpu.roll(y, LANES - half, 1) * tab_ref[1]
            + pltpu.roll(y, half, 1) * tab_ref[2])


def _in_proj_kernel(x_ref, g_ref, w_ref, moba_tab, ret_tab, o_ref, h_ref):
    h_ref[...] = _rms(x_ref[...], g_ref[...]).astype(BF16)
    lane_tiles = GROUP // LANES
    for j in range(N_GROUPS):
        acc = jnp.dot(h_ref[...], w_ref[:, j * GROUP:(j + 1) * GROUP], preferred_element_type=F32)
        for r in range(lane_tiles):
            y = acc[:, r * LANES:(r + 1) * LANES]
            if j in (G_MOBA_Q, G_MOBA_K):
                y = _rope_lanes(y, moba_tab, MOBA_HEAD_DIM // PARTIAL_ROPE_DIV // 2)
            elif j == G_RET_QK:
                y = _rope_lanes(y, ret_tab, RET_QK_DIM // 2)
                if r >= lane_tiles // 2:
                    y = y * (RET_QK_DIM ** -0.5)
            o_ref[:, j * GROUP + r * LANES:j * GROUP + (r + 1) * LANES] = y.astype(BF16)


def _in_proj(x2, g, w, moba_tab, ret_tab, seq, tm):
    tokens = x2.shape[0]
    s_tiles = seq // tm
    return pl.pallas_call(
        _in_proj_kernel,
        grid=(tokens // tm,),
        in_specs=[
            pl.BlockSpec((tm, D_MODEL), lambda i: (i, 0)),
            pl.BlockSpec((1, D_MODEL), lambda i: (0, 0)),
            pl.BlockSpec((D_MODEL, IN_COLS_PADDED), lambda i: (0, 0), pipeline_mode=pl.Buffered(1)),
            pl.BlockSpec((3, tm, LANES), lambda i: (0, i % s_tiles, 0)),
            pl.BlockSpec((3, tm, LANES), lambda i: (0, i % s_tiles, 0)),
        ],
        out_specs=pl.BlockSpec((tm, IN_COLS_PADDED), lambda i: (i, 0)),
        out_shape=jax.ShapeDtypeStruct((tokens, IN_COLS_PADDED), BF16),
        scratch_shapes=[pltpu.VMEM((tm, D_MODEL), BF16)],
        compiler_params=pltpu.CompilerParams(
            dimension_semantics=("parallel",), vmem_limit_bytes=VMEM_LIMIT),
        name="in_proj",
    )(x2, g, w, moba_tab, ret_tab)


def _mla_prep_kernel(p_ref, qg_ref, kvg_ref, wq_ref, wk_ref, wv_ref, tab, q_ref, k_ref, v_ref):
    blk = p_ref[...]
    cq = blk[:, :MLA_Q_RANK].astype(F32)
    ckv = blk[:, MLA_Q_RANK:MLA_Q_RANK + MLA_KV_RANK].astype(F32)
    kr = blk[:, MLA_Q_RANK + MLA_KV_RANK:].astype(F32)
    cqn = _rms(cq, qg_ref[...]).astype(BF16)
    ckvn = _rms(ckv, kvg_ref[...]).astype(BF16)
    q = jnp.dot(cqn, wq_ref[...], preferred_element_type=F32)
    k = jnp.dot(ckvn, wk_ref[...], preferred_element_type=F32)
    v = jnp.dot(ckvn, wv_ref[...], preferred_element_type=F32)
    kpe = _rope_lanes(kr, tab, MLA_ROPE // 2)
    scale = (MLA_NOPE + MLA_ROPE) ** -0.5 * LOG2_E
    for h in range(MLA_HEADS):
        sl = slice(h * LANES, (h + 1) * LANES)
        q_ref[0, h] = (_rope_lanes(q[:, sl], tab, MLA_ROPE // 2) * scale).astype(BF16)
        k_ref[0, h] = (k[:, sl] + kpe).astype(BF16)
    v_ref[0] = v.astype(BF16)


def _mla_prep(p3, qg, kvg, wq, wk, wv, tab, tm):
    batch, seq, _ = p3.shape
    hs = jax.ShapeDtypeStruct((batch, MLA_HEADS, seq, LANES), BF16)
    const = lambda shape: pl.BlockSpec(shape, lambda b, i: (0,) * len(shape))
    return pl.pallas_call(
        _mla_prep_kernel,
        grid=(batch, seq // tm),
        in_specs=[
            pl.BlockSpec((None, tm, GROUP), lambda b, i: (b, i, G_MLA)),
            const((1, MLA_Q_RANK)), const((1, MLA_KV_RANK)),
            const((MLA_Q_RANK, MLA_HEADS * LANES)), const((MLA_KV_RANK, MLA_HEADS * LANES)),
            const((MLA_KV_RANK, MLA_HEADS * MLA_V)),
            pl.BlockSpec((3, tm, LANES), lambda b, i: (0, i, 0)),
        ],
        out_specs=[
            pl.BlockSpec((1, MLA_HEADS, tm, LANES), lambda b, i: (b, 0, i, 0)),
            pl.BlockSpec((1, MLA_HEADS, tm, LANES), lambda b, i: (b, 0, i, 0)),
            pl.BlockSpec((1, tm, MLA_HEADS * MLA_V), lambda b, i: (b, i, 0)),
        ],
        out_shape=[hs, hs, jax.ShapeDtypeStruct((batch, seq, MLA_HEADS * MLA_V), BF16)],
        compiler_params=pltpu.CompilerParams(
            dimension_semantics=("parallel", "parallel"), vmem_limit_bytes=VMEM_LIMIT),
        name="mla_prep",
    )(p3, qg, kvg, wq, wk, wv, tab)


_HEADS = range(HEAD_PAIR)
_ROWS = range(2)


def _ones_row(h, head_dim):
    return head_dim if h == 0 else 0


def _fill_v_transposed(v_ref, vt_ref, t, head_dim):
    sub = lax.broadcasted_iota(jnp.int32, (LANES, t), 0)
    for j in range(vt_ref.shape[1]):
        vt = v_ref[0, j * t:(j + 1) * t, :].astype(F32).T
        for h in _HEADS:
            own = (sub >= h * head_dim) & (sub < (h + 1) * head_dim)
            fill = jnp.where(sub == _ones_row(h, head_dim), 1.0, 0.0)
            vt_ref[h, j] = jnp.where(own, vt, fill).astype(BF16)


def _fill_causal_cap(cap_ref, t):
    key = lax.broadcasted_iota(jnp.int32, (t, t), 0)
    query = lax.broadcasted_iota(jnp.int32, (t, t), 1)
    cap_ref[...] = jnp.where(key <= query, jnp.inf, NEG_INF)


def _attend_rows(g, n_tiles, t, load_qt, load_k, vt_ref, write_out, scratch, head_dim, row_cap=None):
    causal_cap_ref, bufs, m_ref, acc_ref = scratch
    depth = bufs.shape[0]
    q_tile = [g, n_tiles - 1 - g]
    m_ref[...] = jnp.full_like(m_ref, -jnp.inf)
    acc_ref[...] = jnp.zeros_like(acc_ref)

    def locate(n):
        row = (n > q_tile[0]).astype(jnp.int32)
        return row, jnp.where(row == 0, q_tile[0] - n, n - (q_tile[0] + 1))

    def issue_scores(n, slot):
        row, kj = locate(n)
        ks, qts = load_k(kj), load_qt(row)
        for h in _HEADS:
            bufs[slot, h] = jnp.dot(ks[h], qts[h], preferred_element_type=F32)

    def softmax_step(n, slot, diagonal=False):
        row, kj = locate(n)
        chain = [HEAD_PAIR * row + h for h in _HEADS]
        s = [bufs[slot, h] for h in _HEADS]
        if diagonal:
            s = [jnp.minimum(s[h], causal_cap_ref[...]) for h in _HEADS]
        elif row_cap is not None:
            s = [jnp.minimum(s[h], row_cap(row, h, kj)) for h in _HEADS]
        m_old = [m_ref[chain[h]] for h in _HEADS]
        m_new = [jnp.maximum(m_old[h], jnp.max(s[h], axis=0, keepdims=True)) for h in _HEADS]
        alpha = [jnp.exp2(m_old[h] - m_new[h]) for h in _HEADS]
        p = [jnp.exp2(s[h] - m_new[h]).astype(BF16) for h in _HEADS]
        pv = [jnp.dot(vt_ref[h, kj], p[h], preferred_element_type=F32) for h in _HEADS]
        for h in _HEADS:
            m_ref[chain[h]] = m_new[h]
            acc_ref[chain[h]] = alpha[h] * acc_ref[chain[h]] + pv[h]

    steps = n_tiles + 1
    for n in range(min(depth, steps)):
        issue_scores(n, n)
    for n in range(steps):
        softmax_step(n, n % depth, diagonal=n in (0, n_tiles))
        if n + depth < steps:
            issue_scores(n + depth, n % depth)

    sub = lax.broadcasted_iota(jnp.int32, acc_ref.shape[1:], 0)
    for row in _ROWS:
        outs = []
        for h in _HEADS:
            acc = acc_ref[HEAD_PAIR * row + h]
            ones_row = _ones_row(h, head_dim)
            outs.append(acc * (1.0 / acc[ones_row:ones_row + 1, :]))
        ot = jnp.where(sub < head_dim, outs[0], outs[1])
        write_out(q_tile[row], ot.T.astype(BF16))


def _attn_scratch(seq, t):
    chains = HEAD_PAIR * len(_ROWS)
    return [pltpu.VMEM((HEAD_PAIR, seq // t, LANES, t), BF16),
            pltpu.VMEM((t, t), F32),
            pltpu.VMEM((SCORE_DEPTH, HEAD_PAIR, t, t), F32),
            pltpu.VMEM((chains, 1, t), F32),
            pltpu.VMEM((chains, LANES, t), F32)]


def _tile_rows(i, t):
    return pl.ds(pl.multiple_of(i * t, t), t)


def _mla_attn_kernel(q_ref, k_ref, v_ref, o_ref, qt_ref, vt_ref, *scratch, t, n_tiles):
    g = pl.program_id(2)

    @pl.when(g == 0)
    def _():
        _fill_v_transposed(v_ref, vt_ref, t, MLA_V)
        _fill_causal_cap(scratch[0], t)

    for row, qi in enumerate((g, n_tiles - 1 - g)):
        for h in _HEADS:
            qt_ref[row, h] = q_ref[0, h, _tile_rows(qi, t), :].astype(F32).T.astype(BF16)

    def load_qt(row):
        return [qt_ref[row, h] for h in _HEADS]

    def load_k(kj):
        return [k_ref[0, h, _tile_rows(kj, t), :] for h in _HEADS]

    def write_out(qt, tile):
        o_ref[0, _tile_rows(qt, t), :] = tile

    _attend_rows(g, n_tiles, t, load_qt, load_k, vt_ref, write_out, scratch, MLA_V)


def _mla_attn(q, k, v, t):
    batch, heads, seq, _ = q.shape
    n_tiles = seq // t
    assert n_tiles % 2 == 0
    return pl.pallas_call(
        functools.partial(_mla_attn_kernel, t=t, n_tiles=n_tiles),
        grid=(batch, heads // HEAD_PAIR, n_tiles // 2),
        in_specs=[
            pl.BlockSpec((1, HEAD_PAIR, seq, LANES), lambda b, h, g: (b, h, 0, 0)),
            pl.BlockSpec((1, HEAD_PAIR, seq, LANES), lambda b, h, g: (b, h, 0, 0)),
            pl.BlockSpec((1, seq, LANES), lambda b, h, g: (b, 0, h)),
        ],
        out_specs=pl.BlockSpec((1, seq, LANES), lambda b, h, g: (b, 0, h)),
        out_shape=jax.ShapeDtypeStruct((batch, seq, heads * MLA_V), BF16),
        scratch_shapes=[pltpu.VMEM((len(_ROWS), HEAD_PAIR, LANES, t), BF16)]
                       + _attn_scratch(seq, t),
        compiler_params=pltpu.CompilerParams(
            dimension_semantics=("parallel", "parallel", "arbitrary"), vmem_limit_bytes=VMEM_LIMIT),
        name="mla_attn",
    )(q, k, v)


def _moba_kernel(q_ref, k_ref, v_ref, o_ref, kmean_ref, qh_ref, cap_ref, vt_ref, *scratch, nb):
    t = MOBA_BLOCK
    g = pl.program_id(2)

    @pl.when(g == 0)
    def _():
        _fill_v_transposed(v_ref, vt_ref, t, MOBA_HEAD_DIM)
        _fill_causal_cap(scratch[0], t)
        kmean_ref[...] = jnp.zeros_like(kmean_ref)
        for j in range(nb):
            kb = k_ref[0, j * t:(j + 1) * t, :].astype(F32)
            kmean_ref[j:j + 1, :] = jnp.mean(kb, axis=0, keepdims=True)

    lane = lax.broadcasted_iota(jnp.int32, (t, LANES), 1)
    lane_k = lax.broadcasted_iota(jnp.int32, (LANES, LANES), 1)
    blk_id = lax.broadcasted_iota(jnp.int32, (nb, t), 0)
    nb_rows = -(-nb // SUBLANES) * SUBLANES
    for row, qi in enumerate((g, nb - 1 - g)):
        n_sel = jnp.minimum(qi, MOBA_TOPK).astype(F32)
        q_pair = q_ref[0, _tile_rows(qi, t), :]
        for hh in _HEADS:
            in_head = (lane >= hh * MOBA_HEAD_DIM) & (lane < (hh + 1) * MOBA_HEAD_DIM)
            qt_gate = jnp.where(in_head, q_pair, jnp.zeros_like(q_pair)).astype(F32).T
            qh_ref[row, hh] = (qt_gate * (MOBA_HEAD_DIM ** -0.5 * LOG2_E)).astype(BF16)
            in_head_k = (lane_k >= hh * MOBA_HEAD_DIM) & (lane_k < (hh + 1) * MOBA_HEAD_DIM)
            km = jnp.where(in_head_k, kmean_ref[...], 0.0)[:nb_rows]
            gate = jnp.dot(km, qt_gate, precision=lax.Precision.HIGHEST, preferred_element_type=F32)[:nb]
            gate = jnp.where(blk_id < qi, gate, NEG_INF)
            rank = jnp.zeros((nb, t), F32)
            for jp in range(nb):
                gj = gate[jp:jp + 1, :]
                beats = (gj > gate) | ((gj == gate) & (blk_id > jp))
                rank = rank + jnp.where(beats, 1.0, 0.0)
            keep = ((rank < n_sel) & (blk_id < qi)) | (blk_id == qi)
            cap = jnp.where(keep, jnp.inf, NEG_INF)
            for j in range(nb):
                cap_ref[row, hh, j] = cap[j:j + 1, :]

    def load_qt(row):
        return [qh_ref[row, h] for h in _HEADS]

    def load_k(kj):
        return [k_ref[0, _tile_rows(kj, t), :]] * HEAD_PAIR

    def write_out(qt, tile):
        o_ref[0, _tile_rows(qt, t), :] = tile

    _attend_rows(g, nb, t, load_qt, load_k, vt_ref, write_out, scratch, MOBA_HEAD_DIM,
                 row_cap=lambda row, h, kj: cap_ref[row, h, kj])


def _moba_attn(p3):
    batch, seq, _ = p3.shape
    t = MOBA_BLOCK
    nb = seq // t
    assert nb % 2 == 0
    per_group = GROUP // LANES
    return pl.pallas_call(
        functools.partial(_moba_kernel, nb=nb),
        grid=(batch, MOBA_HEADS // HEAD_PAIR, nb // 2),
        in_specs=[
            pl.BlockSpec((1, seq, LANES), lambda b, h, g: (b, 0, G_MOBA_Q * per_group + h)),
            pl.BlockSpec((1, seq, LANES), lambda b, h, g: (b, 0, G_MOBA_K * per_group + h)),
            pl.BlockSpec((1, seq, LANES), lambda b, h, g: (b, 0, G_MOBA_V * per_group + h)),
        ],
        out_specs=pl.BlockSpec((1, seq, LANES), lambda b, h, g: (b, 0, h)),
        out_shape=jax.ShapeDtypeStruct((batch, seq, MOBA_HEADS * MOBA_HEAD_DIM), BF16),
        scratch_shapes=[pltpu.VMEM((LANES, LANES), F32),
                        pltpu.VMEM((len(_ROWS), HEAD_PAIR, LANES, t), BF16),
                        pltpu.VMEM((len(_ROWS), HEAD_PAIR, nb, 1, t), F32)]
                       + _attn_scratch(seq, t),
        compiler_params=pltpu.CompilerParams(
            dimension_semantics=("parallel", "parallel", "arbitrary"), vmem_limit_bytes=VMEM_LIMIT),
        name="moba_attn",
    )(p3, p3, p3)


def _retention_kernel(qk_ref, v_ref, g_ref, decay_ref, xi_ref, zeta_ref, gamma_ref, gnw_ref, gnb_ref,
                      o_ref, state_ref, *, chunks):
    c = RET_CHUNK

    @pl.when(pl.program_id(1) == 0)
    def _():
        state_ref[...] = jnp.zeros_like(state_ref)

    lane = lax.broadcasted_iota(jnp.int32, (c, LANES), 1)
    k_off = RET_HEADS * RET_QK_DIM
    for n in range(chunks):
        rows = slice(n * c, (n + 1) * c)
        for h in range(RET_HEADS):
            pair, half = divmod(h, HEAD_PAIR)
            in_head = (lane >= half * RET_QK_DIM) & (lane < (half + 1) * RET_QK_DIM)
            q = jnp.where(in_head, qk_ref[0, rows, pair * LANES:(pair + 1) * LANES].astype(F32), 0.0)
            k = jnp.where(in_head, qk_ref[0, rows, k_off + pair * LANES:k_off + (pair + 1) * LANES].astype(F32), 0.0)
            vs = slice(h * RET_V_DIM, (h + 1) * RET_V_DIM)
            v = v_ref[0, rows, vs]
            state = state_ref[h]
            scores = _nt_dot(q.astype(BF16), k.astype(BF16)) * decay_ref[h]
            inner = jnp.dot(scores.astype(BF16), v, preferred_element_type=F32)
            cross = jnp.dot((q * xi_ref[h]).astype(BF16), state.astype(BF16), preferred_element_type=F32)
            kv = lax.dot_general((k * zeta_ref[h]).astype(BF16), v, (((0,), (0,)), ((), ())),
                                 preferred_element_type=F32)
            state_ref[h] = gamma_ref[h] * state + kv
            o = inner + cross
            mu = jnp.mean(o, axis=-1, keepdims=True)
            d = o - mu
            var = jnp.mean(d * d, axis=-1, keepdims=True)
            o = d * lax.rsqrt(var + GN_EPS) * gnw_ref[:, vs] + gnb_ref[:, vs]
            g = g_ref[0, rows, vs].astype(F32)
            o_ref[0, rows, vs] = (g * jax.nn.sigmoid(g) * o).astype(BF16)


def _retention(p3, decay, xi, zeta, gamma, gnw, gnb, tr):
    batch, seq, _ = p3.shape
    width = RET_HEADS * RET_V_DIM
    const = lambda shape: pl.BlockSpec(shape, lambda b, i: (0,) * len(shape))
    return pl.pallas_call(
        functools.partial(_retention_kernel, chunks=tr // RET_CHUNK),
        grid=(batch, seq // tr),
        in_specs=[
            pl.BlockSpec((1, tr, GROUP), lambda b, i: (b, i, G_RET_QK)),
            pl.BlockSpec((1, tr, GROUP), lambda b, i: (b, i, G_RET_V)),
            pl.BlockSpec((1, tr, GROUP), lambda b, i: (b, i, G_RET_G)),
            const((RET_HEADS, RET_CHUNK, RET_CHUNK)),
            const((RET_HEADS, RET_CHUNK, LANES)), const((RET_HEADS, RET_CHUNK, LANES)),
            const((RET_HEADS, 1, LANES)),
            const((1, width)), const((1, width)),
        ],
        out_specs=pl.BlockSpec((1, tr, width), lambda b, i: (b, i, 0)),
        out_shape=jax.ShapeDtypeStruct((batch, seq, width), BF16),
        scratch_shapes=[pltpu.VMEM((RET_HEADS, LANES, RET_V_DIM), F32)],
        compiler_params=pltpu.CompilerParams(
            dimension_semantics=("parallel", "arbitrary"), vmem_limit_bytes=VMEM_LIMIT),
        name="retention",
    )(p3, p3, p3, decay, xi, zeta, gamma, gnw, gnb)


def _merge_kernel(x_ref, gates_ref, oa_ref, ob_ref, oc_ref, wa_ref, wb_ref, wc_ref, wo_ref, o_ref):
    merged = None
    for idx, (o_in, w) in enumerate(((oa_ref, wa_ref), (ob_ref, wb_ref), (oc_ref, wc_ref))):
        gate = gates_ref[:, idx * D_MODEL:(idx + 1) * D_MODEL].astype(F32)
        term = jax.nn.sigmoid(gate) * jnp.dot(o_in[...], w[...], preferred_element_type=F32)
        merged = term if merged is None else merged + term
    o_ref[...] = x_ref[...] + jnp.dot(merged.astype(BF16), wo_ref[...], preferred_element_type=F32)


def _merge(x2, p2, oa, ob, oc, wa, wb, wc, wo, tm):
    tokens = x2.shape[0]
    row = lambda w: pl.BlockSpec((tm, w), lambda i: (i, 0))
    const = lambda shape: pl.BlockSpec(shape, lambda i: (0, 0))
    return pl.pallas_call(
        _merge_kernel,
        grid=(tokens // tm,),
        in_specs=[row(D_MODEL), row(3 * D_MODEL), row(GROUP), row(GROUP), row(GROUP),
                  const((GROUP, D_MODEL)), const((GROUP, D_MODEL)), const((GROUP, D_MODEL)),
                  const((D_MODEL, D_MODEL))],
        out_specs=row(D_MODEL),
        out_shape=jax.ShapeDtypeStruct((tokens, D_MODEL), F32),
        compiler_params=pltpu.CompilerParams(
            dimension_semantics=("parallel",), vmem_limit_bytes=VMEM_LIMIT),
        name="merge",
    )(x2, p2, oa, ob, oc, wa, wb, wc, wo)


def _mlp_kernel(x_ref, g_ref, wu_ref, wd_ref, fg_ref, o_ref, h_ref, *, tf, final_norm):
    x = x_ref[...]
    h_ref[...] = _rms(x, g_ref[...]).astype(BF16)
    acc = x
    for f in range(D_FF // tf):
        cols = slice(f * tf, (f + 1) * tf)
        u = jnp.maximum(jnp.dot(h_ref[...], wu_ref[:, cols], preferred_element_type=F32), 0.0)
        acc = acc + jnp.dot((u * u).astype(BF16), wd_ref[cols, :], preferred_element_type=F32)
    o_ref[...] = _rms(acc, fg_ref[...]) if final_norm else acc


def _mlp(x2, g, wu, wd, fg, tm, tf, final_norm):
    tokens = x2.shape[0]
    resident = lambda shape: pl.BlockSpec(shape, lambda i: (0, 0), pipeline_mode=pl.Buffered(1))
    return pl.pallas_call(
        functools.partial(_mlp_kernel, tf=tf, final_norm=final_norm),
        grid=(tokens // tm,),
        in_specs=[
            pl.BlockSpec((tm, D_MODEL), lambda i: (i, 0)),
            pl.BlockSpec((1, D_MODEL), lambda i: (0, 0)),
            resident((D_MODEL, D_FF)), resident((D_FF, D_MODEL)),
            pl.BlockSpec((1, D_MODEL), lambda i: (0, 0)),
        ],
        out_specs=pl.BlockSpec((tm, D_MODEL), lambda i: (i, 0)),
        out_shape=jax.ShapeDtypeStruct((tokens, D_MODEL), F32),
        scratch_shapes=[pltpu.VMEM((tm, D_MODEL), BF16)],
        compiler_params=pltpu.CompilerParams(
            dimension_semantics=("parallel",), vmem_limit_bytes=VMEM_LIMIT),
        name="mlp",
    )(x2, g, wu, wd, fg)


def _rope_cos_sin(seq, dim, theta):
    inv = 1.0 / (theta ** (jnp.arange(0, dim, 2, dtype=F32) / dim))
    ang = jnp.arange(seq, dtype=F32)[:, None] * inv[None, :]
    return jnp.cos(ang), jnp.sin(ang)


def _rope_tables(seq, dim, theta, period, offset):
    cos, sin = _rope_cos_sin(seq, dim, theta)
    half = dim // 2
    lo, hi = slice(offset, offset + half), slice(offset + half, offset + dim)
    c = jnp.ones((seq, period), F32).at[:, lo].set(cos).at[:, hi].set(cos)
    s_next = jnp.zeros((seq, period), F32).at[:, lo].set(-sin)
    s_prev = jnp.zeros((seq, period), F32).at[:, hi].set(sin)
    return jnp.tile(jnp.stack([c, s_next, s_prev]), (1, 1, LANES // period))


def _retention_tables():
    c = RET_CHUNK
    log_gamma = jnp.log(1.0 - 2.0 ** (-5.0 - jnp.arange(RET_HEADS, dtype=F32)))
    pos = jnp.arange(c, dtype=F32)
    diff = pos[:, None] - pos[None, :]
    decay = jnp.where(diff >= 0, jnp.exp(log_gamma[:, None, None] * diff), 0.0)
    xi = jnp.exp(log_gamma[:, None] * (pos + 1.0))
    zeta = jnp.exp(log_gamma[:, None] * (c - 1.0 - pos))
    gamma_c = jnp.exp(log_gamma * c)
    bcast = lambda t: jnp.broadcast_to(t[..., None], t.shape + (LANES,))
    return decay, bcast(xi), bcast(zeta), bcast(gamma_c[:, None])


def _pad_in_proj_weight(w):
    sizes = (MLA_Q_RANK, MLA_KV_RANK, MLA_ROPE, 512, 512, 512, 256, 256, 512, 512, D_MODEL, D_MODEL, D_MODEL)
    parts, start = [], 0
    for sz in sizes:
        parts.append(w[:, start:start + sz])
        start += sz
    cq, ckv, kr, mq, mk, mv, rq, rk, rv, rg, g_mla, g_moba, g_ret = parts
    zeros = lambda n: jnp.zeros((w.shape[0], n), w.dtype)
    kr_pad = jnp.concatenate([zeros(MLA_NOPE), kr, zeros(LANES - MLA_NOPE - MLA_ROPE)], axis=1)
    return jnp.concatenate([g_mla, g_moba, g_ret, cq, ckv, kr_pad, mq, mk, mv, rq, rk, rv, rg],
                           axis=1).astype(BF16)


def _pad_mla_weights(w_q_up, w_kv_up):
    d_qk = MLA_NOPE + MLA_ROPE
    wq = w_q_up.reshape(MLA_Q_RANK, MLA_HEADS, d_qk)
    wq = jnp.pad(wq, ((0, 0), (0, 0), (0, LANES - d_qk))).reshape(MLA_Q_RANK, MLA_HEADS * LANES)
    wkv = w_kv_up.reshape(MLA_KV_RANK, MLA_HEADS, MLA_NOPE + MLA_V)
    wk = jnp.pad(wkv[:, :, :MLA_NOPE], ((0, 0), (0, 0), (0, LANES - MLA_NOPE)))
    wk = wk.reshape(MLA_KV_RANK, MLA_HEADS * LANES)
    wv = wkv[:, :, MLA_NOPE:].reshape(MLA_KV_RANK, MLA_HEADS * MLA_V)
    return wq.astype(BF16), wk.astype(BF16), wv.astype(BF16)


def _tile_sizes(seq):
    pick = lambda want: want if seq % want == 0 else seq
    return dict(in_proj=pick(512), prep=pick(512), attn=pick(256), ret=pick(512),
                merge=pick(512), mlp=pick(512), mlp_ff=512)


def kernel(x, attn_norm, w_in, mla_q_norm, mla_kv_norm, mla_w_q_up, mla_w_kv_up, ret_gn_w, ret_gn_b,
           w_branch_mla, w_branch_moba, w_branch_ret, w_out, mlp_norm, w_mlp_up, w_mlp_down, final_norm):
    batch, seq, d = x.shape
    depth = w_in.shape[0]
    tiles = _tile_sizes(seq)
    mla_tab = _rope_tables(seq, MLA_ROPE, ROPE_THETA, LANES, MLA_NOPE)
    moba_tab = _rope_tables(seq, MOBA_HEAD_DIM // PARTIAL_ROPE_DIV, ROPE_THETA, MOBA_HEAD_DIM, 0)
    ret_tab = _rope_tables(seq, RET_QK_DIM, RET_THETA, RET_QK_DIM, 0)
    decay, xi, zeta, gamma_c = _retention_tables()
    row = lambda v: v.reshape(1, -1)

    x2 = x.reshape(batch * seq, d)
    for l in range(depth):
        w_in_p = _pad_in_proj_weight(w_in[l])
        wq, wk, wv = _pad_mla_weights(mla_w_q_up[l], mla_w_kv_up[l])
        p2 = _in_proj(x2, row(attn_norm[l]), w_in_p, moba_tab, ret_tab, seq, tiles["in_proj"])
        p3 = p2.reshape(batch, seq, IN_COLS_PADDED)
        q_mla, k_mla, v_mla = _mla_prep(p3, row(mla_q_norm[l]), row(mla_kv_norm[l]), wq, wk, wv,
                                        mla_tab, tiles["prep"])
        o_mla = _mla_attn(q_mla, k_mla, v_mla, tiles["attn"])
        o_moba = _moba_attn(p3)
        o_ret = _retention(p3, decay, xi, zeta, gamma_c, row(ret_gn_w[l]), row(ret_gn_b[l]), tiles["ret"])
        flat = lambda o: o.reshape(batch * seq, GROUP)
        x2 = _merge(x2, p2, flat(o_mla), flat(o_moba), flat(o_ret),
                    w_branch_mla[l].astype(BF16), w_branch_moba[l].astype(BF16),
                    w_branch_ret[l].astype(BF16), w_out[l].astype(BF16), tiles["merge"])
        x2 = _mlp(x2, row(mlp_norm[l]), w_mlp_up[l].astype(BF16), w_mlp_down[l].astype(BF16),
                  row(final_norm), tiles["mlp"], tiles["mlp_ff"], final_norm=(l == depth - 1))
    return x2.reshape(batch, seq, d)
```

```python
import functools

import jax
import jax.numpy as jnp
from jax import lax
from jax.experimental import pallas as pl
from jax.experimental.pallas import tpu as pltpu

F32 = jnp.float32
BF16 = jnp.bfloat16

D_MODEL = 1024
MLA_HEADS = 8
MLA_Q_RANK = 256
MLA_KV_RANK = 128
MLA_NOPE = 64
MLA_ROPE = 32
MLA_V = 64
MOBA_HEADS = 8
MOBA_HEAD_DIM = 64
MOBA_BLOCK = 256
MOBA_TOPK = 3
RET_HEADS = 4
RET_QK_DIM = 64
RET_V_DIM = 128
RET_CHUNK = 128
RET_THETA = 10000.0
ROPE_THETA = 500000.0
PARTIAL_ROPE_DIV = 4
D_FF = 4 * D_MODEL
NORM_EPS = 1e-6
GN_EPS = 1e-5
NEG_INF = -1e30
LOG2_E = 1.4426950408889634

LANES = 128
SUBLANES = 8
BF16_ROWS = 16
HEAD_PAIR = 2
SCORE_DEPTH = 3
TILE_PAIRS = 2
VMEM_LIMIT = 48 * 1024 * 1024

GROUP = 512
G_GATES = 0
G_MLA = 6
G_MOBA_Q, G_MOBA_K, G_MOBA_V = 7, 8, 9
G_RET_QK, G_RET_V, G_RET_G = 10, 11, 12
N_GROUPS = 13
IN_COLS_PADDED = N_GROUPS * GROUP


def _nt_dot(a, b):
    return lax.dot_general(a, b, (((1,), (1,)), ((), ())), preferred_element_type=F32)


def _rms(x, g):
    return x * lax.rsqrt(jnp.mean(x * x, axis=-1, keepdims=True) + NORM_EPS) * g


def _rope_lanes(y, tab_ref, half):
    return (y * tab_ref[0]
            + pltpu.roll(y, LANES - half, 1) * tab_ref[1]
            + pltpu.roll(y, half, 1) * tab_ref[2])


def _in_proj_kernel(x_ref, g_ref, w_ref, moba_tab, ret_tab, o_ref, h_ref):
    h_ref[...] = _rms(x_ref[...], g_ref[...]).astype(BF16)
    lane_tiles = GROUP // LANES
    for j in range(N_GROUPS):
        acc = jnp.dot(h_ref[...], w_ref[:, j * GROUP:(j + 1) * GROUP], preferred_element_type=F32)
        for r in range(lane_tiles):
            y = acc[:, r * LANES:(r + 1) * LANES]
            if j in (G_MOBA_Q, G_MOBA_K):
                y = _rope_lanes(y, moba_tab, MOBA_HEAD_DIM // PARTIAL_ROPE_DIV // 2)
            elif j == G_RET_QK:
                y = _rope_lanes(y, ret_tab, RET_QK_DIM // 2)
                if r >= lane_tiles // 2:
                    y = y * (RET_QK_DIM ** -0.5)
            o_ref[:, j * GROUP + r * LANES:j * GROUP + (r + 1) * LANES] = y.astype(BF16)


def _in_proj(x2, g, w, moba_tab, ret_tab, seq, tm):
    tokens = x2.shape[0]
    s_tiles = seq // tm
    return pl.pallas_call(
        _in_proj_kernel,
        grid=(tokens // tm,),
        in_specs=[
            pl.BlockSpec((tm, D_MODEL), lambda i: (i, 0)),
            pl.BlockSpec((1, D_MODEL), lambda i: (0, 0)),
            pl.BlockSpec((D_MODEL, IN_COLS_PADDED), lambda i: (0, 0), pipeline_mode=pl.Buffered(1)),
            pl.BlockSpec((3, tm, LANES), lambda i: (0, i % s_tiles, 0)),
            pl.BlockSpec((3, tm, LANES), lambda i: (0, i % s_tiles, 0)),
        ],
        out_specs=pl.BlockSpec((tm, IN_COLS_PADDED), lambda i: (i, 0)),
        out_shape=jax.ShapeDtypeStruct((tokens, IN_COLS_PADDED), BF16),
        scratch_shapes=[pltpu.VMEM((tm, D_MODEL), BF16)],
        compiler_params=pltpu.CompilerParams(
            dimension_semantics=("parallel",), vmem_limit_bytes=VMEM_LIMIT),
        name="in_proj",
    )(x2, g, w, moba_tab, ret_tab)


def _mla_prep_kernel(p_ref, qg_ref, kvg_ref, wq_ref, wk_ref, wv_ref, tab, q_ref, k_ref, v_ref):
    blk = p_ref[...]
    cq = blk[:, :MLA_Q_RANK].astype(F32)
    ckv = blk[:, MLA_Q_RANK:MLA_Q_RANK + MLA_KV_RANK].astype(F32)
    kr = blk[:, MLA_Q_RANK + MLA_KV_RANK:].astype(F32)
    cqn = _rms(cq, qg_ref[...]).astype(BF16)
    ckvn = _rms(ckv, kvg_ref[...]).astype(BF16)
    q = jnp.dot(cqn, wq_ref[...], preferred_element_type=F32)
    k = jnp.dot(ckvn, wk_ref[...], preferred_element_type=F32)
    v = jnp.dot(ckvn, wv_ref[...], preferred_element_type=F32)
    kpe = _rope_lanes(kr, tab, MLA_ROPE // 2)
    scale = (MLA_NOPE + MLA_ROPE) ** -0.5 * LOG2_E
    for h in range(MLA_HEADS):
        sl = slice(h * LANES, (h + 1) * LANES)
        q_ref[0, h] = (_rope_lanes(q[:, sl], tab, MLA_ROPE // 2) * scale).astype(BF16)
        k_ref[0, h] = (k[:, sl] + kpe).astype(BF16)
    v_ref[0] = v.astype(BF16)


def _mla_prep(p3, qg, kvg, wq, wk, wv, tab, tm):
    batch, seq, _ = p3.shape
    hs = jax.ShapeDtypeStruct((batch, MLA_HEADS, seq, LANES), BF16)
    const = lambda shape: pl.BlockSpec(shape, lambda b, i: (0,) * len(shape))
    return pl.pallas_call(
        _mla_prep_kernel,
        grid=(batch, seq // tm),
        in_specs=[
            pl.BlockSpec((None, tm, GROUP), lambda b, i: (b, i, G_MLA)),
            const((1, MLA_Q_RANK)), const((1, MLA_KV_RANK)),
            const((MLA_Q_RANK, MLA_HEADS * LANES)), const((MLA_KV_RANK, MLA_HEADS * LANES)),
            const((MLA_KV_RANK, MLA_HEADS * MLA_V)),
            pl.BlockSpec((3, tm, LANES), lambda b, i: (0, i, 0)),
        ],
        out_specs=[
            pl.BlockSpec((1, MLA_HEADS, tm, LANES), lambda b, i: (b, 0, i, 0)),
            pl.BlockSpec((1, MLA_HEADS, tm, LANES), lambda b, i: (b, 0, i, 0)),
            pl.BlockSpec((1, tm, MLA_HEADS * MLA_V), lambda b, i: (b, i, 0)),
        ],
        out_shape=[hs, hs, jax.ShapeDtypeStruct((batch, seq, MLA_HEADS * MLA_V), BF16)],
        compiler_params=pltpu.CompilerParams(
            dimension_semantics=("parallel", "parallel"), vmem_limit_bytes=VMEM_LIMIT),
        name="mla_prep",
    )(p3, qg, kvg, wq, wk, wv, tab)


_HEADS = range(HEAD_PAIR)
_ROWS = range(2)


def _fill_v_transposed(v_ref, vt_ref, t, head_dim):
    sub = lax.broadcasted_iota(jnp.int32, (BF16_ROWS, t), 0)
    ones_block = jnp.where(sub == 0, 1.0, 0.0)
    for j in range(vt_ref.shape[1]):
        vt = v_ref[0, j * t:(j + 1) * t, :].astype(F32).T
        for h in _HEADS:
            own = vt[h * head_dim:(h + 1) * head_dim]
            vt_ref[h, j] = jnp.concatenate([own, ones_block], axis=0).astype(BF16)


def _fill_causal_cap(cap_ref, t):
    key = lax.broadcasted_iota(jnp.int32, (t, t), 0)
    query = lax.broadcasted_iota(jnp.int32, (t, t), 1)
    cap_ref[...] = jnp.where(key <= query, jnp.inf, NEG_INF)


def _attend_rows(g, n_tiles, t, load_qt, load_k, vt_ref, write_out, scratch, head_dim, row_cap=None):
    causal_cap_ref, bufs, m_ref, acc_ref = scratch
    depth = bufs.shape[0]
    q_tile = [g, n_tiles - 1 - g]
    m_ref[...] = jnp.full_like(m_ref, -jnp.inf)
    acc_ref[...] = jnp.zeros_like(acc_ref)

    def locate(n):
        row = (n > q_tile[0]).astype(jnp.int32)
        return row, jnp.where(row == 0, q_tile[0] - n, n - (q_tile[0] + 1))

    def issue_scores(n, slot):
        row, kj = locate(n)
        ks, qts = load_k(kj), load_qt(row)
        for h in _HEADS:
            bufs[slot, h] = jnp.dot(ks[h], qts[h], preferred_element_type=F32)

    def softmax_step(n, slot, diagonal=False):
        row, kj = locate(n)
        chain = [HEAD_PAIR * row + h for h in _HEADS]
        s = [bufs[slot, h] for h in _HEADS]
        if diagonal:
            s = [jnp.minimum(s[h], causal_cap_ref[...]) for h in _HEADS]
        elif row_cap is not None:
            s = [jnp.minimum(s[h], row_cap(row, h, kj)) for h in _HEADS]
        m_old = [m_ref[chain[h]] for h in _HEADS]
        m_new = [jnp.maximum(m_old[h], jnp.max(s[h], axis=0, keepdims=True)) for h in _HEADS]
        alpha = [jnp.exp2(m_old[h] - m_new[h]) for h in _HEADS]
        p = [jnp.exp2(s[h] - m_new[h]).astype(BF16) for h in _HEADS]
        pv = [jnp.dot(vt_ref[h, kj], p[h], preferred_element_type=F32) for h in _HEADS]
        for h in _HEADS:
            m_ref[chain[h]] = m_new[h]
            acc_ref[chain[h]] = alpha[h] * acc_ref[chain[h]] + pv[h]

    steps = n_tiles + 1
    for n in range(min(depth, steps)):
        issue_scores(n, n)
    for n in range(steps):
        softmax_step(n, n % depth, diagonal=n in (0, n_tiles))
        if n + depth < steps:
            issue_scores(n + depth, n % depth)

    for row in _ROWS:
        outs = []
        for h in _HEADS:
            acc = acc_ref[HEAD_PAIR * row + h]
            outs.append(acc[:head_dim] * (1.0 / acc[head_dim:head_dim + 1]))
        ot = jnp.concatenate(outs, axis=0)
        write_out(q_tile[row], ot.T.astype(BF16))


def _attn_scratch(seq, t, head_dim):
    chains = HEAD_PAIR * len(_ROWS)
    v_rows = head_dim + BF16_ROWS
    return [pltpu.VMEM((HEAD_PAIR, seq // t, v_rows, t), BF16),
            pltpu.VMEM((t, t), F32),
            pltpu.VMEM((TILE_PAIRS, SCORE_DEPTH, HEAD_PAIR, t, t), F32),
            pltpu.VMEM((TILE_PAIRS, chains, 1, t), F32),
            pltpu.VMEM((TILE_PAIRS, chains, v_rows, t), F32)]


def _pair_scratch(scratch, pair):
    causal_cap_ref, bufs, m_ref, acc_ref = scratch
    return causal_cap_ref, bufs.at[pair], m_ref.at[pair], acc_ref.at[pair]


def _tile_rows(i, t):
    return pl.ds(pl.multiple_of(i * t, t), t)


def _mla_attn_kernel(q_ref, k_ref, v_ref, o_ref, qt_ref, vt_ref, *scratch, t, n_tiles):
    step = pl.program_id(2)

    @pl.when(step == 0)
    def _():
        _fill_v_transposed(v_ref, vt_ref, t, MLA_V)
        _fill_causal_cap(scratch[0], t)

    def load_k(kj):
        return [k_ref[0, h, _tile_rows(kj, t), :] for h in _HEADS]

    def write_out(qt, tile):
        o_ref[0, _tile_rows(qt, t), :] = tile

    pair_g = [TILE_PAIRS * step + pair for pair in range(TILE_PAIRS)]
    for pair, g in enumerate(pair_g):
        for row, qi in enumerate((g, n_tiles - 1 - g)):
            for h in _HEADS:
                qt_ref[pair, row, h] = q_ref[0, h, _tile_rows(qi, t), :].astype(F32).T.astype(BF16)
    for pair, g in enumerate(pair_g):
        load_qt = lambda row, pair=pair: [qt_ref[pair, row, h] for h in _HEADS]
        _attend_rows(g, n_tiles, t, load_qt, load_k, vt_ref, write_out, _pair_scratch(scratch, pair), MLA_V)


def _mla_attn(q, k, v, t):
    batch, heads, seq, _ = q.shape
    n_tiles = seq // t
    assert n_tiles % (2 * TILE_PAIRS) == 0
    return pl.pallas_call(
        functools.partial(_mla_attn_kernel, t=t, n_tiles=n_tiles),
        grid=(batch, heads // HEAD_PAIR, n_tiles // (2 * TILE_PAIRS)),
        in_specs=[
            pl.BlockSpec((1, HEAD_PAIR, seq, LANES), lambda b, h, g: (b, h, 0, 0)),
            pl.BlockSpec((1, HEAD_PAIR, seq, LANES), lambda b, h, g: (b, h, 0, 0)),
            pl.BlockSpec((1, seq, LANES), lambda b, h, g: (b, 0, h)),
        ],
        out_specs=pl.BlockSpec((1, seq, LANES), lambda b, h, g: (b, 0, h)),
        out_shape=jax.ShapeDtypeStruct((batch, seq, heads * MLA_V), BF16),
        scratch_shapes=[pltpu.VMEM((TILE_PAIRS, len(_ROWS), HEAD_PAIR, LANES, t), BF16)]
                       + _attn_scratch(seq, t, MLA_V),
        compiler_params=pltpu.CompilerParams(
            dimension_semantics=("parallel", "parallel", "arbitrary"), vmem_limit_bytes=VMEM_LIMIT),
        name="mla_attn",
    )(q, k, v)


def _moba_kernel(q_ref, k_ref, v_ref, o_ref, kmean_ref, qh_ref, cap_ref, vt_ref, *scratch, nb):
    t = MOBA_BLOCK
    step = pl.program_id(2)

    @pl.when(step == 0)
    def _():
        _fill_v_transposed(v_ref, vt_ref, t, MOBA_HEAD_DIM)
        _fill_causal_cap(scratch[0], t)
        kmean_ref[...] = jnp.zeros_like(kmean_ref)
        for j in range(nb):
            kb = k_ref[0, j * t:(j + 1) * t, :].astype(F32)
            kmean_ref[j:j + 1, :] = jnp.mean(kb, axis=0, keepdims=True)

    lane = lax.broadcasted_iota(jnp.int32, (t, LANES), 1)
    lane_k = lax.broadcasted_iota(jnp.int32, (LANES, LANES), 1)
    blk_id = lax.broadcasted_iota(jnp.int32, (nb, t), 0)
    nb_rows = -(-nb // SUBLANES) * SUBLANES
    pair_g = [TILE_PAIRS * step + pair for pair in range(TILE_PAIRS)]
    for pair, g in enumerate(pair_g):
        for row, qi in enumerate((g, nb - 1 - g)):
            n_sel = jnp.minimum(qi, MOBA_TOPK).astype(F32)
            q_pair = q_ref[0, _tile_rows(qi, t), :]
            for hh in _HEADS:
                in_head = (lane >= hh * MOBA_HEAD_DIM) & (lane < (hh + 1) * MOBA_HEAD_DIM)
                qt_gate = jnp.where(in_head, q_pair, jnp.zeros_like(q_pair)).astype(F32).T
                qh_ref[pair, row, hh] = (qt_gate * (MOBA_HEAD_DIM ** -0.5 * LOG2_E)).astype(BF16)
                in_head_k = (lane_k >= hh * MOBA_HEAD_DIM) & (lane_k < (hh + 1) * MOBA_HEAD_DIM)
                km = jnp.where(in_head_k, kmean_ref[...], 0.0)[:nb_rows]
                gate = jnp.dot(km, qt_gate, precision=lax.Precision.HIGHEST, preferred_element_type=F32)[:nb]
                gate = jnp.where(blk_id < qi, gate, NEG_INF)
                rank = jnp.zeros((nb, t), F32)
                for jp in range(nb):
                    gj = gate[jp:jp + 1, :]
                    beats = (gj > gate) | ((gj == gate) & (blk_id > jp))
                    rank = rank + jnp.where(beats, 1.0, 0.0)
                keep = ((rank < n_sel) & (blk_id < qi)) | (blk_id == qi)
                cap = jnp.where(keep, jnp.inf, NEG_INF)
                for j in range(nb):
                    cap_ref[pair, row, hh, j] = cap[j:j + 1, :]

    def load_k(kj):
        return [k_ref[0, _tile_rows(kj, t), :]] * HEAD_PAIR

    def write_out(qt, tile):
        o_ref[0, _tile_rows(qt, t), :] = tile

    for pair, g in enumerate(pair_g):
        load_qt = lambda row, pair=pair: [qh_ref[pair, row, h] for h in _HEADS]
        row_cap = lambda row, h, kj, pair=pair: cap_ref[pair, row, h, kj]
        _attend_rows(g, nb, t, load_qt, load_k, vt_ref, write_out, _pair_scratch(scratch, pair),
                     MOBA_HEAD_DIM, row_cap=row_cap)


def _moba_attn(p3):
    batch, seq, _ = p3.shape
    t = MOBA_BLOCK
    nb = seq // t
    assert nb % (2 * TILE_PAIRS) == 0
    per_group = GROUP // LANES
    return pl.pallas_call(
        functools.partial(_moba_kernel, nb=nb),
        grid=(batch, MOBA_HEADS // HEAD_PAIR, nb // (2 * TILE_PAIRS)),
        in_specs=[
            pl.BlockSpec((1, seq, LANES), lambda b, h, g: (b, 0, G_MOBA_Q * per_group + h)),
            pl.BlockSpec((1, seq, LANES), lambda b, h, g: (b, 0, G_MOBA_K * per_group + h)),
            pl.BlockSpec((1, seq, LANES), lambda b, h, g: (b, 0, G_MOBA_V * per_group + h)),
        ],
        out_specs=pl.BlockSpec((1, seq, LANES), lambda b, h, g: (b, 0, h)),
        out_shape=jax.ShapeDtypeStruct((batch, seq, MOBA_HEADS * MOBA_HEAD_DIM), BF16),
        scratch_shapes=[pltpu.VMEM((LANES, LANES), F32),
                        pltpu.VMEM((TILE_PAIRS, len(_ROWS), HEAD_PAIR, LANES, t), BF16),
                        pltpu.VMEM((TILE_PAIRS, len(_ROWS), HEAD_PAIR, nb, 1, t), F32)]
                       + _attn_scratch(seq, t, MOBA_HEAD_DIM),
        compiler_params=pltpu.CompilerParams(
            dimension_semantics=("parallel", "parallel", "arbitrary"), vmem_limit_bytes=VMEM_LIMIT),
        name="moba_attn",
    )(p3, p3, p3)


def _retention_kernel(qk_ref, v_ref, g_ref, decay_ref, xi_ref, zeta_ref, gamma_ref, gnw_ref, gnb_ref,
                      o_ref, state_ref, *, chunks):
    c = RET_CHUNK

    @pl.when(pl.program_id(1) == 0)
    def _():
        state_ref[...] = jnp.zeros_like(state_ref)

    lane = lax.broadcasted_iota(jnp.int32, (c, LANES), 1)
    k_off = RET_HEADS * RET_QK_DIM
    for n in range(chunks):
        rows = slice(n * c, (n + 1) * c)
        for h in range(RET_HEADS):
            pair, half = divmod(h, HEAD_PAIR)
            in_head = (lane >= half * RET_QK_DIM) & (lane < (half + 1) * RET_QK_DIM)
            q = jnp.where(in_head, qk_ref[0, rows, pair * LANES:(pair + 1) * LANES].astype(F32), 0.0)
            k = jnp.where(in_head, qk_ref[0, rows, k_off + pair * LANES:k_off + (pair + 1) * LANES].astype(F32), 0.0)
            vs = slice(h * RET_V_DIM, (h + 1) * RET_V_DIM)
            v = v_ref[0, rows, vs]
            state = state_ref[h]
            scores = _nt_dot(q.astype(BF16), k.astype(BF16)) * decay_ref[h]
            inner = jnp.dot(scores.astype(BF16), v, preferred_element_type=F32)
            cross = jnp.dot((q * xi_ref[h]).astype(BF16), state.astype(BF16), preferred_element_type=F32)
            kv = lax.dot_general((k * zeta_ref[h]).astype(BF16), v, (((0,), (0,)), ((), ())),
                                 preferred_element_type=F32)
            state_ref[h] = gamma_ref[h] * state + kv
            o = inner + cross
            mu = jnp.mean(o, axis=-1, keepdims=True)
            d = o - mu
            var = jnp.mean(d * d, axis=-1, keepdims=True)
            o = d * lax.rsqrt(var + GN_EPS) * gnw_ref[:, vs] + gnb_ref[:, vs]
            g = g_ref[0, rows, vs].astype(F32)
            o_ref[0, rows, vs] = (g * jax.nn.sigmoid(g) * o).astype(BF16)


def _retention(p3, decay, xi, zeta, gamma, gnw, gnb, tr):
    batch, seq, _ = p3.shape
    width = RET_HEADS * RET_V_DIM
    const = lambda shape: pl.BlockSpec(shape, lambda b, i: (0,) * len(shape))
    return pl.pallas_call(
        functools.partial(_retention_kernel, chunks=tr // RET_CHUNK),
        grid=(batch, seq // tr),
        in_specs=[
            pl.BlockSpec((1, tr, GROUP), lambda b, i: (b, i, G_RET_QK)),
            pl.BlockSpec((1, tr, GROUP), lambda b, i: (b, i, G_RET_V)),
            pl.BlockSpec((1, tr, GROUP), lambda b, i: (b, i, G_RET_G)),
            const((RET_HEADS, RET_CHUNK, RET_CHUNK)),
            const((RET_HEADS, RET_CHUNK, LANES)), const((RET_HEADS, RET_CHUNK, LANES)),
            const((RET_HEADS, 1, LANES)),
            const((1, width)), const((1, width)),
        ],
        out_specs=pl.BlockSpec((1, tr, width), lambda b, i: (b, i, 0)),
        out_shape=jax.ShapeDtypeStruct((batch, seq, width), BF16),
        scratch_shapes=[pltpu.VMEM((RET_HEADS, LANES, RET_V_DIM), F32)],
        compiler_params=pltpu.CompilerParams(
            dimension_semantics=("parallel", "arbitrary"), vmem_limit_bytes=VMEM_LIMIT),
        name="retention",
    )(p3, p3, p3, decay, xi, zeta, gamma, gnw, gnb)


def _merge_kernel(x_ref, gates_ref, oa_ref, ob_ref, oc_ref, wa_ref, wb_ref, wc_ref, wo_ref, o_ref):
    merged = None
    for idx, (o_in, w) in enumerate(((oa_ref, wa_ref), (ob_ref, wb_ref), (oc_ref, wc_ref))):
        gate = gates_ref[:, idx * D_MODEL:(idx + 1) * D_MODEL].astype(F32)
        term = jax.nn.sigmoid(gate) * jnp.dot(o_in[...], w[...], preferred_element_type=F32)
        merged = term if merged is None else merged + term
    o_ref[...] = x_ref[...] + jnp.dot(merged.astype(BF16), wo_ref[...], preferred_element_type=F32)


def _merge(x2, p2, oa, ob, oc, wa, wb, wc, wo, tm):
    tokens = x2.shape[0]
    row = lambda w: pl.BlockSpec((tm, w), lambda i: (i, 0))
    const = lambda shape: pl.BlockSpec(shape, lambda i: (0, 0))
    return pl.pallas_call(
        _merge_kernel,
        grid=(tokens // tm,),
        in_specs=[row(D_MODEL), row(3 * D_MODEL), row(GROUP), row(GROUP), row(GROUP),
                  const((GROUP, D_MODEL)), const((GROUP, D_MODEL)), const((GROUP, D_MODEL)),
                  const((D_MODEL, D_MODEL))],
        out_specs=row(D_MODEL),
        out_shape=jax.ShapeDtypeStruct((tokens, D_MODEL), F32),
        compiler_params=pltpu.CompilerParams(
            dimension_semantics=("parallel",), vmem_limit_bytes=VMEM_LIMIT),
        name="merge",
    )(x2, p2, oa, ob, oc, wa, wb, wc, wo)


def _mlp_kernel(x_ref, g_ref, wu_ref, wd_ref, fg_ref, o_ref, h_ref, *, tf, final_norm):
    x = x_ref[...]
    h_ref[...] = _rms(x, g_ref[...]).astype(BF16)
    acc = x
    for f in range(D_FF // tf):
        cols = slice(f * tf, (f + 1) * tf)
        u = jnp.maximum(jnp.dot(h_ref[...], wu_ref[:, cols], preferred_element_type=F32), 0.0)
        acc = acc + jnp.dot((u * u).astype(BF16), wd_ref[cols, :], preferred_element_type=F32)
    o_ref[...] = _rms(acc, fg_ref[...]) if final_norm else acc


def _mlp(x2, g, wu, wd, fg, tm, tf, final_norm):
    tokens = x2.shape[0]
    resident = lambda shape: pl.BlockSpec(shape, lambda i: (0, 0), pipeline_mode=pl.Buffered(1))
    return pl.pallas_call(
        functools.partial(_mlp_kernel, tf=tf, final_norm=final_norm),
        grid=(tokens // tm,),
        in_specs=[
            pl.BlockSpec((tm, D_MODEL), lambda i: (i, 0)),
            pl.BlockSpec((1, D_MODEL), lambda i: (0, 0)),
            resident((D_MODEL, D_FF)), resident((D_FF, D_MODEL)),
            pl.BlockSpec((1, D_MODEL), lambda i: (0, 0)),
        ],
        out_specs=pl.BlockSpec((tm, D_MODEL), lambda i: (i, 0)),
        out_shape=jax.ShapeDtypeStruct((tokens, D_MODEL), F32),
        scratch_shapes=[pltpu.VMEM((tm, D_MODEL), BF16)],
        compiler_params=pltpu.CompilerParams(
            dimension_semantics=("parallel",), vmem_limit_bytes=VMEM_LIMIT),
        name="mlp",
    )(x2, g, wu, wd, fg)


def _rope_cos_sin(seq, dim, theta):
    inv = 1.0 / (theta ** (jnp.arange(0, dim, 2, dtype=F32) / dim))
    ang = jnp.arange(seq, dtype=F32)[:, None] * inv[None, :]
    return jnp.cos(ang), jnp.sin(ang)


def _rope_tables(seq, dim, theta, period, offset):
    cos, sin = _rope_cos_sin(seq, dim, theta)
    half = dim // 2
    lo, hi = slice(offset, offset + half), slice(offset + half, offset + dim)
    c = jnp.ones((seq, period), F32).at[:, lo].set(cos).at[:, hi].set(cos)
    s_next = jnp.zeros((seq, period), F32).at[:, lo].set(-sin)
    s_prev = jnp.zeros((seq, period), F32).at[:, hi].set(sin)
    return jnp.tile(jnp.stack([c, s_next, s_prev]), (1, 1, LANES // period))


def _retention_tables():
    c = RET_CHUNK
    log_gamma = jnp.log(1.0 - 2.0 ** (-5.0 - jnp.arange(RET_HEADS, dtype=F32)))
    pos = jnp.arange(c, dtype=F32)
    diff = pos[:, None] - pos[None, :]
    decay = jnp.where(diff >= 0, jnp.exp(log_gamma[:, None, None] * diff), 0.0)
    xi = jnp.exp(log_gamma[:, None] * (pos + 1.0))
    zeta = jnp.exp(log_gamma[:, None] * (c - 1.0 - pos))
    gamma_c = jnp.exp(log_gamma * c)
    bcast = lambda t: jnp.broadcast_to(t[..., None], t.shape + (LANES,))
    return decay, bcast(xi), bcast(zeta), bcast(gamma_c[:, None])


def _pad_in_proj_weight(w):
    w = w.astype(BF16)
    latent = MLA_Q_RANK + MLA_KV_RANK
    mixers = latent + MLA_ROPE
    gates = w.shape[1] - 3 * D_MODEL
    zeros = lambda n: jnp.zeros((w.shape[0], n), BF16)
    return jnp.concatenate([w[:, gates:], w[:, :latent], zeros(MLA_NOPE), w[:, latent:mixers],
                            zeros(LANES - MLA_NOPE - MLA_ROPE), w[:, mixers:gates]], axis=1)


def _pad_mla_weights(w_q_up, w_kv_up):
    d_qk = MLA_NOPE + MLA_ROPE
    wq = w_q_up.reshape(MLA_Q_RANK, MLA_HEADS, d_qk)
    wq = jnp.pad(wq, ((0, 0), (0, 0), (0, LANES - d_qk))).reshape(MLA_Q_RANK, MLA_HEADS * LANES)
    wkv = w_kv_up.reshape(MLA_KV_RANK, MLA_HEADS, MLA_NOPE + MLA_V)
    wk = jnp.pad(wkv[:, :, :MLA_NOPE], ((0, 0), (0, 0), (0, LANES - MLA_NOPE)))
    wk = wk.reshape(MLA_KV_RANK, MLA_HEADS * LANES)
    wv = wkv[:, :, MLA_NOPE:].reshape(MLA_KV_RANK, MLA_HEADS * MLA_V)
    return wq.astype(BF16), wk.astype(BF16), wv.astype(BF16)


def _tile_sizes(seq):
    pick = lambda want: want if seq % want == 0 else seq
    return dict(in_proj=pick(512), prep=pick(512), attn=pick(256), ret=pick(512),
                merge=pick(512), mlp=pick(512), mlp_ff=512)


def kernel(x, attn_norm, w_in, mla_q_norm, mla_kv_norm, mla_w_q_up, mla_w_kv_up, ret_gn_w, ret_gn_b,
           w_branch_mla, w_branch_moba, w_branch_ret, w_out, mlp_norm, w_mlp_up, w_mlp_down, final_norm):
    batch, seq, d = x.shape
    depth = w_in.shape[0]
    tiles = _tile_sizes(seq)
    mla_tab = _rope_tables(seq, MLA_ROPE, ROPE_THETA, LANES, MLA_NOPE)
    moba_tab = _rope_tables(seq, MOBA_HEAD_DIM // PARTIAL_ROPE_DIV, ROPE_THETA, MOBA_HEAD_DIM, 0)
    ret_tab = _rope_tables(seq, RET_QK_DIM, RET_THETA, RET_QK_DIM, 0)
    decay, xi, zeta, gamma_c = _retention_tables()
    row = lambda v: v.reshape(1, -1)

    x2 = x.reshape(batch * seq, d)
    for l in range(depth):
        w_in_p = _pad_in_proj_weight(w_in[l])
        wq, wk, wv = _pad_mla_weights(mla_w_q_up[l], mla_w_kv_up[l])
        p2 = _in_proj(x2, row(attn_norm[l]), w_in_p, moba_tab, ret_tab, seq, tiles["in_proj"])
        p3 = p2.reshape(batch, seq, IN_COLS_PADDED)
        q_mla, k_mla, v_mla = _mla_prep(p3, row(mla_q_norm[l]), row(mla_kv_norm[l]), wq, wk, wv,
                                        mla_tab, tiles["prep"])
        o_mla = _mla_attn(q_mla, k_mla, v_mla, tiles["attn"])
        o_moba = _moba_attn(p3)
        o_ret = _retention(p3, decay, xi, zeta, gamma_c, row(ret_gn_w[l]), row(ret_gn_b[l]), tiles["ret"])
        flat = lambda o: o.reshape(batch * seq, GROUP)
        x2 = _merge(x2, p2, flat(o_mla), flat(o_moba), flat(o_ret),
                    w_branch_mla[l].astype(BF16), w_branch_moba[l].astype(BF16),
                    w_branch_ret[l].astype(BF16), w_out[l].astype(BF16), tiles["merge"])
        x2 = _mlp(x2, row(mlp_norm[l]), w_mlp_up[l].astype(BF16), w_mlp_down[l].astype(BF16),
                  row(final_norm), tiles["mlp"], tiles["mlp_ff"], final_norm=(l == depth - 1))
    return x2.reshape(batch, seq, d)
```

```python
import functools

import jax
import jax.numpy as jnp
from jax import lax
from jax.experimental import pallas as pl
from jax.experimental.pallas import tpu as pltpu

F32 = jnp.float32
BF16 = jnp.bfloat16

D_MODEL = 1024
MLA_HEADS = 8
MLA_Q_RANK = 256
MLA_KV_RANK = 128
MLA_NOPE = 64
MLA_ROPE = 32
MLA_V = 64
MOBA_HEADS = 8
MOBA_HEAD_DIM = 64
MOBA_BLOCK = 256
MOBA_TOPK = 3
RET_HEADS = 4
RET_QK_DIM = 64
RET_V_DIM = 128
RET_CHUNK = 128
RET_THETA = 10000.0
ROPE_THETA = 500000.0
PARTIAL_ROPE_DIV = 4
D_FF = 4 * D_MODEL
NORM_EPS = 1e-6
GN_EPS = 1e-5
NEG_INF = -1e30
LOG2_E = 1.4426950408889634

LANES = 128
SUBLANES = 8
BF16_ROWS = 16
HEAD_PAIR = 2
SCORE_DEPTH = 3
TILE_PAIRS = 2
VMEM_LIMIT = 48 * 1024 * 1024

GROUP = 512
G_GATES = 0
G_MLA = 6
G_MOBA_Q, G_MOBA_K, G_MOBA_V = 7, 8, 9
G_RET_QK, G_RET_V, G_RET_G = 10, 11, 12
N_GROUPS = 13
IN_COLS_PADDED = N_GROUPS * GROUP


def _nt_dot(a, b):
    return lax.dot_general(a, b, (((1,), (1,)), ((), ())), preferred_element_type=F32)


def _rms(x, g):
    return x * lax.rsqrt(jnp.mean(x * x, axis=-1, keepdims=True) + NORM_EPS) * g


def _rope_lanes(y, tab_ref, half):
    return (y * tab_ref[0]
            + pltpu.roll(y, LANES - half, 1) * tab_ref[1]
            + pltpu.roll(y, half, 1) * tab_ref[2])


def _in_proj_kernel(x_ref, g_ref, wt_ref, moba_tab, ret_tab, o_ref, h_ref):
    h_ref[...] = _rms(x_ref[...], g_ref[...]).astype(BF16)
    lane_tiles = GROUP // LANES
    for j in range(N_GROUPS):
        acc = _nt_dot(h_ref[...], wt_ref[j * GROUP:(j + 1) * GROUP, :])
        for r in range(lane_tiles):
            y = acc[:, r * LANES:(r + 1) * LANES]
            if j in (G_MOBA_Q, G_MOBA_K):
                y = _rope_lanes(y, moba_tab, MOBA_HEAD_DIM // PARTIAL_ROPE_DIV // 2)
            elif j == G_RET_QK:
                y = _rope_lanes(y, ret_tab, RET_QK_DIM // 2)
                if r >= lane_tiles // 2:
                    y = y * (RET_QK_DIM ** -0.5)
            o_ref[:, j * GROUP + r * LANES:j * GROUP + (r + 1) * LANES] = y.astype(BF16)


def _in_proj(x2, g, w, moba_tab, ret_tab, seq, tm):
    tokens = x2.shape[0]
    s_tiles = seq // tm
    return pl.pallas_call(
        _in_proj_kernel,
        grid=(tokens // tm,),
        in_specs=[
            pl.BlockSpec((tm, D_MODEL), lambda i: (i, 0)),
            pl.BlockSpec((1, D_MODEL), lambda i: (0, 0)),
            pl.BlockSpec((IN_COLS_PADDED, D_MODEL), lambda i: (0, 0), pipeline_mode=pl.Buffered(1)),
            pl.BlockSpec((3, tm, LANES), lambda i: (0, i % s_tiles, 0)),
            pl.BlockSpec((3, tm, LANES), lambda i: (0, i % s_tiles, 0)),
        ],
        out_specs=pl.BlockSpec((tm, IN_COLS_PADDED), lambda i: (i, 0)),
        out_shape=jax.ShapeDtypeStruct((tokens, IN_COLS_PADDED), BF16),
        scratch_shapes=[pltpu.VMEM((tm, D_MODEL), BF16)],
        compiler_params=pltpu.CompilerParams(
            dimension_semantics=("parallel",), vmem_limit_bytes=VMEM_LIMIT),
        name="in_proj",
    )(x2, g, w, moba_tab, ret_tab)


def _mla_prep_kernel(p_ref, qg_ref, kvg_ref, wq_ref, wk_ref, wv_ref, tab, q_ref, k_ref, v_ref):
    blk = p_ref[...]
    cq = blk[:, :MLA_Q_RANK].astype(F32)
    ckv = blk[:, MLA_Q_RANK:MLA_Q_RANK + MLA_KV_RANK].astype(F32)
    kr = blk[:, MLA_Q_RANK + MLA_KV_RANK:].astype(F32)
    cqn = _rms(cq, qg_ref[...]).astype(BF16)
    ckvn = _rms(ckv, kvg_ref[...]).astype(BF16)
    q = jnp.dot(cqn, wq_ref[...], preferred_element_type=F32)
    k = jnp.dot(ckvn, wk_ref[...], preferred_element_type=F32)
    v = jnp.dot(ckvn, wv_ref[...], preferred_element_type=F32)
    kpe = _rope_lanes(kr, tab, MLA_ROPE // 2)
    scale = (MLA_NOPE + MLA_ROPE) ** -0.5 * LOG2_E
    for h in range(MLA_HEADS):
        sl = slice(h * LANES, (h + 1) * LANES)
        q_ref[0, h] = (_rope_lanes(q[:, sl], tab, MLA_ROPE // 2) * scale).astype(BF16)
        k_ref[0, h] = (k[:, sl] + kpe).astype(BF16)
    v_ref[0] = v.astype(BF16)


def _mla_prep(p3, qg, kvg, wq, wk, wv, tab, tm):
    batch, seq, _ = p3.shape
    hs = jax.ShapeDtypeStruct((batch, MLA_HEADS, seq, LANES), BF16)
    const = lambda shape: pl.BlockSpec(shape, lambda b, i: (0,) * len(shape))
    return pl.pallas_call(
        _mla_prep_kernel,
        grid=(batch, seq // tm),
        in_specs=[
            pl.BlockSpec((None, tm, GROUP), lambda b, i: (b, i, G_MLA)),
            const((1, MLA_Q_RANK)), const((1, MLA_KV_RANK)),
            const((MLA_Q_RANK, MLA_HEADS * LANES)), const((MLA_KV_RANK, MLA_HEADS * LANES)),
            const((MLA_KV_RANK, MLA_HEADS * MLA_V)),
            pl.BlockSpec((3, tm, LANES), lambda b, i: (0, i, 0)),
        ],
        out_specs=[
            pl.BlockSpec((1, MLA_HEADS, tm, LANES), lambda b, i: (b, 0, i, 0)),
            pl.BlockSpec((1, MLA_HEADS, tm, LANES), lambda b, i: (b, 0, i, 0)),
            pl.BlockSpec((1, tm, MLA_HEADS * MLA_V), lambda b, i: (b, i, 0)),
        ],
        out_shape=[hs, hs, jax.ShapeDtypeStruct((batch, seq, MLA_HEADS * MLA_V), BF16)],
        compiler_params=pltpu.CompilerParams(
            dimension_semantics=("parallel", "parallel"), vmem_limit_bytes=VMEM_LIMIT),
        name="mla_prep",
    )(p3, qg, kvg, wq, wk, wv, tab)


_HEADS = range(HEAD_PAIR)
_ROWS = range(2)


def _fill_v_transposed(v_ref, vt_ref, t, head_dim):
    sub = lax.broadcasted_iota(jnp.int32, (BF16_ROWS, t), 0)
    ones_block = jnp.where(sub == 0, 1.0, 0.0)
    for j in range(vt_ref.shape[1]):
        vt = v_ref[0, j * t:(j + 1) * t, :].astype(F32).T
        for h in _HEADS:
            own = vt[h * head_dim:(h + 1) * head_dim]
            vt_ref[h, j] = jnp.concatenate([own, ones_block], axis=0).astype(BF16)


def _fill_causal_cap(cap_ref, t):
    key = lax.broadcasted_iota(jnp.int32, (t, t), 0)
    query = lax.broadcasted_iota(jnp.int32, (t, t), 1)
    cap_ref[...] = jnp.where(key <= query, jnp.inf, NEG_INF)


def _attend_rows(g, n_tiles, t, load_qt, load_k, vt_ref, write_out, scratch, head_dim, row_cap=None):
    causal_cap_ref, bufs, m_ref, acc_ref = scratch
    depth = bufs.shape[0]
    q_tile = [g, n_tiles - 1 - g]
    m_ref[...] = jnp.full_like(m_ref, -jnp.inf)
    acc_ref[...] = jnp.zeros_like(acc_ref)

    def locate(n):
        row = (n > q_tile[0]).astype(jnp.int32)
        return row, jnp.where(row == 0, q_tile[0] - n, n - (q_tile[0] + 1))

    def issue_scores(n, slot):
        row, kj = locate(n)
        ks, qts = load_k(kj), load_qt(row)
        for h in _HEADS:
            bufs[slot, h] = jnp.dot(ks[h], qts[h], preferred_element_type=F32)

    def softmax_step(n, slot, diagonal=False):
        row, kj = locate(n)
        chain = [HEAD_PAIR * row + h for h in _HEADS]
        s = [bufs[slot, h] for h in _HEADS]
        if diagonal:
            s = [jnp.minimum(s[h], causal_cap_ref[...]) for h in _HEADS]
        elif row_cap is not None:
            s = [jnp.minimum(s[h], row_cap(row, h, kj)) for h in _HEADS]
        m_old = [m_ref[chain[h]] for h in _HEADS]
        m_new = [jnp.maximum(m_old[h], jnp.max(s[h], axis=0, keepdims=True)) for h in _HEADS]
        alpha = [jnp.exp2(m_old[h] - m_new[h]) for h in _HEADS]
        p = [jnp.exp2(s[h] - m_new[h]).astype(BF16) for h in _HEADS]
        pv = [jnp.dot(vt_ref[h, kj], p[h], preferred_element_type=F32) for h in _HEADS]
        for h in _HEADS:
            m_ref[chain[h]] = m_new[h]
            acc_ref[chain[h]] = alpha[h] * acc_ref[chain[h]] + pv[h]

    steps = n_tiles + 1
    for n in range(min(depth, steps)):
        issue_scores(n, n)
    for n in range(steps):
        softmax_step(n, n % depth, diagonal=n in (0, n_tiles))
        if n + depth < steps:
            issue_scores(n + depth, n % depth)

    for row in _ROWS:
        outs = []
        for h in _HEADS:
            acc = acc_ref[HEAD_PAIR * row + h]
            outs.append(acc[:head_dim] * (1.0 / acc[head_dim:head_dim + 1]))
        ot = jnp.concatenate(outs, axis=0)
        write_out(q_tile[row], ot.T.astype(BF16))


def _attn_scratch(seq, t, head_dim):
    chains = HEAD_PAIR * len(_ROWS)
    v_rows = head_dim + BF16_ROWS
    return [pltpu.VMEM((HEAD_PAIR, seq // t, v_rows, t), BF16),
            pltpu.VMEM((t, t), F32),
            pltpu.VMEM((TILE_PAIRS, SCORE_DEPTH, HEAD_PAIR, t, t), F32),
            pltpu.VMEM((TILE_PAIRS, chains, 1, t), F32),
            pltpu.VMEM((TILE_PAIRS, chains, v_rows, t), F32)]


def _pair_scratch(scratch, pair):
    causal_cap_ref, bufs, m_ref, acc_ref = scratch
    return causal_cap_ref, bufs.at[pair], m_ref.at[pair], acc_ref.at[pair]


def _tile_rows(i, t):
    return pl.ds(pl.multiple_of(i * t, t), t)


def _mla_attn_kernel(q_ref, k_ref, v_ref, o_ref, qt_ref, vt_ref, *scratch, t, n_tiles):
    step = pl.program_id(2)

    @pl.when(step == 0)
    def _():
        _fill_v_transposed(v_ref, vt_ref, t, MLA_V)
        _fill_causal_cap(scratch[0], t)

    def load_k(kj):
        return [k_ref[0, h, _tile_rows(kj, t), :] for h in _HEADS]

    def write_out(qt, tile):
        o_ref[0, _tile_rows(qt, t), :] = tile

    pair_g = [TILE_PAIRS * step + pair for pair in range(TILE_PAIRS)]
    for pair, g in enumerate(pair_g):
        for row, qi in enumerate((g, n_tiles - 1 - g)):
            for h in _HEADS:
                qt_ref[pair, row, h] = q_ref[0, h, _tile_rows(qi, t), :].astype(F32).T.astype(BF16)
    for pair, g in enumerate(pair_g):
        load_qt = lambda row, pair=pair: [qt_ref[pair, row, h] for h in _HEADS]
        _attend_rows(g, n_tiles, t, load_qt, load_k, vt_ref, write_out, _pair_scratch(scratch, pair), MLA_V)


def _mla_attn(q, k, v, t):
    batch, heads, seq, _ = q.shape
    n_tiles = seq // t
    assert n_tiles % (2 * TILE_PAIRS) == 0
    return pl.pallas_call(
        functools.partial(_mla_attn_kernel, t=t, n_tiles=n_tiles),
        grid=(batch, heads // HEAD_PAIR, n_tiles // (2 * TILE_PAIRS)),
        in_specs=[
            pl.BlockSpec((1, HEAD_PAIR, seq, LANES), lambda b, h, g: (b, h, 0, 0)),
            pl.BlockSpec((1, HEAD_PAIR, seq, LANES), lambda b, h, g: (b, h, 0, 0)),
            pl.BlockSpec((1, seq, LANES), lambda b, h, g: (b, 0, h)),
        ],
        out_specs=pl.BlockSpec((1, seq, LANES), lambda b, h, g: (b, 0, h)),
        out_shape=jax.ShapeDtypeStruct((batch, seq, heads * MLA_V), BF16),
        scratch_shapes=[pltpu.VMEM((TILE_PAIRS, len(_ROWS), HEAD_PAIR, LANES, t), BF16)]
                       + _attn_scratch(seq, t, MLA_V),
        compiler_params=pltpu.CompilerParams(
            dimension_semantics=("parallel", "parallel", "arbitrary"), vmem_limit_bytes=VMEM_LIMIT),
        name="mla_attn",
    )(q, k, v)


def _moba_kernel(q_ref, k_ref, v_ref, o_ref, kmean_ref, qh_ref, cap_ref, vt_ref, *scratch, nb):
    t = MOBA_BLOCK
    step = pl.program_id(2)

    @pl.when(step == 0)
    def _():
        _fill_v_transposed(v_ref, vt_ref, t, MOBA_HEAD_DIM)
        _fill_causal_cap(scratch[0], t)
        kmean_ref[...] = jnp.zeros_like(kmean_ref)
        for j in range(nb):
            kb = k_ref[0, j * t:(j + 1) * t, :].astype(F32)
            kmean_ref[j:j + 1, :] = jnp.mean(kb, axis=0, keepdims=True)

    lane = lax.broadcasted_iota(jnp.int32, (t, LANES), 1)
    lane_k = lax.broadcasted_iota(jnp.int32, (LANES, LANES), 1)
    blk_id = lax.broadcasted_iota(jnp.int32, (nb, t), 0)
    nb_rows = -(-nb // SUBLANES) * SUBLANES
    pair_g = [TILE_PAIRS * step + pair for pair in range(TILE_PAIRS)]
    for pair, g in enumerate(pair_g):
        for row, qi in enumerate((g, nb - 1 - g)):
            q_pair = q_ref[0, _tile_rows(qi, t), :]
            for hh in _HEADS:
                in_head = (lane >= hh * MOBA_HEAD_DIM) & (lane < (hh + 1) * MOBA_HEAD_DIM)
                qt_gate = jnp.where(in_head, q_pair, jnp.zeros_like(q_pair)).astype(F32).T
                qh_ref[pair, row, hh] = (qt_gate * (MOBA_HEAD_DIM ** -0.5 * LOG2_E)).astype(BF16)
                in_head_k = (lane_k >= hh * MOBA_HEAD_DIM) & (lane_k < (hh + 1) * MOBA_HEAD_DIM)
                km = jnp.where(in_head_k, kmean_ref[...], 0.0)[:nb_rows]
                gate = jnp.dot(km, qt_gate, precision=lax.Precision.HIGHEST, preferred_element_type=F32)[:nb]
                left = jnp.where(blk_id < qi, gate, NEG_INF)
                keep = blk_id == qi
                for r in range(MOBA_TOPK):
                    is_max = left == jnp.max(left, axis=0, keepdims=True)
                    pick = blk_id == jnp.min(jnp.where(is_max, blk_id, nb), axis=0, keepdims=True)
                    keep = keep | (pick & (qi > r))
                    left = jnp.where(pick, -jnp.inf, left)
                cap = jnp.where(keep, jnp.inf, NEG_INF)
                for j in range(nb):
                    cap_ref[pair, row, hh, j] = cap[j:j + 1, :]

    def load_k(kj):
        return [k_ref[0, _tile_rows(kj, t), :]] * HEAD_PAIR

    def write_out(qt, tile):
        o_ref[0, _tile_rows(qt, t), :] = tile

    for pair, g in enumerate(pair_g):
        load_qt = lambda row, pair=pair: [qh_ref[pair, row, h] for h in _HEADS]
        row_cap = lambda row, h, kj, pair=pair: cap_ref[pair, row, h, kj]
        _attend_rows(g, nb, t, load_qt, load_k, vt_ref, write_out, _pair_scratch(scratch, pair),
                     MOBA_HEAD_DIM, row_cap=row_cap)


def _moba_attn(p3):
    batch, seq, _ = p3.shape
    t = MOBA_BLOCK
    nb = seq // t
    assert nb % (2 * TILE_PAIRS) == 0
    per_group = GROUP // LANES
    return pl.pallas_call(
        functools.partial(_moba_kernel, nb=nb),
        grid=(batch, MOBA_HEADS // HEAD_PAIR, nb // (2 * TILE_PAIRS)),
        in_specs=[
            pl.BlockSpec((1, seq, LANES), lambda b, h, g: (b, 0, G_MOBA_Q * per_group + h)),
            pl.BlockSpec((1, seq, LANES), lambda b, h, g: (b, 0, G_MOBA_K * per_group + h)),
            pl.BlockSpec((1, seq, LANES), lambda b, h, g: (b, 0, G_MOBA_V * per_group + h)),
        ],
        out_specs=pl.BlockSpec((1, seq, LANES), lambda b, h, g: (b, 0, h)),
        out_shape=jax.ShapeDtypeStruct((batch, seq, MOBA_HEADS * MOBA_HEAD_DIM), BF16),
        scratch_shapes=[pltpu.VMEM((LANES, LANES), F32),
                        pltpu.VMEM((TILE_PAIRS, len(_ROWS), HEAD_PAIR, LANES, t), BF16),
                        pltpu.VMEM((TILE_PAIRS, len(_ROWS), HEAD_PAIR, nb, 1, t), F32)]
                       + _attn_scratch(seq, t, MOBA_HEAD_DIM),
        compiler_params=pltpu.CompilerParams(
            dimension_semantics=("parallel", "parallel", "arbitrary"), vmem_limit_bytes=VMEM_LIMIT),
        name="moba_attn",
    )(p3, p3, p3)


def _retention_kernel(qk_ref, v_ref, g_ref, decay_ref, xi_ref, zeta_ref, gamma_ref, gnw_ref, gnb_ref,
                      o_ref, state_ref, *, chunks):
    c = RET_CHUNK

    @pl.when(pl.program_id(1) == 0)
    def _():
        state_ref[...] = jnp.zeros_like(state_ref)

    lane = lax.broadcasted_iota(jnp.int32, (c, LANES), 1)
    k_off = RET_HEADS * RET_QK_DIM
    for n in range(chunks):
        rows = slice(n * c, (n + 1) * c)
        for h in range(RET_HEADS):
            pair, half = divmod(h, HEAD_PAIR)
            in_head = (lane >= half * RET_QK_DIM) & (lane < (half + 1) * RET_QK_DIM)
            q = jnp.where(in_head, qk_ref[0, rows, pair * LANES:(pair + 1) * LANES].astype(F32), 0.0)
            k = jnp.where(in_head, qk_ref[0, rows, k_off + pair * LANES:k_off + (pair + 1) * LANES].astype(F32), 0.0)
            vs = slice(h * RET_V_DIM, (h + 1) * RET_V_DIM)
            v = v_ref[0, rows, vs]
            state = state_ref[h]
            scores = _nt_dot(q.astype(BF16), k.astype(BF16)) * decay_ref[h]
            inner = jnp.dot(scores.astype(BF16), v, preferred_element_type=F32)
            cross = jnp.dot((q * xi_ref[h]).astype(BF16), state.astype(BF16), preferred_element_type=F32)
            kv = lax.dot_general((k * zeta_ref[h]).astype(BF16), v, (((0,), (0,)), ((), ())),
                                 preferred_element_type=F32)
            state_ref[h] = gamma_ref[h] * state + kv
            o = inner + cross
            mu = jnp.mean(o, axis=-1, keepdims=True)
            d = o - mu
            var = jnp.mean(d * d, axis=-1, keepdims=True)
            o = d * lax.rsqrt(var + GN_EPS) * gnw_ref[:, vs] + gnb_ref[:, vs]
            g = g_ref[0, rows, vs].astype(F32)
            o_ref[0, rows, vs] = (g * jax.nn.sigmoid(g) * o).astype(BF16)


def _retention(p3, decay, xi, zeta, gamma, gnw, gnb, tr):
    batch, seq, _ = p3.shape
    width = RET_HEADS * RET_V_DIM
    const = lambda shape: pl.BlockSpec(shape, lambda b, i: (0,) * len(shape))
    return pl.pallas_call(
        functools.partial(_retention_kernel, chunks=tr // RET_CHUNK),
        grid=(batch, seq // tr),
        in_specs=[
            pl.BlockSpec((1, tr, GROUP), lambda b, i: (b, i, G_RET_QK)),
            pl.BlockSpec((1, tr, GROUP), lambda b, i: (b, i, G_RET_V)),
            pl.BlockSpec((1, tr, GROUP), lambda b, i: (b, i, G_RET_G)),
            const((RET_HEADS, RET_CHUNK, RET_CHUNK)),
            const((RET_HEADS, RET_CHUNK, LANES)), const((RET_HEADS, RET_CHUNK, LANES)),
            const((RET_HEADS, 1, LANES)),
            const((1, width)), const((1, width)),
        ],
        out_specs=pl.BlockSpec((1, tr, width), lambda b, i: (b, i, 0)),
        out_shape=jax.ShapeDtypeStruct((batch, seq, width), BF16),
        scratch_shapes=[pltpu.VMEM((RET_HEADS, LANES, RET_V_DIM), F32)],
        compiler_params=pltpu.CompilerParams(
            dimension_semantics=("parallel", "arbitrary"), vmem_limit_bytes=VMEM_LIMIT),
        name="retention",
    )(p3, p3, p3, decay, xi, zeta, gamma, gnw, gnb)


def _merge_mlp_kernel(x_ref, gates_ref, oa_ref, ob_ref, oc_ref, wa_ref, wb_ref, wc_ref, wo_ref,
                      g_ref, wu_ref, wd_ref, fg_ref, o_ref, h_ref, *, tf, final_norm):
    merged = None
    for idx, (o_in, w) in enumerate(((oa_ref, wa_ref), (ob_ref, wb_ref), (oc_ref, wc_ref))):
        gate = gates_ref[:, idx * D_MODEL:(idx + 1) * D_MODEL].astype(F32)
        term = jax.nn.sigmoid(gate) * jnp.dot(o_in[...], w[...], preferred_element_type=F32)
        merged = term if merged is None else merged + term
    x = x_ref[...] + jnp.dot(merged.astype(BF16), wo_ref[...], preferred_element_type=F32)

    h_ref[...] = _rms(x, g_ref[...]).astype(BF16)
    acc = x
    for f in range(D_FF // tf):
        cols = slice(f * tf, (f + 1) * tf)
        u = jnp.maximum(jnp.dot(h_ref[...], wu_ref[:, cols], preferred_element_type=F32), 0.0)
        acc = acc + jnp.dot((u * u).astype(BF16), wd_ref[cols, :], preferred_element_type=F32)
    o_ref[...] = _rms(acc, fg_ref[...]) if final_norm else acc


def _merge_mlp(x2, p2, oa, ob, oc, wa, wb, wc, wo, g, wu, wd, fg, tm, tf, final_norm):
    tokens = x2.shape[0]
    row = lambda w: pl.BlockSpec((tm, w), lambda i: (i, 0))
    vec = pl.BlockSpec((1, D_MODEL), lambda i: (0, 0))
    resident = lambda shape: pl.BlockSpec(shape, lambda i: (0, 0), pipeline_mode=pl.Buffered(1))
    return pl.pallas_call(
        functools.partial(_merge_mlp_kernel, tf=tf, final_norm=final_norm),
        grid=(tokens // tm,),
        in_specs=[row(D_MODEL), row(3 * D_MODEL), row(GROUP), row(GROUP), row(GROUP),
                  resident((GROUP, D_MODEL)), resident((GROUP, D_MODEL)), resident((GROUP, D_MODEL)),
                  resident((D_MODEL, D_MODEL)),
                  vec, resident((D_MODEL, D_FF)), resident((D_FF, D_MODEL)), vec],
        out_specs=row(D_MODEL),
        out_shape=jax.ShapeDtypeStruct((tokens, D_MODEL), F32),
        scratch_shapes=[pltpu.VMEM((tm, D_MODEL), BF16)],
        compiler_params=pltpu.CompilerParams(
            dimension_semantics=("parallel",), vmem_limit_bytes=VMEM_LIMIT),
        name="merge_mlp",
    )(x2, p2, oa, ob, oc, wa, wb, wc, wo, g, wu, wd, fg)


def _rope_cos_sin(seq, dim, theta):
    inv = 1.0 / (theta ** (jnp.arange(0, dim, 2, dtype=F32) / dim))
    ang = jnp.arange(seq, dtype=F32)[:, None] * inv[None, :]
    return jnp.cos(ang), jnp.sin(ang)


def _rope_tables(seq, dim, theta, period, offset):
    cos, sin = _rope_cos_sin(seq, dim, theta)
    half = dim // 2
    lo, hi = slice(offset, offset + half), slice(offset + half, offset + dim)
    c = jnp.ones((seq, period), F32).at[:, lo].set(cos).at[:, hi].set(cos)
    s_next = jnp.zeros((seq, period), F32).at[:, lo].set(-sin)
    s_prev = jnp.zeros((seq, period), F32).at[:, hi].set(sin)
    return jnp.tile(jnp.stack([c, s_next, s_prev]), (1, 1, LANES // period))


def _retention_tables():
    c = RET_CHUNK
    log_gamma = jnp.log(1.0 - 2.0 ** (-5.0 - jnp.arange(RET_HEADS, dtype=F32)))
    pos = jnp.arange(c, dtype=F32)
    diff = pos[:, None] - pos[None, :]
    decay = jnp.where(diff >= 0, jnp.exp(log_gamma[:, None, None] * diff), 0.0)
    xi = jnp.exp(log_gamma[:, None] * (pos + 1.0))
    zeta = jnp.exp(log_gamma[:, None] * (c - 1.0 - pos))
    gamma_c = jnp.exp(log_gamma * c)
    bcast = lambda t: jnp.broadcast_to(t[..., None], t.shape + (LANES,))
    return decay, bcast(xi), bcast(zeta), bcast(gamma_c[:, None])


def _pad_in_proj_weight(w):
    wt = jnp.swapaxes(w, 1, 2).astype(BF16)
    latent = MLA_Q_RANK + MLA_KV_RANK
    mixers = latent + MLA_ROPE
    gates = wt.shape[1] - 3 * D_MODEL
    zeros = lambda n: jnp.zeros((wt.shape[0], n, wt.shape[2]), BF16)
    return jnp.concatenate([wt[:, gates:], wt[:, :latent], zeros(MLA_NOPE), wt[:, latent:mixers],
                            zeros(LANES - MLA_NOPE - MLA_ROPE), wt[:, mixers:gates]], axis=1)


def _pad_mla_weights(w_q_up, w_kv_up):
    d_qk = MLA_NOPE + MLA_ROPE
    wq = w_q_up.reshape(MLA_Q_RANK, MLA_HEADS, d_qk)
    wq = jnp.pad(wq, ((0, 0), (0, 0), (0, LANES - d_qk))).reshape(MLA_Q_RANK, MLA_HEADS * LANES)
    wkv = w_kv_up.reshape(MLA_KV_RANK, MLA_HEADS, MLA_NOPE + MLA_V)
    wk = jnp.pad(wkv[:, :, :MLA_NOPE], ((0, 0), (0, 0), (0, LANES - MLA_NOPE)))
    wk = wk.reshape(MLA_KV_RANK, MLA_HEADS * LANES)
    wv = wkv[:, :, MLA_NOPE:].reshape(MLA_KV_RANK, MLA_HEADS * MLA_V)
    return wq.astype(BF16), wk.astype(BF16), wv.astype(BF16)


def _tile_sizes(seq):
    pick = lambda want: want if seq % want == 0 else seq
    return dict(in_proj=pick(512), prep=pick(512), attn=pick(256), ret=pick(512),
                mlp=pick(512), mlp_ff=512)


def kernel(x, attn_norm, w_in, mla_q_norm, mla_kv_norm, mla_w_q_up, mla_w_kv_up, ret_gn_w, ret_gn_b,
           w_branch_mla, w_branch_moba, w_branch_ret, w_out, mlp_norm, w_mlp_up, w_mlp_down, final_norm):
    batch, seq, d = x.shape
    depth = w_in.shape[0]
    tiles = _tile_sizes(seq)
    mla_tab = _rope_tables(seq, MLA_ROPE, ROPE_THETA, LANES, MLA_NOPE)
    moba_tab = _rope_tables(seq, MOBA_HEAD_DIM // PARTIAL_ROPE_DIV, ROPE_THETA, MOBA_HEAD_DIM, 0)
    ret_tab = _rope_tables(seq, RET_QK_DIM, RET_THETA, RET_QK_DIM, 0)
    decay, xi, zeta, gamma_c = _retention_tables()
    row = lambda v: v.reshape(1, -1)

    w_in_t = _pad_in_proj_weight(w_in)
    x2 = x.reshape(batch * seq, d)
    for l in range(depth):
        wq, wk, wv = _pad_mla_weights(mla_w_q_up[l], mla_w_kv_up[l])
        p2 = _in_proj(x2, row(attn_norm[l]), w_in_t[l], moba_tab, ret_tab, seq, tiles["in_proj"])
        p3 = p2.reshape(batch, seq, IN_COLS_PADDED)
        q_mla, k_mla, v_mla = _mla_prep(p3, row(mla_q_norm[l]), row(mla_kv_norm[l]), wq, wk, wv,
                                        mla_tab, tiles["prep"])
        o_mla = _mla_attn(q_mla, k_mla, v_mla, tiles["attn"])
        o_moba = _moba_attn(p3)
        o_ret = _retention(p3, decay, xi, zeta, gamma_c, row(ret_gn_w[l]), row(ret_gn_b[l]), tiles["ret"])
        flat = lambda o: o.reshape(batch * seq, GROUP)
        x2 = _merge_mlp(x2, p2, flat(o_mla), flat(o_moba), flat(o_ret),
                        w_branch_mla[l].astype(BF16), w_branch_moba[l].astype(BF16),
                        w_branch_ret[l].astype(BF16), w_out[l].astype(BF16),
                        row(mlp_norm[l]), w_mlp_up[l].astype(BF16), w_mlp_down[l].astype(BF16),
                        row(final_norm), tiles["mlp"], tiles["mlp_ff"], final_norm=(l == depth - 1))
    return x2.reshape(batch, seq, d)
```

```python
import functools

import jax
import jax.numpy as jnp
from jax import lax
from jax.experimental import pallas as pl
from jax.experimental.pallas import tpu as pltpu

F32 = jnp.float32
BF16 = jnp.bfloat16

D_MODEL = 1024
MLA_HEADS = 8
MLA_Q_RANK = 256
MLA_KV_RANK = 128
MLA_NOPE = 64
MLA_ROPE = 32
MLA_V = 64
MOBA_HEADS = 8
MOBA_HEAD_DIM = 64
MOBA_BLOCK = 256
MOBA_TOPK = 3
RET_HEADS = 4
RET_QK_DIM = 64
RET_V_DIM = 128
RET_CHUNK = 128
RET_THETA = 10000.0
ROPE_THETA = 500000.0
PARTIAL_ROPE_DIV = 4
D_FF = 4 * D_MODEL
NORM_EPS = 1e-6
GN_EPS = 1e-5
NEG_INF = -1e30
LOG2_E = 1.4426950408889634

LANES = 128
SUBLANES = 8
BF16_ROWS = 16
HEAD_PAIR = 2
SCORE_DEPTH = 3
TILE_PAIRS = 2
VMEM_LIMIT = 48 * 1024 * 1024

GROUP = 512
G_GATES = 0
G_MOBA_Q, G_MOBA_K, G_MOBA_V = 6, 7, 8
G_RET_QK, G_RET_V, G_RET_G = 9, 10, 11
G_MLA = 12
N_GROUPS = 13
IN_COLS_PADDED = G_MLA * GROUP


def _nt_dot(a, b):
    return lax.dot_general(a, b, (((1,), (1,)), ((), ())), preferred_element_type=F32)


def _rms(x, g):
    return x * lax.rsqrt(jnp.mean(x * x, axis=-1, keepdims=True) + NORM_EPS) * g


def _rope_lanes(y, tab_ref, half):
    return (y * tab_ref[0]
            + pltpu.roll(y, LANES - half, 1) * tab_ref[1]
            + pltpu.roll(y, half, 1) * tab_ref[2])


def _mla_heads(latent, tab, qg_ref, kvg_ref, wq_ref, wk_ref, wv_ref, q_ref, k_ref, v_ref):
    cq = latent[:, :MLA_Q_RANK]
    ckv = latent[:, MLA_Q_RANK:MLA_Q_RANK + MLA_KV_RANK]
    kr = latent[:, MLA_Q_RANK + MLA_KV_RANK:]
    cqn = _rms(cq, qg_ref[...]).astype(BF16)
    ckvn = _rms(ckv, kvg_ref[...]).astype(BF16)
    q = jnp.dot(cqn, wq_ref[...], preferred_element_type=F32)
    k = jnp.dot(ckvn, wk_ref[...], preferred_element_type=F32)
    v = jnp.dot(ckvn, wv_ref[...], preferred_element_type=F32)
    kpe = _rope_lanes(kr, tab, MLA_ROPE // 2)
    scale = (MLA_NOPE + MLA_ROPE) ** -0.5 * LOG2_E
    for h in range(MLA_HEADS):
        sl = slice(h * LANES, (h + 1) * LANES)
        q_ref[0, h] = (_rope_lanes(q[:, sl], tab, MLA_ROPE // 2) * scale).astype(BF16)
        k_ref[0, h] = (k[:, sl] + kpe).astype(BF16)
    v_ref[0] = v.astype(BF16)


def _in_proj_kernel(x_ref, g_ref, wt_ref, moba_tab, ret_tab, mla_tab, qg_ref, kvg_ref, wq_ref, wk_ref, wv_ref,
                    p_ref, q_ref, k_ref, v_ref, h_ref):
    h_ref[...] = _rms(x_ref[...], g_ref[...]).astype(BF16)
    lane_tiles = GROUP // LANES
    for j in [G_MLA] + [j for j in range(N_GROUPS) if j != G_MLA]:
        acc = _nt_dot(h_ref[...], wt_ref[j * GROUP:(j + 1) * GROUP, :])
        if j == G_MLA:
            _mla_heads(acc, mla_tab, qg_ref, kvg_ref, wq_ref, wk_ref, wv_ref, q_ref, k_ref, v_ref)
            continue
        for r in range(lane_tiles):
            y = acc[:, r * LANES:(r + 1) * LANES]
            if j in (G_MOBA_Q, G_MOBA_K):
                y = _rope_lanes(y, moba_tab, MOBA_HEAD_DIM // PARTIAL_ROPE_DIV // 2)
            elif j == G_RET_QK:
                y = _rope_lanes(y, ret_tab, RET_QK_DIM // 2)
                if r >= lane_tiles // 2:
                    y = y * (RET_QK_DIM ** -0.5)
            p_ref[:, j * GROUP + r * LANES:j * GROUP + (r + 1) * LANES] = y.astype(BF16)


def _in_proj(x2, g, w, moba_tab, ret_tab, mla_tab, qg, kvg, wq, wk, wv, seq, tm):
    tokens = x2.shape[0]
    s_tiles = seq // tm
    batch = tokens // seq
    tab = pl.BlockSpec((3, tm, LANES), lambda i: (0, i % s_tiles, 0))
    const = lambda shape: pl.BlockSpec(shape, lambda i: (0, 0))
    heads = pl.BlockSpec((1, MLA_HEADS, tm, LANES), lambda i: (i // s_tiles, 0, i % s_tiles, 0))
    head_shape = jax.ShapeDtypeStruct((batch, MLA_HEADS, seq, LANES), BF16)
    return pl.pallas_call(
        _in_proj_kernel,
        grid=(tokens // tm,),
        in_specs=[
            pl.BlockSpec((tm, D_MODEL), lambda i: (i, 0)),
            const((1, D_MODEL)),
            pl.BlockSpec((N_GROUPS * GROUP, D_MODEL), lambda i: (0, 0), pipeline_mode=pl.Buffered(1)),
            tab, tab, tab,
            const((1, MLA_Q_RANK)), const((1, MLA_KV_RANK)),
            const((MLA_Q_RANK, MLA_HEADS * LANES)), const((MLA_KV_RANK, MLA_HEADS * LANES)),
            const((MLA_KV_RANK, MLA_HEADS * MLA_V)),
        ],
        out_specs=[
            pl.BlockSpec((tm, IN_COLS_PADDED), lambda i: (i, 0)),
            heads, heads,
            pl.BlockSpec((1, tm, MLA_HEADS * MLA_V), lambda i: (i // s_tiles, i % s_tiles, 0)),
        ],
        out_shape=[jax.ShapeDtypeStruct((tokens, IN_COLS_PADDED), BF16), head_shape, head_shape,
                   jax.ShapeDtypeStruct((batch, seq, MLA_HEADS * MLA_V), BF16)],
        scratch_shapes=[pltpu.VMEM((tm, D_MODEL), BF16)],
        compiler_params=pltpu.CompilerParams(
            dimension_semantics=("parallel",), vmem_limit_bytes=VMEM_LIMIT),
        name="in_proj",
    )(x2, g, w, moba_tab, ret_tab, mla_tab, qg, kvg, wq, wk, wv)


_HEADS = range(HEAD_PAIR)
_ROWS = range(2)


def _fill_v_transposed(v_ref, vt_ref, t, head_dim):
    sub = lax.broadcasted_iota(jnp.int32, (BF16_ROWS, t), 0)
    ones_block = jnp.where(sub == 0, 1.0, 0.0)
    for j in range(vt_ref.shape[1]):
        vt = v_ref[0, j * t:(j + 1) * t, :].astype(F32).T
        for h in _HEADS:
            own = vt[h * head_dim:(h + 1) * head_dim]
            vt_ref[h, j] = jnp.concatenate([own, ones_block], axis=0).astype(BF16)


def _fill_causal_cap(cap_ref, t):
    key = lax.broadcasted_iota(jnp.int32, (t, t), 0)
    query = lax.broadcasted_iota(jnp.int32, (t, t), 1)
    cap_ref[...] = jnp.where(key <= query, jnp.inf, NEG_INF)


def _attend_rows(g, n_tiles, t, load_qt, load_k, vt_ref, write_out, scratch, head_dim, row_cap=None):
    causal_cap_ref, bufs, m_ref, acc_ref = scratch
    depth = bufs.shape[0]
    q_tile = [g, n_tiles - 1 - g]
    m_ref[...] = jnp.full_like(m_ref, -jnp.inf)
    acc_ref[...] = jnp.zeros_like(acc_ref)

    def locate(n):
        row = (n > q_tile[0]).astype(jnp.int32)
        return row, jnp.where(row == 0, q_tile[0] - n, n - (q_tile[0] + 1))

    def issue_scores(n, slot):
        row, kj = locate(n)
        ks, qts = load_k(kj), load_qt(row)
        for h in _HEADS:
            bufs[slot, h] = jnp.dot(ks[h], qts[h], preferred_element_type=F32)

    def softmax_step(n, slot, diagonal=False):
        row, kj = locate(n)
        chain = [HEAD_PAIR * row + h for h in _HEADS]
        s = [bufs[slot, h] for h in _HEADS]
        if diagonal:
            s = [jnp.minimum(s[h], causal_cap_ref[...]) for h in _HEADS]
        elif row_cap is not None:
            s = [jnp.minimum(s[h], row_cap(row, h, kj)) for h in _HEADS]
        m_old = [m_ref[chain[h]] for h in _HEADS]
        m_new = [jnp.maximum(m_old[h], jnp.max(s[h], axis=0, keepdims=True)) for h in _HEADS]
        alpha = [jnp.exp2(m_old[h] - m_new[h]) for h in _HEADS]
        p = [jnp.exp2(s[h] - m_new[h]).astype(BF16) for h in _HEADS]
        pv = [jnp.dot(vt_ref[h, kj], p[h], preferred_element_type=F32) for h in _HEADS]
        for h in _HEADS:
            m_ref[chain[h]] = m_new[h]
            acc_ref[chain[h]] = alpha[h] * acc_ref[chain[h]] + pv[h]

    steps = n_tiles + 1
    for n in range(min(depth, steps)):
        issue_scores(n, n)
    for n in range(steps):
        softmax_step(n, n % depth, diagonal=n in (0, n_tiles))
        if n + depth < steps:
            issue_scores(n + depth, n % depth)

    for row in _ROWS:
        outs = []
        for h in _HEADS:
            acc = acc_ref[HEAD_PAIR * row + h]
            outs.append(acc[:head_dim] * (1.0 / acc[head_dim:head_dim + 1]))
        ot = jnp.concatenate(outs, axis=0)
        write_out(q_tile[row], ot.T.astype(BF16))


def _attn_scratch(seq, t, head_dim):
    chains = HEAD_PAIR * len(_ROWS)
    v_rows = head_dim + BF16_ROWS
    return [pltpu.VMEM((HEAD_PAIR, seq // t, v_rows, t), BF16),
            pltpu.VMEM((t, t), F32),
            pltpu.VMEM((TILE_PAIRS, SCORE_DEPTH, HEAD_PAIR, t, t), F32),
            pltpu.VMEM((TILE_PAIRS, chains, 1, t), F32),
            pltpu.VMEM((TILE_PAIRS, chains, v_rows, t), F32)]


def _pair_scratch(scratch, pair):
    causal_cap_ref, bufs, m_ref, acc_ref = scratch
    return causal_cap_ref, bufs.at[pair], m_ref.at[pair], acc_ref.at[pair]


def _tile_rows(i, t):
    return pl.ds(pl.multiple_of(i * t, t), t)


def _mla_attn_kernel(q_ref, k_ref, v_ref, o_ref, qt_ref, vt_ref, *scratch, t, n_tiles):
    step = pl.program_id(2)

    @pl.when(step == 0)
    def _():
        _fill_v_transposed(v_ref, vt_ref, t, MLA_V)
        _fill_causal_cap(scratch[0], t)

    def load_k(kj):
        return [k_ref[0, h, _tile_rows(kj, t), :] for h in _HEADS]

    def write_out(qt, tile):
        o_ref[0, _tile_rows(qt, t), :] = tile

    pair_g = [TILE_PAIRS * step + pair for pair in range(TILE_PAIRS)]
    for pair, g in enumerate(pair_g):
        for row, qi in enumerate((g, n_tiles - 1 - g)):
            for h in _HEADS:
                qt_ref[pair, row, h] = q_ref[0, h, _tile_rows(qi, t), :].astype(F32).T.astype(BF16)
    for pair, g in enumerate(pair_g):
        load_qt = lambda row, pair=pair: [qt_ref[pair, row, h] for h in _HEADS]
        _attend_rows(g, n_tiles, t, load_qt, load_k, vt_ref, write_out, _pair_scratch(scratch, pair), MLA_V)


def _mla_attn(q, k, v, t):
    batch, heads, seq, _ = q.shape
    n_tiles = seq // t
    assert n_tiles % (2 * TILE_PAIRS) == 0
    return pl.pallas_call(
        functools.partial(_mla_attn_kernel, t=t, n_tiles=n_tiles),
        grid=(batch, heads // HEAD_PAIR, n_tiles // (2 * TILE_PAIRS)),
        in_specs=[
            pl.BlockSpec((1, HEAD_PAIR, seq, LANES), lambda b, h, g: (b, h, 0, 0)),
            pl.BlockSpec((1, HEAD_PAIR, seq, LANES), lambda b, h, g: (b, h, 0, 0)),
            pl.BlockSpec((1, seq, LANES), lambda b, h, g: (b, 0, h)),
        ],
        out_specs=pl.BlockSpec((1, seq, LANES), lambda b, h, g: (b, 0, h)),
        out_shape=jax.ShapeDtypeStruct((batch, seq, heads * MLA_V), BF16),
        scratch_shapes=[pltpu.VMEM((TILE_PAIRS, len(_ROWS), HEAD_PAIR, LANES, t), BF16)]
                       + _attn_scratch(seq, t, MLA_V),
        compiler_params=pltpu.CompilerParams(
            dimension_semantics=("parallel", "parallel", "arbitrary"), vmem_limit_bytes=VMEM_LIMIT),
        name="mla_attn",
    )(q, k, v)


def _moba_kernel(q_ref, k_ref, v_ref, o_ref, kmean_ref, qh_ref, cap_ref, vt_ref, *scratch, nb):
    t = MOBA_BLOCK
    step = pl.program_id(2)

    @pl.when(step == 0)
    def _():
        _fill_v_transposed(v_ref, vt_ref, t, MOBA_HEAD_DIM)
        _fill_causal_cap(scratch[0], t)
        kmean_ref[...] = jnp.zeros_like(kmean_ref)
        for j in range(nb):
            kb = k_ref[0, j * t:(j + 1) * t, :].astype(F32)
            kmean_ref[j:j + 1, :] = jnp.mean(kb, axis=0, keepdims=True)

    lane = lax.broadcasted_iota(jnp.int32, (t, LANES), 1)
    lane_k = lax.broadcasted_iota(jnp.int32, (LANES, LANES), 1)
    blk_id = lax.broadcasted_iota(jnp.int32, (nb, t), 0)
    nb_rows = -(-nb // SUBLANES) * SUBLANES
    pair_g = [TILE_PAIRS * step + pair for pair in range(TILE_PAIRS)]
    for pair, g in enumerate(pair_g):
        for row, qi in enumerate((g, nb - 1 - g)):
            q_pair = q_ref[0, _tile_rows(qi, t), :]
            for hh in _HEADS:
                in_head = (lane >= hh * MOBA_HEAD_DIM) & (lane < (hh + 1) * MOBA_HEAD_DIM)
                qt_gate = jnp.where(in_head, q_pair, jnp.zeros_like(q_pair)).astype(F32).T
                qh_ref[pair, row, hh] = (qt_gate * (MOBA_HEAD_DIM ** -0.5 * LOG2_E)).astype(BF16)
                in_head_k = (lane_k >= hh * MOBA_HEAD_DIM) & (lane_k < (hh + 1) * MOBA_HEAD_DIM)
                km = jnp.where(in_head_k, kmean_ref[...], 0.0)[:nb_rows]
                gate = jnp.dot(km, qt_gate, precision=lax.Precision.HIGHEST, preferred_element_type=F32)[:nb]
                left = jnp.where(blk_id < qi, gate, NEG_INF)
                keep = blk_id == qi
                for r in range(MOBA_TOPK):
                    is_max = left == jnp.max(left, axis=0, keepdims=True)
                    pick = blk_id == jnp.min(jnp.where(is_max, blk_id, nb), axis=0, keepdims=True)
                    keep = keep | (pick & (qi > r))
                    left = jnp.where(pick, -jnp.inf, left)
                cap = jnp.where(keep, jnp.inf, NEG_INF)
                for j in range(nb):
                    cap_ref[pair, row, hh, j] = cap[j:j + 1, :]

    def load_k(kj):
        return [k_ref[0, _tile_rows(kj, t), :]] * HEAD_PAIR

    def write_out(qt, tile):
        o_ref[0, _tile_rows(qt, t), :] = tile

    for pair, g in enumerate(pair_g):
        load_qt = lambda row, pair=pair: [qh_ref[pair, row, h] for h in _HEADS]
        row_cap = lambda row, h, kj, pair=pair: cap_ref[pair, row, h, kj]
        _attend_rows(g, nb, t, load_qt, load_k, vt_ref, write_out, _pair_scratch(scratch, pair),
                     MOBA_HEAD_DIM, row_cap=row_cap)


def _moba_attn(p3):
    batch, seq, _ = p3.shape
    t = MOBA_BLOCK
    nb = seq // t
    assert nb % (2 * TILE_PAIRS) == 0
    per_group = GROUP // LANES
    return pl.pallas_call(
        functools.partial(_moba_kernel, nb=nb),
        grid=(batch, MOBA_HEADS // HEAD_PAIR, nb // (2 * TILE_PAIRS)),
        in_specs=[
            pl.BlockSpec((1, seq, LANES), lambda b, h, g: (b, 0, G_MOBA_Q * per_group + h)),
            pl.BlockSpec((1, seq, LANES), lambda b, h, g: (b, 0, G_MOBA_K * per_group + h)),
            pl.BlockSpec((1, seq, LANES), lambda b, h, g: (b, 0, G_MOBA_V * per_group + h)),
        ],
        out_specs=pl.BlockSpec((1, seq, LANES), lambda b, h, g: (b, 0, h)),
        out_shape=jax.ShapeDtypeStruct((batch, seq, MOBA_HEADS * MOBA_HEAD_DIM), BF16),
        scratch_shapes=[pltpu.VMEM((LANES, LANES), F32),
                        pltpu.VMEM((TILE_PAIRS, len(_ROWS), HEAD_PAIR, LANES, t), BF16),
                        pltpu.VMEM((TILE_PAIRS, len(_ROWS), HEAD_PAIR, nb, 1, t), F32)]
                       + _attn_scratch(seq, t, MOBA_HEAD_DIM),
        compiler_params=pltpu.CompilerParams(
            dimension_semantics=("parallel", "parallel", "arbitrary"), vmem_limit_bytes=VMEM_LIMIT),
        name="moba_attn",
    )(p3, p3, p3)


def _retention_kernel(qk_ref, v_ref, g_ref, decay_ref, xi_ref, zeta_ref, gamma_ref, gnw_ref, gnb_ref,
                      o_ref, state_ref, *, chunks):
    c = RET_CHUNK

    @pl.when(pl.program_id(1) == 0)
    def _():
        state_ref[...] = jnp.zeros_like(state_ref)

    lane = lax.broadcasted_iota(jnp.int32, (c, LANES), 1)
    k_off = RET_HEADS * RET_QK_DIM
    for n in range(chunks):
        rows = slice(n * c, (n + 1) * c)
        for h in range(RET_HEADS):
            pair, half = divmod(h, HEAD_PAIR)
            in_head = (lane >= half * RET_QK_DIM) & (lane < (half + 1) * RET_QK_DIM)
            q = jnp.where(in_head, qk_ref[0, rows, pair * LANES:(pair + 1) * LANES].astype(F32), 0.0)
            k = jnp.where(in_head, qk_ref[0, rows, k_off + pair * LANES:k_off + (pair + 1) * LANES].astype(F32), 0.0)
            vs = slice(h * RET_V_DIM, (h + 1) * RET_V_DIM)
            v = v_ref[0, rows, vs]
            state = state_ref[h]
            scores = _nt_dot(q.astype(BF16), k.astype(BF16)) * decay_ref[h]
            inner = jnp.dot(scores.astype(BF16), v, preferred_element_type=F32)
            cross = jnp.dot((q * xi_ref[h]).astype(BF16), state.astype(BF16), preferred_element_type=F32)
            kv = lax.dot_general((k * zeta_ref[h]).astype(BF16), v, (((0,), (0,)), ((), ())),
                                 preferred_element_type=F32)
            state_ref[h] = gamma_ref[h] * state + kv
            o = inner + cross
            mu = jnp.mean(o, axis=-1, keepdims=True)
            d = o - mu
            var = jnp.mean(d * d, axis=-1, keepdims=True)
            o = d * lax.rsqrt(var + GN_EPS) * gnw_ref[:, vs] + gnb_ref[:, vs]
            g = g_ref[0, rows, vs].astype(F32)
            o_ref[0, rows, vs] = (g * jax.nn.sigmoid(g) * o).astype(BF16)


def _retention(p3, decay, xi, zeta, gamma, gnw, gnb, tr):
    batch, seq, _ = p3.shape
    width = RET_HEADS * RET_V_DIM
    const = lambda shape: pl.BlockSpec(shape, lambda b, i: (0,) * len(shape))
    return pl.pallas_call(
        functools.partial(_retention_kernel, chunks=tr // RET_CHUNK),
        grid=(batch, seq // tr),
        in_specs=[
            pl.BlockSpec((1, tr, GROUP), lambda b, i: (b, i, G_RET_QK)),
            pl.BlockSpec((1, tr, GROUP), lambda b, i: (b, i, G_RET_V)),
            pl.BlockSpec((1, tr, GROUP), lambda b, i: (b, i, G_RET_G)),
            const((RET_HEADS, RET_CHUNK, RET_CHUNK)),
            const((RET_HEADS, RET_CHUNK, LANES)), const((RET_HEADS, RET_CHUNK, LANES)),
            const((RET_HEADS, 1, LANES)),
            const((1, width)), const((1, width)),
        ],
        out_specs=pl.BlockSpec((1, tr, width), lambda b, i: (b, i, 0)),
        out_shape=jax.ShapeDtypeStruct((batch, seq, width), BF16),
        scratch_shapes=[pltpu.VMEM((RET_HEADS, LANES, RET_V_DIM), F32)],
        compiler_params=pltpu.CompilerParams(
            dimension_semantics=("parallel", "arbitrary"), vmem_limit_bytes=VMEM_LIMIT),
        name="retention",
    )(p3, p3, p3, decay, xi, zeta, gamma, gnw, gnb)


def _merge_mlp_kernel(x_ref, gates_ref, oa_ref, ob_ref, oc_ref, wa_ref, wb_ref, wc_ref, wo_ref,
                      g_ref, wu_ref, wd_ref, fg_ref, o_ref, h_ref, *, tf, final_norm):
    merged = None
    for idx, (o_in, w) in enumerate(((oa_ref, wa_ref), (ob_ref, wb_ref), (oc_ref, wc_ref))):
        gate = gates_ref[:, idx * D_MODEL:(idx + 1) * D_MODEL].astype(F32)
        term = jax.nn.sigmoid(gate) * jnp.dot(o_in[...], w[...], preferred_element_type=F32)
        merged = term if merged is None else merged + term
    x = x_ref[...] + jnp.dot(merged.astype(BF16), wo_ref[...], preferred_element_type=F32)

    h_ref[...] = _rms(x, g_ref[...]).astype(BF16)
    acc = x
    for f in range(D_FF // tf):
        cols = slice(f * tf, (f + 1) * tf)
        u = jnp.maximum(jnp.dot(h_ref[...], wu_ref[:, cols], preferred_element_type=F32), 0.0)
        acc = acc + jnp.dot((u * u).astype(BF16), wd_ref[cols, :], preferred_element_type=F32)
    o_ref[...] = _rms(acc, fg_ref[...]) if final_norm else acc


def _merge_mlp(x2, p2, oa, ob, oc, wa, wb, wc, wo, g, wu, wd, fg, tm, tf, final_norm):
    tokens = x2.shape[0]
    row = lambda w: pl.BlockSpec((tm, w), lambda i: (i, 0))
    vec = pl.BlockSpec((1, D_MODEL), lambda i: (0, 0))
    resident = lambda shape: pl.BlockSpec(shape, lambda i: (0, 0), pipeline_mode=pl.Buffered(1))
    return pl.pallas_call(
        functools.partial(_merge_mlp_kernel, tf=tf, final_norm=final_norm),
        grid=(tokens // tm,),
        in_specs=[row(D_MODEL), row(3 * D_MODEL), row(GROUP), row(GROUP), row(GROUP),
                  resident((GROUP, D_MODEL)), resident((GROUP, D_MODEL)), resident((GROUP, D_MODEL)),
                  resident((D_MODEL, D_MODEL)),
                  vec, resident((D_MODEL, D_FF)), resident((D_FF, D_MODEL)), vec],
        out_specs=row(D_MODEL),
        out_shape=jax.ShapeDtypeStruct((tokens, D_MODEL), F32),
        scratch_shapes=[pltpu.VMEM((tm, D_MODEL), BF16)],
        compiler_params=pltpu.CompilerParams(
            dimension_semantics=("parallel",), vmem_limit_bytes=VMEM_LIMIT),
        name="merge_mlp",
    )(x2, p2, oa, ob, oc, wa, wb, wc, wo, g, wu, wd, fg)


def _rope_cos_sin(seq, dim, theta):
    inv = 1.0 / (theta ** (jnp.arange(0, dim, 2, dtype=F32) / dim))
    ang = jnp.arange(seq, dtype=F32)[:, None] * inv[None, :]
    return jnp.cos(ang), jnp.sin(ang)


def _rope_tables(seq, dim, theta, period, offset):
    cos, sin = _rope_cos_sin(seq, dim, theta)
    half = dim // 2
    const = lambda value, n: [jnp.full((seq, n), value, F32)] if n else []
    before, after = offset, period - offset - dim

    def lanes(first, second, fill):
        parts = const(fill, before) + [first, second] + const(fill, after)
        return jnp.concatenate(parts * (LANES // period), axis=1)

    zero = jnp.zeros_like(sin)
    return jnp.stack([lanes(cos, cos, 1.0), lanes(-sin, zero, 0.0), lanes(zero, sin, 0.0)])


def _retention_tables():
    c = RET_CHUNK
    log_gamma = jnp.log(1.0 - 2.0 ** (-5.0 - jnp.arange(RET_HEADS, dtype=F32)))
    pos = jnp.arange(c, dtype=F32)
    diff = pos[:, None] - pos[None, :]
    decay = jnp.where(diff >= 0, jnp.exp(log_gamma[:, None, None] * diff), 0.0)
    xi = jnp.exp(log_gamma[:, None] * (pos + 1.0))
    zeta = jnp.exp(log_gamma[:, None] * (c - 1.0 - pos))
    gamma_c = jnp.exp(log_gamma * c)
    bcast = lambda t: jnp.broadcast_to(t[..., None], t.shape + (LANES,))
    return decay, bcast(xi), bcast(zeta), bcast(gamma_c[:, None])


def _pad_in_proj_weight(w):
    wt = jnp.swapaxes(w, 1, 2).astype(BF16)
    latent = MLA_Q_RANK + MLA_KV_RANK
    mixers = latent + MLA_ROPE
    gates = wt.shape[1] - 3 * D_MODEL
    zeros = lambda n: jnp.zeros((wt.shape[0], n, wt.shape[2]), BF16)
    return jnp.concatenate([wt[:, gates:], wt[:, mixers:gates], wt[:, :latent], zeros(MLA_NOPE),
                            wt[:, latent:mixers], zeros(LANES - MLA_NOPE - MLA_ROPE)], axis=1)


def _pad_mla_weights(w_q_up, w_kv_up):
    d_qk = MLA_NOPE + MLA_ROPE
    wq = w_q_up.reshape(MLA_Q_RANK, MLA_HEADS, d_qk)
    wq = jnp.pad(wq, ((0, 0), (0, 0), (0, LANES - d_qk))).reshape(MLA_Q_RANK, MLA_HEADS * LANES)
    wkv = w_kv_up.reshape(MLA_KV_RANK, MLA_HEADS, MLA_NOPE + MLA_V)
    wk = jnp.pad(wkv[:, :, :MLA_NOPE], ((0, 0), (0, 0), (0, LANES - MLA_NOPE)))
    wk = wk.reshape(MLA_KV_RANK, MLA_HEADS * LANES)
    wv = wkv[:, :, MLA_NOPE:].reshape(MLA_KV_RANK, MLA_HEADS * MLA_V)
    return wq.astype(BF16), wk.astype(BF16), wv.astype(BF16)


def _tile_sizes(seq):
    pick = lambda want: want if seq % want == 0 else seq
    return dict(in_proj=pick(512), attn=pick(256), ret=pick(512),
                mlp=pick(512), mlp_ff=512)


def kernel(x, attn_norm, w_in, mla_q_norm, mla_kv_norm, mla_w_q_up, mla_w_kv_up, ret_gn_w, ret_gn_b,
           w_branch_mla, w_branch_moba, w_branch_ret, w_out, mlp_norm, w_mlp_up, w_mlp_down, final_norm):
    batch, seq, d = x.shape
    depth = w_in.shape[0]
    tiles = _tile_sizes(seq)
    mla_tab = _rope_tables(seq, MLA_ROPE, ROPE_THETA, LANES, MLA_NOPE)
    moba_tab = _rope_tables(seq, MOBA_HEAD_DIM // PARTIAL_ROPE_DIV, ROPE_THETA, MOBA_HEAD_DIM, 0)
    ret_tab = _rope_tables(seq, RET_QK_DIM, RET_THETA, RET_QK_DIM, 0)
    decay, xi, zeta, gamma_c = _retention_tables()
    row = lambda v: v.reshape(1, -1)

    w_in_t = _pad_in_proj_weight(w_in)
    x2 = x.reshape(batch * seq, d)
    for l in range(depth):
        wq, wk, wv = _pad_mla_weights(mla_w_q_up[l], mla_w_kv_up[l])
        p2, q_mla, k_mla, v_mla = _in_proj(x2, row(attn_norm[l]), w_in_t[l], moba_tab, ret_tab, mla_tab,
                                           row(mla_q_norm[l]), row(mla_kv_norm[l]), wq, wk, wv,
                                           seq, tiles["in_proj"])
        p3 = p2.reshape(batch, seq, IN_COLS_PADDED)
        o_mla = _mla_attn(q_mla, k_mla, v_mla, tiles["attn"])
        o_moba = _moba_attn(p3)
        o_ret = _retention(p3, decay, xi, zeta, gamma_c, row(ret_gn_w[l]), row(ret_gn_b[l]), tiles["ret"])
        flat = lambda o: o.reshape(batch * seq, GROUP)
        x2 = _merge_mlp(x2, p2, flat(o_mla), flat(o_moba), flat(o_ret),
                        w_branch_mla[l].astype(BF16), w_branch_moba[l].astype(BF16),
                        w_branch_ret[l].astype(BF16), w_out[l].astype(BF16),
                        row(mlp_norm[l]), w_mlp_up[l].astype(BF16), w_mlp_down[l].astype(BF16),
                        row(final_norm), tiles["mlp"], tiles["mlp_ff"], final_norm=(l == depth - 1))
    return x2.reshape(batch, seq, d)
```

```python
import functools
import math

import jax
import jax.numpy as jnp
from jax import lax
from jax.experimental import pallas as pl
from jax.experimental.pallas import tpu as pltpu

F32 = jnp.float32
BF16 = jnp.bfloat16

D_MODEL = 1024
MLA_HEADS = 8
MLA_Q_RANK = 256
MLA_KV_RANK = 128
MLA_NOPE = 64
MLA_ROPE = 32
MLA_V = 64
MOBA_HEADS = 8
MOBA_HEAD_DIM = 64
MOBA_BLOCK = 256
MOBA_TOPK = 3
RET_HEADS = 4
RET_QK_DIM = 64
RET_V_DIM = 128
RET_CHUNK = 128
RET_THETA = 10000.0
ROPE_THETA = 500000.0
PARTIAL_ROPE_DIV = 4
D_FF = 4 * D_MODEL
NORM_EPS = 1e-6
GN_EPS = 1e-5
NEG_INF = -1e30
LOG2_E = 1.4426950408889634

LANES = 128
SUBLANES = 8
BF16_ROWS = 16
HEAD_PAIR = 2
SCORE_DEPTH = 3
MLA_TILE_PAIRS = 4
MOBA_TILE_PAIRS = 2
VMEM_LIMIT = 48 * 1024 * 1024

GROUP = 512
G_GATES = 0
G_MOBA_Q, G_MOBA_K, G_MOBA_V = 6, 7, 8
G_RET_QK, G_RET_V, G_RET_G = 9, 10, 11
N_GROUPS = 12
IN_COLS_PADDED = N_GROUPS * GROUP
W_KR = MLA_Q_RANK + MLA_KV_RANK
W_MIXERS = W_KR + MLA_ROPE
W_GATES = W_MIXERS + (G_RET_G + 1 - G_MOBA_Q) * GROUP
IN_COLS = W_GATES + 3 * D_MODEL


def _nt_dot(a, b):
    return lax.dot_general(a, b, (((1,), (1,)), ((), ())), preferred_element_type=F32)


def _rms(x, g):
    return x * lax.rsqrt(jnp.mean(x * x, axis=-1, keepdims=True) + NORM_EPS) * g


def _rope_lanes(y, tab_ref, half):
    return (y * tab_ref[0]
            + pltpu.roll(y, LANES - half, 1) * tab_ref[1]
            + pltpu.roll(y, half, 1) * tab_ref[2])


def _mla_heads(latent, kr, tab, qg_ref, kvg_ref, wq_ref, wk_ref, wv_ref, q_ref, k_ref, v_ref):
    cq = latent[:, :MLA_Q_RANK]
    ckv = latent[:, MLA_Q_RANK:]
    cqn = _rms(cq, qg_ref[...]).astype(BF16)
    ckvn = _rms(ckv, kvg_ref[...]).astype(BF16)
    q = jnp.dot(cqn, wq_ref[...], preferred_element_type=F32)
    k = jnp.dot(ckvn, wk_ref[...], preferred_element_type=F32)
    v = jnp.dot(ckvn, wv_ref[...], preferred_element_type=F32)
    kpe = _rope_lanes(kr, tab, MLA_ROPE // 2)
    scale = (MLA_NOPE + MLA_ROPE) ** -0.5 * LOG2_E
    for h in range(MLA_HEADS):
        sl = slice(h * LANES, (h + 1) * LANES)
        q_ref[0, h] = (_rope_lanes(q[:, sl], tab, MLA_ROPE // 2) * scale).astype(BF16)
        k_ref[0, h] = (k[:, sl] + kpe).astype(BF16)
    v_ref[0] = v.astype(BF16)


def _in_proj_kernel(x_ref, g_ref, wt_ref, moba_tab, ret_tab, mla_tab, qg_ref, kvg_ref, wq_ref, wk_ref, wv_ref,
                    p_ref, q_ref, k_ref, v_ref, h_ref, kr_ref):
    @pl.when(pl.program_id(0) == 0)
    def _():
        kr_ref[...] = jnp.zeros_like(kr_ref)
        kr_ref[MLA_NOPE:MLA_NOPE + MLA_ROPE, :] = wt_ref[W_KR:W_MIXERS, :]

    h_ref[...] = _rms(x_ref[...], g_ref[...]).astype(BF16)
    _mla_heads(_nt_dot(h_ref[...], wt_ref[:W_KR, :]), _nt_dot(h_ref[...], kr_ref[...]),
               mla_tab, qg_ref, kvg_ref, wq_ref, wk_ref, wv_ref, q_ref, k_ref, v_ref)
    lane_tiles = GROUP // LANES
    for j in range(N_GROUPS):
        first = W_GATES + j * GROUP if j < G_MOBA_Q else W_MIXERS + (j - G_MOBA_Q) * GROUP
        acc = _nt_dot(h_ref[...], wt_ref[first:first + GROUP, :])
        for r in range(lane_tiles):
            y = acc[:, r * LANES:(r + 1) * LANES]
            if j in (G_MOBA_Q, G_MOBA_K):
                y = _rope_lanes(y, moba_tab, MOBA_HEAD_DIM // PARTIAL_ROPE_DIV // 2)
            elif j == G_RET_QK:
                y = _rope_lanes(y, ret_tab, RET_QK_DIM // 2)
                if r >= lane_tiles // 2:
                    y = y * (RET_QK_DIM ** -0.5)
            p_ref[:, j * GROUP + r * LANES:j * GROUP + (r + 1) * LANES] = y.astype(BF16)


def _in_proj(x2, g, w, moba_tab, ret_tab, mla_tab, qg, kvg, wq, wk, wv, seq, tm):
    tokens = x2.shape[0]
    s_tiles = seq // tm
    batch = tokens // seq
    tab = pl.BlockSpec((3, tm, LANES), lambda i: (0, i % s_tiles, 0))
    const = lambda shape: pl.BlockSpec(shape, lambda i: (0, 0))
    heads = pl.BlockSpec((1, MLA_HEADS, tm, LANES), lambda i: (i // s_tiles, 0, i % s_tiles, 0))
    head_shape = jax.ShapeDtypeStruct((batch, MLA_HEADS, seq, LANES), BF16)
    return pl.pallas_call(
        _in_proj_kernel,
        grid=(tokens // tm,),
        in_specs=[
            pl.BlockSpec((tm, D_MODEL), lambda i: (i, 0)),
            const((1, D_MODEL)),
            pl.BlockSpec((IN_COLS, D_MODEL), lambda i: (0, 0), pipeline_mode=pl.Buffered(1)),
            tab, tab, tab,
            const((1, MLA_Q_RANK)), const((1, MLA_KV_RANK)),
            const((MLA_Q_RANK, MLA_HEADS * LANES)), const((MLA_KV_RANK, MLA_HEADS * LANES)),
            const((MLA_KV_RANK, MLA_HEADS * MLA_V)),
        ],
        out_specs=[
            pl.BlockSpec((tm, IN_COLS_PADDED), lambda i: (i, 0)),
            heads, heads,
            pl.BlockSpec((1, tm, MLA_HEADS * MLA_V), lambda i: (i // s_tiles, i % s_tiles, 0)),
        ],
        out_shape=[jax.ShapeDtypeStruct((tokens, IN_COLS_PADDED), BF16), head_shape, head_shape,
                   jax.ShapeDtypeStruct((batch, seq, MLA_HEADS * MLA_V), BF16)],
        scratch_shapes=[pltpu.VMEM((tm, D_MODEL), BF16), pltpu.VMEM((LANES, D_MODEL), BF16)],
        compiler_params=pltpu.CompilerParams(
            dimension_semantics=("arbitrary",), vmem_limit_bytes=VMEM_LIMIT),
        name="in_proj",
    )(x2, g, w, moba_tab, ret_tab, mla_tab, qg, kvg, wq, wk, wv)


_HEADS = range(HEAD_PAIR)
_ROWS = range(2)


def _fill_v_transposed(v_ref, vt_ref, t, head_dim):
    sub = lax.broadcasted_iota(jnp.int32, (BF16_ROWS, t), 0)
    ones_block = jnp.where(sub == 0, 1.0, 0.0)
    for j in range(vt_ref.shape[1]):
        vt = v_ref[0, j * t:(j + 1) * t, :].astype(F32).T
        for h in _HEADS:
            own = vt[h * head_dim:(h + 1) * head_dim]
            vt_ref[h, j] = jnp.concatenate([own, ones_block], axis=0).astype(BF16)


def _fill_causal_cap(cap_ref, t):
    key = lax.broadcasted_iota(jnp.int32, (t, t), 0)
    query = lax.broadcasted_iota(jnp.int32, (t, t), 1)
    cap_ref[...] = jnp.where(key <= query, jnp.inf, NEG_INF)


def _attend_rows(g, n_tiles, t, load_qt, load_k, vt_ref, write_out, scratch, head_dim, row_cap=None):
    causal_cap_ref, bufs, m_ref, acc_ref = scratch
    depth = bufs.shape[0]
    q_tile = [g, n_tiles - 1 - g]
    m_ref[...] = jnp.full_like(m_ref, -jnp.inf)
    acc_ref[...] = jnp.zeros_like(acc_ref)

    def locate(n):
        row = (n > q_tile[0]).astype(jnp.int32)
        return row, jnp.where(row == 0, q_tile[0] - n, n - (q_tile[0] + 1))

    def issue_scores(n, slot):
        row, kj = locate(n)
        ks, qts = load_k(kj), load_qt(row)
        for h in _HEADS:
            bufs[slot, h] = jnp.dot(ks[h], qts[h], preferred_element_type=F32)

    def softmax_step(n, slot, diagonal=False):
        row, kj = locate(n)
        chain = [HEAD_PAIR * row + h for h in _HEADS]
        s = [bufs[slot, h] for h in _HEADS]
        if diagonal:
            s = [jnp.minimum(s[h], causal_cap_ref[...]) for h in _HEADS]
        elif row_cap is not None:
            s = [jnp.minimum(s[h], row_cap(row, h, kj)) for h in _HEADS]
        m_old = [m_ref[chain[h]] for h in _HEADS]
        m_new = [jnp.maximum(m_old[h], jnp.max(s[h], axis=0, keepdims=True)) for h in _HEADS]
        alpha = [jnp.exp2(m_old[h] - m_new[h]) for h in _HEADS]
        p = [jnp.exp2(s[h] - m_new[h]).astype(BF16) for h in _HEADS]
        pv = [jnp.dot(vt_ref[h, kj], p[h], preferred_element_type=F32) for h in _HEADS]
        for h in _HEADS:
            m_ref[chain[h]] = m_new[h]
            acc_ref[chain[h]] = alpha[h] * acc_ref[chain[h]] + pv[h]

    steps = n_tiles + 1
    for n in range(min(depth, steps)):
        issue_scores(n, n)
    for n in range(steps):
        softmax_step(n, n % depth, diagonal=n in (0, n_tiles))
        if n + depth < steps:
            issue_scores(n + depth, n % depth)

    for row in _ROWS:
        outs = []
        for h in _HEADS:
            acc = acc_ref[HEAD_PAIR * row + h]
            outs.append(acc[:head_dim] * (1.0 / acc[head_dim:head_dim + 1]))
        ot = jnp.concatenate(outs, axis=0)
        write_out(q_tile[row], ot.T.astype(BF16))


def _attn_scratch(seq, t, head_dim, pairs):
    chains = HEAD_PAIR * len(_ROWS)
    v_rows = head_dim + BF16_ROWS
    return [pltpu.VMEM((HEAD_PAIR, seq // t, v_rows, t), BF16),
            pltpu.VMEM((t, t), F32),
            pltpu.VMEM((pairs, SCORE_DEPTH, HEAD_PAIR, t, t), F32),
            pltpu.VMEM((pairs, chains, 1, t), F32),
            pltpu.VMEM((pairs, chains, v_rows, t), F32)]


def _pairs_per_step(n_tiles, wanted):
    assert n_tiles % 2 == 0
    return math.gcd(wanted, n_tiles // 2)


def _pair_scratch(scratch, pair):
    causal_cap_ref, bufs, m_ref, acc_ref = scratch
    return causal_cap_ref, bufs.at[pair], m_ref.at[pair], acc_ref.at[pair]


def _tile_rows(i, t):
    return pl.ds(pl.multiple_of(i * t, t), t)


def _mla_attn_kernel(q_ref, k_ref, v_ref, o_ref, qt_ref, vt_ref, *scratch, t, n_tiles):
    step = pl.program_id(2)

    @pl.when(step == 0)
    def _():
        _fill_v_transposed(v_ref, vt_ref, t, MLA_V)
        _fill_causal_cap(scratch[0], t)

    def load_k(kj):
        return [k_ref[0, h, _tile_rows(kj, t), :] for h in _HEADS]

    def write_out(qt, tile):
        o_ref[0, _tile_rows(qt, t), :] = tile

    pairs = qt_ref.shape[0]
    pair_g = [pairs * step + pair for pair in range(pairs)]
    for pair, g in enumerate(pair_g):
        for row, qi in enumerate((g, n_tiles - 1 - g)):
            for h in _HEADS:
                qt_ref[pair, row, h] = q_ref[0, h, _tile_rows(qi, t), :].astype(F32).T.astype(BF16)
    for pair, g in enumerate(pair_g):
        load_qt = lambda row, pair=pair: [qt_ref[pair, row, h] for h in _HEADS]
        _attend_rows(g, n_tiles, t, load_qt, load_k, vt_ref, write_out, _pair_scratch(scratch, pair), MLA_V)


def _mla_attn(q, k, v, t):
    batch, heads, seq, _ = q.shape
    n_tiles = seq // t
    pairs = _pairs_per_step(n_tiles, MLA_TILE_PAIRS)
    return pl.pallas_call(
        functools.partial(_mla_attn_kernel, t=t, n_tiles=n_tiles),
        grid=(batch, heads // HEAD_PAIR, n_tiles // (2 * pairs)),
        in_specs=[
            pl.BlockSpec((1, HEAD_PAIR, seq, LANES), lambda b, h, g: (b, h, 0, 0)),
            pl.BlockSpec((1, HEAD_PAIR, seq, LANES), lambda b, h, g: (b, h, 0, 0)),
            pl.BlockSpec((1, seq, LANES), lambda b, h, g: (b, 0, h)),
        ],
        out_specs=pl.BlockSpec((1, seq, LANES), lambda b, h, g: (b, 0, h)),
        out_shape=jax.ShapeDtypeStruct((batch, seq, heads * MLA_V), BF16),
        scratch_shapes=[pltpu.VMEM((pairs, len(_ROWS), HEAD_PAIR, LANES, t), BF16)]
                       + _attn_scratch(seq, t, MLA_V, pairs),
        compiler_params=pltpu.CompilerParams(
            dimension_semantics=("parallel", "parallel", "arbitrary"), vmem_limit_bytes=VMEM_LIMIT),
        name="mla_attn",
    )(q, k, v)


def _moba_kernel(q_ref, k_ref, v_ref, o_ref, kmean_ref, qh_ref, cap_ref, vt_ref, *scratch, nb):
    t = MOBA_BLOCK
    step = pl.program_id(2)

    @pl.when(step == 0)
    def _():
        _fill_v_transposed(v_ref, vt_ref, t, MOBA_HEAD_DIM)
        _fill_causal_cap(scratch[0], t)
        kmean_ref[...] = jnp.zeros_like(kmean_ref)
        for j in range(nb):
            kb = k_ref[0, j * t:(j + 1) * t, :].astype(F32)
            kmean_ref[j:j + 1, :] = jnp.mean(kb, axis=0, keepdims=True)

    lane = lax.broadcasted_iota(jnp.int32, (t, LANES), 1)
    lane_k = lax.broadcasted_iota(jnp.int32, (LANES, LANES), 1)
    blk_id = lax.broadcasted_iota(jnp.int32, (nb, t), 0)
    nb_rows = -(-nb // SUBLANES) * SUBLANES
    pairs = qh_ref.shape[0]
    pair_g = [pairs * step + pair for pair in range(pairs)]
    for pair, g in enumerate(pair_g):
        for row, qi in enumerate((g, nb - 1 - g)):
            q_pair = q_ref[0, _tile_rows(qi, t), :]
            for hh in _HEADS:
                in_head = (lane >= hh * MOBA_HEAD_DIM) & (lane < (hh + 1) * MOBA_HEAD_DIM)
                qt_gate = jnp.where(in_head, q_pair, jnp.zeros_like(q_pair)).astype(F32).T
                qh_ref[pair, row, hh] = (qt_gate * (MOBA_HEAD_DIM ** -0.5 * LOG2_E)).astype(BF16)
                in_head_k = (lane_k >= hh * MOBA_HEAD_DIM) & (lane_k < (hh + 1) * MOBA_HEAD_DIM)
                km = jnp.where(in_head_k, kmean_ref[...], 0.0)[:nb_rows]
                gate = jnp.dot(km, qt_gate, precision=lax.Precision.HIGHEST, preferred_element_type=F32)[:nb]
                left = jnp.where(blk_id < qi, gate, NEG_INF)
                keep = blk_id == qi
                for r in range(MOBA_TOPK):
                    is_max = left == jnp.max(left, axis=0, keepdims=True)
                    pick = blk_id == jnp.min(jnp.where(is_max, blk_id, nb), axis=0, keepdims=True)
                    keep = keep | (pick & (qi > r))
                    left = jnp.where(pick, -jnp.inf, left)
                cap = jnp.where(keep, jnp.inf, NEG_INF)
                for j in range(nb):
                    cap_ref[pair, row, hh, j] = cap[j:j + 1, :]

    def load_k(kj):
        return [k_ref[0, _tile_rows(kj, t), :]] * HEAD_PAIR

    def write_out(qt, tile):
        o_ref[0, _tile_rows(qt, t), :] = tile

    for pair, g in enumerate(pair_g):
        load_qt = lambda row, pair=pair: [qh_ref[pair, row, h] for h in _HEADS]
        row_cap = lambda row, h, kj, pair=pair: cap_ref[pair, row, h, kj]
        _attend_rows(g, nb, t, load_qt, load_k, vt_ref, write_out, _pair_scratch(scratch, pair),
                     MOBA_HEAD_DIM, row_cap=row_cap)


def _moba_attn(p3):
    batch, seq, _ = p3.shape
    t = MOBA_BLOCK
    nb = seq // t
    pairs = _pairs_per_step(nb, MOBA_TILE_PAIRS)
    per_group = GROUP // LANES
    return pl.pallas_call(
        functools.partial(_moba_kernel, nb=nb),
        grid=(batch, MOBA_HEADS // HEAD_PAIR, nb // (2 * pairs)),
        in_specs=[
            pl.BlockSpec((1, seq, LANES), lambda b, h, g: (b, 0, G_MOBA_Q * per_group + h)),
            pl.BlockSpec((1, seq, LANES), lambda b, h, g: (b, 0, G_MOBA_K * per_group + h)),
            pl.BlockSpec((1, seq, LANES), lambda b, h, g: (b, 0, G_MOBA_V * per_group + h)),
        ],
        out_specs=pl.BlockSpec((1, seq, LANES), lambda b, h, g: (b, 0, h)),
        out_shape=jax.ShapeDtypeStruct((batch, seq, MOBA_HEADS * MOBA_HEAD_DIM), BF16),
        scratch_shapes=[pltpu.VMEM((LANES, LANES), F32),
                        pltpu.VMEM((pairs, len(_ROWS), HEAD_PAIR, LANES, t), BF16),
                        pltpu.VMEM((pairs, len(_ROWS), HEAD_PAIR, nb, 1, t), F32)]
                       + _attn_scratch(seq, t, MOBA_HEAD_DIM, pairs),
        compiler_params=pltpu.CompilerParams(
            dimension_semantics=("parallel", "parallel", "arbitrary"), vmem_limit_bytes=VMEM_LIMIT),
        name="moba_attn",
    )(p3, p3, p3)


def _retention_kernel(qk_ref, v_ref, g_ref, decay_ref, xi_ref, zeta_ref, gamma_ref, gnw_ref, gnb_ref,
                      o_ref, state_ref, *, chunks):
    c = RET_CHUNK

    @pl.when(pl.program_id(1) == 0)
    def _():
        state_ref[...] = jnp.zeros_like(state_ref)

    lane = lax.broadcasted_iota(jnp.int32, (c, LANES), 1)
    k_off = RET_HEADS * RET_QK_DIM
    for n in range(chunks):
        rows = slice(n * c, (n + 1) * c)
        for h in range(RET_HEADS):
            pair, half = divmod(h, HEAD_PAIR)
            in_head = (lane >= half * RET_QK_DIM) & (lane < (half + 1) * RET_QK_DIM)
            q = jnp.where(in_head, qk_ref[0, rows, pair * LANES:(pair + 1) * LANES].astype(F32), 0.0)
            k = jnp.where(in_head, qk_ref[0, rows, k_off + pair * LANES:k_off + (pair + 1) * LANES].astype(F32), 0.0)
            vs = slice(h * RET_V_DIM, (h + 1) * RET_V_DIM)
            v = v_ref[0, rows, vs]
            state = state_ref[h]
            scores = _nt_dot(q.astype(BF16), k.astype(BF16)) * decay_ref[h]
            inner = jnp.dot(scores.astype(BF16), v, preferred_element_type=F32)
            cross = jnp.dot((q * xi_ref[h]).astype(BF16), state.astype(BF16), preferred_element_type=F32)
            kv = lax.dot_general((k * zeta_ref[h]).astype(BF16), v, (((0,), (0,)), ((), ())),
                                 preferred_element_type=F32)
            state_ref[h] = gamma_ref[h] * state + kv
            o = inner + cross
            mu = jnp.mean(o, axis=-1, keepdims=True)
            d = o - mu
            var = jnp.mean(d * d, axis=-1, keepdims=True)
            o = d * lax.rsqrt(var + GN_EPS) * gnw_ref[:, vs] + gnb_ref[:, vs]
            g = g_ref[0, rows, vs].astype(F32)
            o_ref[0, rows, vs] = (g * jax.nn.sigmoid(g) * o).astype(BF16)


def _retention(p3, decay, xi, zeta, gamma, gnw, gnb, tr):
    batch, seq, _ = p3.shape
    width = RET_HEADS * RET_V_DIM
    const = lambda shape: pl.BlockSpec(shape, lambda b, i: (0,) * len(shape))
    return pl.pallas_call(
        functools.partial(_retention_kernel, chunks=tr // RET_CHUNK),
        grid=(batch, seq // tr),
        in_specs=[
            pl.BlockSpec((1, tr, GROUP), lambda b, i: (b, i, G_RET_QK)),
            pl.BlockSpec((1, tr, GROUP), lambda b, i: (b, i, G_RET_V)),
            pl.BlockSpec((1, tr, GROUP), lambda b, i: (b, i, G_RET_G)),
            const((RET_HEADS, RET_CHUNK, RET_CHUNK)),
            const((RET_HEADS, RET_CHUNK, LANES)), const((RET_HEADS, RET_CHUNK, LANES)),
            const((RET_HEADS, 1, LANES)),
            const((1, width)), const((1, width)),
        ],
        out_specs=pl.BlockSpec((1, tr, width), lambda b, i: (b, i, 0)),
        out_shape=jax.ShapeDtypeStruct((batch, seq, width), BF16),
        scratch_shapes=[pltpu.VMEM((RET_HEADS, LANES, RET_V_DIM), F32)],
        compiler_params=pltpu.CompilerParams(
            dimension_semantics=("parallel", "arbitrary"), vmem_limit_bytes=VMEM_LIMIT),
        name="retention",
    )(p3, p3, p3, decay, xi, zeta, gamma, gnw, gnb)


def _merge_mlp_kernel(x_ref, gates_ref, oa_ref, ob_ref, oc_ref, wa_ref, wb_ref, wc_ref, wo_ref,
                      g_ref, wu_ref, wd_ref, fg_ref, o_ref, h_ref, *, tf, final_norm):
    merged = None
    for idx, (o_in, w) in enumerate(((oa_ref, wa_ref), (ob_ref, wb_ref), (oc_ref, wc_ref))):
        gate = gates_ref[:, idx * D_MODEL:(idx + 1) * D_MODEL].astype(F32)
        term = jax.nn.sigmoid(gate) * jnp.dot(o_in[...], w[...], preferred_element_type=F32)
        merged = term if merged is None else merged + term
    x = x_ref[...] + jnp.dot(merged.astype(BF16), wo_ref[...], preferred_element_type=F32)

    h_ref[...] = _rms(x, g_ref[...]).astype(BF16)
    acc = x
    for f in range(D_FF // tf):
        cols = slice(f * tf, (f + 1) * tf)
        u = jnp.maximum(jnp.dot(h_ref[...], wu_ref[:, cols], preferred_element_type=F32), 0.0)
        acc = acc + jnp.dot((u * u).astype(BF16), wd_ref[cols, :], preferred_element_type=F32)
    o_ref[...] = _rms(acc, fg_ref[...]) if final_norm else acc


def _merge_mlp(x2, p2, oa, ob, oc, wa, wb, wc, wo, g, wu, wd, fg, tm, tf, final_norm):
    tokens = x2.shape[0]
    row = lambda w: pl.BlockSpec((tm, w), lambda i: (i, 0))
    vec = pl.BlockSpec((1, D_MODEL), lambda i: (0, 0))
    resident = lambda shape: pl.BlockSpec(shape, lambda i: (0, 0), pipeline_mode=pl.Buffered(1))
    return pl.pallas_call(
        functools.partial(_merge_mlp_kernel, tf=tf, final_norm=final_norm),
        grid=(tokens // tm,),
        in_specs=[row(D_MODEL), row(3 * D_MODEL), row(GROUP), row(GROUP), row(GROUP),
                  resident((GROUP, D_MODEL)), resident((GROUP, D_MODEL)), resident((GROUP, D_MODEL)),
                  resident((D_MODEL, D_MODEL)),
                  vec, resident((D_MODEL, D_FF)), resident((D_FF, D_MODEL)), vec],
        out_specs=row(D_MODEL),
        out_shape=jax.ShapeDtypeStruct((tokens, D_MODEL), F32),
        scratch_shapes=[pltpu.VMEM((tm, D_MODEL), BF16)],
        compiler_params=pltpu.CompilerParams(
            dimension_semantics=("parallel",), vmem_limit_bytes=VMEM_LIMIT),
        name="merge_mlp",
    )(x2, p2, oa, ob, oc, wa, wb, wc, wo, g, wu, wd, fg)


def _rope_cos_sin(seq, dim, theta):
    inv = 1.0 / (theta ** (jnp.arange(0, dim, 2, dtype=F32) / dim))
    ang = jnp.arange(seq, dtype=F32)[:, None] * inv[None, :]
    return jnp.cos(ang), jnp.sin(ang)


def _rope_tables(seq, dim, theta, period, offset):
    cos, sin = _rope_cos_sin(seq, dim, theta)
    half = dim // 2
    const = lambda value, n: [jnp.full((seq, n), value, F32)] if n else []
    before, after = offset, period - offset - dim

    def lanes(first, second, fill):
        parts = const(fill, before) + [first, second] + const(fill, after)
        return jnp.concatenate(parts * (LANES // period), axis=1)

    zero = jnp.zeros_like(sin)
    return jnp.stack([lanes(cos, cos, 1.0), lanes(-sin, zero, 0.0), lanes(zero, sin, 0.0)])


def _retention_tables():
    c = RET_CHUNK
    log_gamma = jnp.log(1.0 - 2.0 ** (-5.0 - jnp.arange(RET_HEADS, dtype=F32)))
    pos = jnp.arange(c, dtype=F32)
    diff = pos[:, None] - pos[None, :]
    decay = jnp.where(diff >= 0, jnp.exp(log_gamma[:, None, None] * diff), 0.0)
    xi = jnp.exp(log_gamma[:, None] * (pos + 1.0))
    zeta = jnp.exp(log_gamma[:, None] * (c - 1.0 - pos))
    gamma_c = jnp.exp(log_gamma * c)
    bcast = lambda t: jnp.broadcast_to(t[..., None], t.shape + (LANES,))
    return decay, bcast(xi), bcast(zeta), bcast(gamma_c[:, None])


def _pad_mla_weights(w_q_up, w_kv_up):
    d_qk = MLA_NOPE + MLA_ROPE
    wq = w_q_up.reshape(MLA_Q_RANK, MLA_HEADS, d_qk)
    wq = jnp.pad(wq, ((0, 0), (0, 0), (0, LANES - d_qk))).reshape(MLA_Q_RANK, MLA_HEADS * LANES)
    wkv = w_kv_up.reshape(MLA_KV_RANK, MLA_HEADS, MLA_NOPE + MLA_V)
    wk = jnp.pad(wkv[:, :, :MLA_NOPE], ((0, 0), (0, 0), (0, LANES - MLA_NOPE)))
    wk = wk.reshape(MLA_KV_RANK, MLA_HEADS * LANES)
    wv = wkv[:, :, MLA_NOPE:].reshape(MLA_KV_RANK, MLA_HEADS * MLA_V)
    return wq.astype(BF16), wk.astype(BF16), wv.astype(BF16)


def _tile_sizes(seq):
    pick = lambda want: want if seq % want == 0 else seq
    return dict(in_proj=pick(512), attn=pick(256), ret=pick(512),
                mlp=pick(512), mlp_ff=512)


def kernel(x, attn_norm, w_in, mla_q_norm, mla_kv_norm, mla_w_q_up, mla_w_kv_up, ret_gn_w, ret_gn_b,
           w_branch_mla, w_branch_moba, w_branch_ret, w_out, mlp_norm, w_mlp_up, w_mlp_down, final_norm):
    batch, seq, d = x.shape
    depth = w_in.shape[0]
    tiles = _tile_sizes(seq)
    mla_tab = _rope_tables(seq, MLA_ROPE, ROPE_THETA, LANES, MLA_NOPE)
    moba_tab = _rope_tables(seq, MOBA_HEAD_DIM // PARTIAL_ROPE_DIV, ROPE_THETA, MOBA_HEAD_DIM, 0)
    ret_tab = _rope_tables(seq, RET_QK_DIM, RET_THETA, RET_QK_DIM, 0)
    decay, xi, zeta, gamma_c = _retention_tables()
    row = lambda v: v.reshape(1, -1)

    assert w_in.shape[1:] == (D_MODEL, IN_COLS)
    w_in_t = jnp.swapaxes(w_in, 1, 2).astype(BF16)
    x2 = x.reshape(batch * seq, d)
    for l in range(depth):
        wq, wk, wv = _pad_mla_weights(mla_w_q_up[l], mla_w_kv_up[l])
        p2, q_mla, k_mla, v_mla = _in_proj(x2, row(attn_norm[l]), w_in_t[l], moba_tab, ret_tab, mla_tab,
                                           row(mla_q_norm[l]), row(mla_kv_norm[l]), wq, wk, wv,
                                           seq, tiles["in_proj"])
        p3 = p2.reshape(batch, seq, IN_COLS_PADDED)
        o_mla = _mla_attn(q_mla, k_mla, v_mla, tiles["attn"])
        o_moba = _moba_attn(p3)
        o_ret = _retention(p3, decay, xi, zeta, gamma_c, row(ret_gn_w[l]), row(ret_gn_b[l]), tiles["ret"])
        flat = lambda o: o.reshape(batch * seq, GROUP)
        x2 = _merge_mlp(x2, p2, flat(o_mla), flat(o_moba), flat(o_ret),
                        w_branch_mla[l].astype(BF16), w_branch_moba[l].astype(BF16),
                        w_branch_ret[l].astype(BF16), w_out[l].astype(BF16),
                        row(mlp_norm[l]), w_mlp_up[l].astype(BF16), w_mlp_down[l].astype(BF16),
                        row(final_norm), tiles["mlp"], tiles["mlp_ff"], final_norm=(l == depth - 1))
    return x2.reshape(batch, seq, d)
```

```python
import functools
import math

import jax
import jax.numpy as jnp
from jax import lax
from jax.experimental import pallas as pl
from jax.experimental.pallas import tpu as pltpu

F32 = jnp.float32
BF16 = jnp.bfloat16

D_MODEL = 1024
MLA_HEADS = 8
MLA_Q_RANK = 256
MLA_KV_RANK = 128
MLA_NOPE = 64
MLA_ROPE = 32
MLA_V = 64
MOBA_HEADS = 8
MOBA_HEAD_DIM = 64
MOBA_BLOCK = 256
MOBA_TOPK = 3
RET_HEADS = 4
RET_QK_DIM = 64
RET_V_DIM = 128
RET_CHUNK = 128
RET_THETA = 10000.0
ROPE_THETA = 500000.0
PARTIAL_ROPE_DIV = 4
D_FF = 4 * D_MODEL
NORM_EPS = 1e-6
GN_EPS = 1e-5
NEG_INF = -1e30
LOG2_E = 1.4426950408889634

LANES = 128
SUBLANES = 8
BF16_ROWS = 16
HEAD_PAIR = 2
SCORE_DEPTH = 3
MLA_TILE_PAIRS = 4
MOBA_TILE_PAIRS = 2
VMEM_LIMIT = 48 * 1024 * 1024

GROUP = 512
G_GATES = 0
G_MOBA_Q, G_MOBA_K, G_MOBA_V = 6, 7, 8
G_RET_QK, G_RET_V, G_RET_G = 9, 10, 11
N_GROUPS = 12
IN_COLS_PADDED = N_GROUPS * GROUP
W_KR = MLA_Q_RANK + MLA_KV_RANK
W_MIXERS = W_KR + MLA_ROPE
W_GATES = W_MIXERS + (G_RET_G + 1 - G_MOBA_Q) * GROUP
IN_COLS = W_GATES + 3 * D_MODEL


def _nt_dot(a, b):
    return lax.dot_general(a, b, (((1,), (1,)), ((), ())), preferred_element_type=F32)


def _rms(x, g):
    return x * lax.rsqrt(jnp.mean(x * x, axis=-1, keepdims=True) + NORM_EPS) * g


def _rope_lanes(y, tab_ref, half):
    return (y * tab_ref[0]
            + pltpu.roll(y, LANES - half, 1) * tab_ref[1]
            + pltpu.roll(y, half, 1) * tab_ref[2])


def _mla_heads(latent, kr, tab, qg_ref, kvg_ref, wq_ref, wk_ref, wv_ref, q_ref, k_ref, v_ref):
    cq = latent[:, :MLA_Q_RANK]
    ckv = latent[:, MLA_Q_RANK:]
    cqn = _rms(cq, qg_ref[...]).astype(BF16)
    ckvn = _rms(ckv, kvg_ref[...]).astype(BF16)
    q = jnp.dot(cqn, wq_ref[...], preferred_element_type=F32)
    k = jnp.dot(ckvn, wk_ref[...], preferred_element_type=F32)
    v = jnp.dot(ckvn, wv_ref[...], preferred_element_type=F32)
    kpe = _rope_lanes(kr, tab, MLA_ROPE // 2)
    scale = (MLA_NOPE + MLA_ROPE) ** -0.5 * LOG2_E
    for h in range(MLA_HEADS):
        sl = slice(h * LANES, (h + 1) * LANES)
        q_ref[0, h] = (_rope_lanes(q[:, sl], tab, MLA_ROPE // 2) * scale).astype(BF16)
        k_ref[0, h] = (k[:, sl] + kpe).astype(BF16)
    v_ref[0] = v.astype(BF16)


def _in_proj_kernel(x_ref, g_ref, wt_ref, moba_tab, ret_tab, mla_tab, qg_ref, kvg_ref, wq_ref, wk_ref, wv_ref,
                    p_ref, q_ref, k_ref, v_ref, h_ref, kr_ref):
    @pl.when(pl.program_id(0) == 0)
    def _():
        kr_ref[...] = jnp.zeros_like(kr_ref)
        kr_ref[MLA_NOPE:MLA_NOPE + MLA_ROPE, :] = wt_ref[W_KR:W_MIXERS, :]

    h_ref[...] = _rms(x_ref[...], g_ref[...]).astype(BF16)
    _mla_heads(_nt_dot(h_ref[...], wt_ref[:W_KR, :]), _nt_dot(h_ref[...], kr_ref[...]),
               mla_tab, qg_ref, kvg_ref, wq_ref, wk_ref, wv_ref, q_ref, k_ref, v_ref)
    lane_tiles = GROUP // LANES
    for j in range(N_GROUPS):
        first = W_GATES + j * GROUP if j < G_MOBA_Q else W_MIXERS + (j - G_MOBA_Q) * GROUP
        acc = _nt_dot(h_ref[...], wt_ref[first:first + GROUP, :])
        for r in range(lane_tiles):
            y = acc[:, r * LANES:(r + 1) * LANES]
            if j in (G_MOBA_Q, G_MOBA_K):
                y = _rope_lanes(y, moba_tab, MOBA_HEAD_DIM // PARTIAL_ROPE_DIV // 2)
            elif j == G_RET_QK:
                y = _rope_lanes(y, ret_tab, RET_QK_DIM // 2)
                if r >= lane_tiles // 2:
                    y = y * (RET_QK_DIM ** -0.5)
            p_ref[:, j * GROUP + r * LANES:j * GROUP + (r + 1) * LANES] = y.astype(BF16)


def _layer_block(layer, shape, resident=False):
    mode = dict(pipeline_mode=pl.Buffered(1)) if resident else {}
    return pl.BlockSpec((None,) + shape, lambda *_: (layer,) + (0,) * len(shape), **mode)


def _in_proj(layer, x2, g, w, moba_tab, ret_tab, mla_tab, qg, kvg, wq, wk, wv, seq, tm):
    tokens = x2.shape[0]
    s_tiles = seq // tm
    batch = tokens // seq
    tab = pl.BlockSpec((3, tm, LANES), lambda i: (0, i % s_tiles, 0))
    const = lambda shape: _layer_block(layer, shape)
    heads = pl.BlockSpec((1, MLA_HEADS, tm, LANES), lambda i: (i // s_tiles, 0, i % s_tiles, 0))
    head_shape = jax.ShapeDtypeStruct((batch, MLA_HEADS, seq, LANES), BF16)
    return pl.pallas_call(
        _in_proj_kernel,
        grid=(tokens // tm,),
        in_specs=[
            pl.BlockSpec((tm, D_MODEL), lambda i: (i, 0)),
            const((1, D_MODEL)),
            _layer_block(layer, (IN_COLS, D_MODEL), resident=True),
            tab, tab, tab,
            const((1, MLA_Q_RANK)), const((1, MLA_KV_RANK)),
            const((MLA_Q_RANK, MLA_HEADS * LANES)), const((MLA_KV_RANK, MLA_HEADS * LANES)),
            const((MLA_KV_RANK, MLA_HEADS * MLA_V)),
        ],
        out_specs=[
            pl.BlockSpec((tm, IN_COLS_PADDED), lambda i: (i, 0)),
            heads, heads,
            pl.BlockSpec((1, tm, MLA_HEADS * MLA_V), lambda i: (i // s_tiles, i % s_tiles, 0)),
        ],
        out_shape=[jax.ShapeDtypeStruct((tokens, IN_COLS_PADDED), BF16), head_shape, head_shape,
                   jax.ShapeDtypeStruct((batch, seq, MLA_HEADS * MLA_V), BF16)],
        scratch_shapes=[pltpu.VMEM((tm, D_MODEL), BF16), pltpu.VMEM((LANES, D_MODEL), BF16)],
        compiler_params=pltpu.CompilerParams(
            dimension_semantics=("arbitrary",), vmem_limit_bytes=VMEM_LIMIT),
        name="in_proj",
    )(x2, g, w, moba_tab, ret_tab, mla_tab, qg, kvg, wq, wk, wv)


_HEADS = range(HEAD_PAIR)
_ROWS = range(2)


def _fill_v_transposed(v_ref, vt_ref, t, head_dim):
    sub = lax.broadcasted_iota(jnp.int32, (BF16_ROWS, t), 0)
    ones_block = jnp.where(sub == 0, 1.0, 0.0)
    for j in range(vt_ref.shape[1]):
        vt = v_ref[0, j * t:(j + 1) * t, :].astype(F32).T
        for h in _HEADS:
            own = vt[h * head_dim:(h + 1) * head_dim]
            vt_ref[h, j] = jnp.concatenate([own, ones_block], axis=0).astype(BF16)


def _fill_causal_cap(cap_ref, t):
    key = lax.broadcasted_iota(jnp.int32, (t, t), 0)
    query = lax.broadcasted_iota(jnp.int32, (t, t), 1)
    cap_ref[...] = jnp.where(key <= query, jnp.inf, NEG_INF)


def _attend_rows(g, n_tiles, t, load_qt, load_k, vt_ref, write_out, scratch, head_dim, row_drop=None):
    causal_cap_ref, bufs, m_ref, acc_ref = scratch
    depth = bufs.shape[0]
    q_tile = [g, n_tiles - 1 - g]
    m_ref[...] = jnp.full_like(m_ref, -jnp.inf)
    acc_ref[...] = jnp.zeros_like(acc_ref)

    def locate(n):
        row = (n > q_tile[0]).astype(jnp.int32)
        return row, jnp.where(row == 0, q_tile[0] - n, n - (q_tile[0] + 1))

    def issue_scores(n, slot):
        row, kj = locate(n)
        ks, qts = load_k(kj), load_qt(row)
        for h in _HEADS:
            bufs[slot, h] = jnp.dot(ks[h], qts[h], preferred_element_type=F32)

    def softmax_step(n, slot, diagonal=False):
        row, kj = locate(n)
        chain = [HEAD_PAIR * row + h for h in _HEADS]
        s = [bufs[slot, h] for h in _HEADS]
        if diagonal:
            s = [jnp.minimum(s[h], causal_cap_ref[...]) for h in _HEADS]
        tile_max = [jnp.max(s[h], axis=0, keepdims=True) for h in _HEADS]
        shift = None
        if row_drop is not None and not diagonal:
            shift = [row_drop(row, h, kj) for h in _HEADS]
            tile_max = [jnp.maximum(tile_max[h] - shift[h], NEG_INF) for h in _HEADS]
        m_old = [m_ref[chain[h]] for h in _HEADS]
        m_new = [jnp.maximum(m_old[h], tile_max[h]) for h in _HEADS]
        alpha = [jnp.exp2(m_old[h] - m_new[h]) for h in _HEADS]
        m_sub = m_new if shift is None else [m_new[h] + shift[h] for h in _HEADS]
        p = [jnp.exp2(s[h] - m_sub[h]).astype(BF16) for h in _HEADS]
        pv = [jnp.dot(vt_ref[h, kj], p[h], preferred_element_type=F32) for h in _HEADS]
        for h in _HEADS:
            m_ref[chain[h]] = m_new[h]
            acc_ref[chain[h]] = alpha[h] * acc_ref[chain[h]] + pv[h]

    steps = n_tiles + 1
    for n in range(min(depth, steps)):
        issue_scores(n, n)
    for n in range(steps):
        softmax_step(n, n % depth, diagonal=n in (0, n_tiles))
        if n + depth < steps:
            issue_scores(n + depth, n % depth)

    for row in _ROWS:
        outs = []
        for h in _HEADS:
            acc = acc_ref[HEAD_PAIR * row + h]
            outs.append(acc[:head_dim] * (1.0 / acc[head_dim:head_dim + 1]))
        ot = jnp.concatenate(outs, axis=0)
        write_out(q_tile[row], ot.T.astype(BF16))


def _attn_scratch(seq, t, head_dim, pairs):
    chains = HEAD_PAIR * len(_ROWS)
    v_rows = head_dim + BF16_ROWS
    return [pltpu.VMEM((HEAD_PAIR, seq // t, v_rows, t), BF16),
            pltpu.VMEM((t, t), F32),
            pltpu.VMEM((pairs, SCORE_DEPTH, HEAD_PAIR, t, t), F32),
            pltpu.VMEM((pairs, chains, 1, t), F32),
            pltpu.VMEM((pairs, chains, v_rows, t), F32)]


def _pairs_per_step(n_tiles, wanted):
    assert n_tiles % 2 == 0
    return math.gcd(wanted, n_tiles // 2)


def _pair_scratch(scratch, pair):
    causal_cap_ref, bufs, m_ref, acc_ref = scratch
    return causal_cap_ref, bufs.at[pair], m_ref.at[pair], acc_ref.at[pair]


def _tile_rows(i, t):
    return pl.ds(pl.multiple_of(i * t, t), t)


def _mla_attn_kernel(q_ref, k_ref, v_ref, o_ref, qt_ref, vt_ref, *scratch, t, n_tiles):
    step = pl.program_id(2)

    @pl.when(step == 0)
    def _():
        _fill_v_transposed(v_ref, vt_ref, t, MLA_V)
        _fill_causal_cap(scratch[0], t)

    def load_k(kj):
        return [k_ref[0, h, _tile_rows(kj, t), :] for h in _HEADS]

    def write_out(qt, tile):
        o_ref[0, _tile_rows(qt, t), :] = tile

    pairs = qt_ref.shape[0]
    pair_g = [pairs * step + pair for pair in range(pairs)]
    for pair, g in enumerate(pair_g):
        for row, qi in enumerate((g, n_tiles - 1 - g)):
            for h in _HEADS:
                qt_ref[pair, row, h] = q_ref[0, h, _tile_rows(qi, t), :].astype(F32).T.astype(BF16)
    for pair, g in enumerate(pair_g):
        load_qt = lambda row, pair=pair: [qt_ref[pair, row, h] for h in _HEADS]
        _attend_rows(g, n_tiles, t, load_qt, load_k, vt_ref, write_out, _pair_scratch(scratch, pair), MLA_V)


def _mla_attn(q, k, v, t):
    batch, heads, seq, _ = q.shape
    n_tiles = seq // t
    pairs = _pairs_per_step(n_tiles, MLA_TILE_PAIRS)
    return pl.pallas_call(
        functools.partial(_mla_attn_kernel, t=t, n_tiles=n_tiles),
        grid=(batch, heads // HEAD_PAIR, n_tiles // (2 * pairs)),
        in_specs=[
            pl.BlockSpec((1, HEAD_PAIR, seq, LANES), lambda b, h, g: (b, h, 0, 0)),
            pl.BlockSpec((1, HEAD_PAIR, seq, LANES), lambda b, h, g: (b, h, 0, 0)),
            pl.BlockSpec((1, seq, LANES), lambda b, h, g: (b, 0, h)),
        ],
        out_specs=pl.BlockSpec((1, seq, LANES), lambda b, h, g: (b, 0, h)),
        out_shape=jax.ShapeDtypeStruct((batch, seq, heads * MLA_V), BF16),
        scratch_shapes=[pltpu.VMEM((pairs, len(_ROWS), HEAD_PAIR, LANES, t), BF16)]
                       + _attn_scratch(seq, t, MLA_V, pairs),
        compiler_params=pltpu.CompilerParams(
            dimension_semantics=("parallel", "parallel", "arbitrary"), vmem_limit_bytes=VMEM_LIMIT),
        name="mla_attn",
    )(q, k, v)


def _moba_kernel(q_ref, k_ref, v_ref, o_ref, kmean_ref, qh_ref, drop_ref, vt_ref, *scratch, nb):
    t = MOBA_BLOCK
    step = pl.program_id(2)

    @pl.when(step == 0)
    def _():
        _fill_v_transposed(v_ref, vt_ref, t, MOBA_HEAD_DIM)
        _fill_causal_cap(scratch[0], t)
        kmean_ref[...] = jnp.zeros_like(kmean_ref)
        for j in range(nb):
            kb = k_ref[0, j * t:(j + 1) * t, :].astype(F32)
            kmean_ref[j:j + 1, :] = jnp.mean(kb, axis=0, keepdims=True)

    lane = lax.broadcasted_iota(jnp.int32, (t, LANES), 1)
    lane_k = lax.broadcasted_iota(jnp.int32, (LANES, LANES), 1)
    blk_id = lax.broadcasted_iota(jnp.int32, (nb, t), 0)
    nb_rows = -(-nb // SUBLANES) * SUBLANES
    pairs = qh_ref.shape[0]
    pair_g = [pairs * step + pair for pair in range(pairs)]
    for pair, g in enumerate(pair_g):
        for row, qi in enumerate((g, nb - 1 - g)):
            q_pair = q_ref[0, _tile_rows(qi, t), :]
            for hh in _HEADS:
                in_head = (lane >= hh * MOBA_HEAD_DIM) & (lane < (hh + 1) * MOBA_HEAD_DIM)
                qt_gate = jnp.where(in_head, q_pair, jnp.zeros_like(q_pair)).astype(F32).T
                qh_ref[pair, row, hh] = (qt_gate * (MOBA_HEAD_DIM ** -0.5 * LOG2_E)).astype(BF16)
                in_head_k = (lane_k >= hh * MOBA_HEAD_DIM) & (lane_k < (hh + 1) * MOBA_HEAD_DIM)
                km = jnp.where(in_head_k, kmean_ref[...], 0.0)[:nb_rows]
                gate = jnp.dot(km, qt_gate, precision=lax.Precision.HIGHEST, preferred_element_type=F32)[:nb]
                left = jnp.where(blk_id < qi, gate, NEG_INF)
                keep = blk_id == qi
                for r in range(MOBA_TOPK):
                    is_max = left == jnp.max(left, axis=0, keepdims=True)
                    pick = blk_id == jnp.min(jnp.where(is_max, blk_id, nb), axis=0, keepdims=True)
                    keep = keep | (pick & (qi > r))
                    left = jnp.where(pick, -jnp.inf, left)
                drop = jnp.where(keep, 0.0, jnp.inf)
                for j in range(nb):
                    drop_ref[pair, row, hh, j] = drop[j:j + 1, :]

    def load_k(kj):
        return [k_ref[0, _tile_rows(kj, t), :]] * HEAD_PAIR

    def write_out(qt, tile):
        o_ref[0, _tile_rows(qt, t), :] = tile

    for pair, g in enumerate(pair_g):
        load_qt = lambda row, pair=pair: [qh_ref[pair, row, h] for h in _HEADS]
        row_drop = lambda row, h, kj, pair=pair: drop_ref[pair, row, h, kj]
        _attend_rows(g, nb, t, load_qt, load_k, vt_ref, write_out, _pair_scratch(scratch, pair),
                     MOBA_HEAD_DIM, row_drop=row_drop)


def _moba_attn(p3):
    batch, seq, _ = p3.shape
    t = MOBA_BLOCK
    nb = seq // t
    pairs = _pairs_per_step(nb, MOBA_TILE_PAIRS)
    per_group = GROUP // LANES
    return pl.pallas_call(
        functools.partial(_moba_kernel, nb=nb),
        grid=(batch, MOBA_HEADS // HEAD_PAIR, nb // (2 * pairs)),
        in_specs=[
            pl.BlockSpec((1, seq, LANES), lambda b, h, g: (b, 0, G_MOBA_Q * per_group + h)),
            pl.BlockSpec((1, seq, LANES), lambda b, h, g: (b, 0, G_MOBA_K * per_group + h)),
            pl.BlockSpec((1, seq, LANES), lambda b, h, g: (b, 0, G_MOBA_V * per_group + h)),
        ],
        out_specs=pl.BlockSpec((1, seq, LANES), lambda b, h, g: (b, 0, h)),
        out_shape=jax.ShapeDtypeStruct((batch, seq, MOBA_HEADS * MOBA_HEAD_DIM), BF16),
        scratch_shapes=[pltpu.VMEM((LANES, LANES), F32),
                        pltpu.VMEM((pairs, len(_ROWS), HEAD_PAIR, LANES, t), BF16),
                        pltpu.VMEM((pairs, len(_ROWS), HEAD_PAIR, nb, 1, t), F32)]
                       + _attn_scratch(seq, t, MOBA_HEAD_DIM, pairs),
        compiler_params=pltpu.CompilerParams(
            dimension_semantics=("parallel", "parallel", "arbitrary"), vmem_limit_bytes=VMEM_LIMIT),
        name="moba_attn",
    )(p3, p3, p3)


def _retention_kernel(qk_ref, v_ref, g_ref, decay_ref, xi_ref, zeta_ref, gamma_ref, gnw_ref, gnb_ref,
                      o_ref, state_ref, *, chunks):
    c = RET_CHUNK

    @pl.when(pl.program_id(1) == 0)
    def _():
        state_ref[...] = jnp.zeros_like(state_ref)

    lane = lax.broadcasted_iota(jnp.int32, (c, LANES), 1)
    k_off = RET_HEADS * RET_QK_DIM
    for n in range(chunks):
        rows = slice(n * c, (n + 1) * c)
        for h in range(RET_HEADS):
            pair, half = divmod(h, HEAD_PAIR)
            in_head = (lane >= half * RET_QK_DIM) & (lane < (half + 1) * RET_QK_DIM)
            q = jnp.where(in_head, qk_ref[0, rows, pair * LANES:(pair + 1) * LANES].astype(F32), 0.0)
            k = jnp.where(in_head, qk_ref[0, rows, k_off + pair * LANES:k_off + (pair + 1) * LANES].astype(F32), 0.0)
            vs = slice(h * RET_V_DIM, (h + 1) * RET_V_DIM)
            v = v_ref[0, rows, vs]
            state = state_ref[h]
            scores = _nt_dot(q.astype(BF16), k.astype(BF16)) * decay_ref[h]
            inner = jnp.dot(scores.astype(BF16), v, preferred_element_type=F32)
            cross = jnp.dot((q * xi_ref[h]).astype(BF16), state.astype(BF16), preferred_element_type=F32)
            kv = lax.dot_general((k * zeta_ref[h]).astype(BF16), v, (((0,), (0,)), ((), ())),
                                 preferred_element_type=F32)
            state_ref[h] = gamma_ref[h] * state + kv
            o = inner + cross
            mu = jnp.mean(o, axis=-1, keepdims=True)
            d = o - mu
            var = jnp.mean(d * d, axis=-1, keepdims=True)
            o = d * lax.rsqrt(var + GN_EPS) * gnw_ref[:, vs] + gnb_ref[:, vs]
            g = g_ref[0, rows, vs].astype(F32)
            o_ref[0, rows, vs] = (g * jax.nn.sigmoid(g) * o).astype(BF16)


def _retention(layer, p3, decay, xi, zeta, gamma, gnw, gnb, tr):
    batch, seq, _ = p3.shape
    width = RET_HEADS * RET_V_DIM
    const = lambda shape: pl.BlockSpec(shape, lambda b, i: (0,) * len(shape))
    return pl.pallas_call(
        functools.partial(_retention_kernel, chunks=tr // RET_CHUNK),
        grid=(batch, seq // tr),
        in_specs=[
            pl.BlockSpec((1, tr, GROUP), lambda b, i: (b, i, G_RET_QK)),
            pl.BlockSpec((1, tr, GROUP), lambda b, i: (b, i, G_RET_V)),
            pl.BlockSpec((1, tr, GROUP), lambda b, i: (b, i, G_RET_G)),
            const((RET_HEADS, RET_CHUNK, RET_CHUNK)),
            const((RET_HEADS, RET_CHUNK, LANES)), const((RET_HEADS, RET_CHUNK, LANES)),
            const((RET_HEADS, 1, LANES)),
            _layer_block(layer, (1, width)), _layer_block(layer, (1, width)),
        ],
        out_specs=pl.BlockSpec((1, tr, width), lambda b, i: (b, i, 0)),
        out_shape=jax.ShapeDtypeStruct((batch, seq, width), BF16),
        scratch_shapes=[pltpu.VMEM((RET_HEADS, LANES, RET_V_DIM), F32)],
        compiler_params=pltpu.CompilerParams(
            dimension_semantics=("parallel", "arbitrary"), vmem_limit_bytes=VMEM_LIMIT),
        name="retention",
    )(p3, p3, p3, decay, xi, zeta, gamma, gnw, gnb)


def _merge_mlp_kernel(x_ref, gates_ref, oa_ref, ob_ref, oc_ref, wa_ref, wb_ref, wc_ref, wo_ref,
                      g_ref, wu_ref, wd_ref, fg_ref, o_ref, h_ref, *, tf, final_norm):
    merged = None
    for idx, (o_in, w) in enumerate(((oa_ref, wa_ref), (ob_ref, wb_ref), (oc_ref, wc_ref))):
        gate = gates_ref[:, idx * D_MODEL:(idx + 1) * D_MODEL].astype(F32)
        term = jax.nn.sigmoid(gate) * jnp.dot(o_in[...], w[...], preferred_element_type=F32)
        merged = term if merged is None else merged + term
    x = x_ref[...] + jnp.dot(merged.astype(BF16), wo_ref[...], preferred_element_type=F32)

    h_ref[...] = _rms(x, g_ref[...]).astype(BF16)
    acc = x
    for f in range(D_FF // tf):
        cols = slice(f * tf, (f + 1) * tf)
        u = jnp.maximum(jnp.dot(h_ref[...], wu_ref[:, cols], preferred_element_type=F32), 0.0)
        acc = acc + jnp.dot((u * u).astype(BF16), wd_ref[cols, :], preferred_element_type=F32)
    o_ref[...] = _rms(acc, fg_ref[...]) if final_norm else acc


def _merge_mlp(layer, x2, p2, oa, ob, oc, wa, wb, wc, wo, g, wu, wd, fg, tm, tf, final_norm):
    tokens = x2.shape[0]
    row = lambda w: pl.BlockSpec((tm, w), lambda i: (i, 0))
    resident = lambda shape: _layer_block(layer, shape, resident=True)
    return pl.pallas_call(
        functools.partial(_merge_mlp_kernel, tf=tf, final_norm=final_norm),
        grid=(tokens // tm,),
        in_specs=[row(D_MODEL), row(3 * D_MODEL), row(GROUP), row(GROUP), row(GROUP),
                  resident((GROUP, D_MODEL)), resident((GROUP, D_MODEL)), resident((GROUP, D_MODEL)),
                  resident((D_MODEL, D_MODEL)),
                  _layer_block(layer, (1, D_MODEL)), resident((D_MODEL, D_FF)), resident((D_FF, D_MODEL)),
                  pl.BlockSpec((1, D_MODEL), lambda i: (0, 0))],
        out_specs=row(D_MODEL),
        out_shape=jax.ShapeDtypeStruct((tokens, D_MODEL), F32),
        scratch_shapes=[pltpu.VMEM((tm, D_MODEL), BF16)],
        compiler_params=pltpu.CompilerParams(
            dimension_semantics=("parallel",), vmem_limit_bytes=VMEM_LIMIT),
        name="merge_mlp",
    )(x2, p2, oa, ob, oc, wa, wb, wc, wo, g, wu, wd, fg)


def _rope_cos_sin(seq, dim, theta):
    inv = 1.0 / (theta ** (jnp.arange(0, dim, 2, dtype=F32) / dim))
    ang = jnp.arange(seq, dtype=F32)[:, None] * inv[None, :]
    return jnp.cos(ang), jnp.sin(ang)


def _rope_tables(seq, dim, theta, period, offset):
    cos, sin = _rope_cos_sin(seq, dim, theta)
    half = dim // 2
    const = lambda value, n: [jnp.full((seq, n), value, F32)] if n else []
    before, after = offset, period - offset - dim

    def lanes(first, second, fill):
        parts = const(fill, before) + [first, second] + const(fill, after)
        return jnp.concatenate(parts * (LANES // period), axis=1)

    zero = jnp.zeros_like(sin)
    return jnp.stack([lanes(cos, cos, 1.0), lanes(-sin, zero, 0.0), lanes(zero, sin, 0.0)])


def _retention_tables():
    c = RET_CHUNK
    log_gamma = jnp.log(1.0 - 2.0 ** (-5.0 - jnp.arange(RET_HEADS, dtype=F32)))
    pos = jnp.arange(c, dtype=F32)
    diff = pos[:, None] - pos[None, :]
    decay = jnp.where(diff >= 0, jnp.exp(log_gamma[:, None, None] * diff), 0.0)
    xi = jnp.exp(log_gamma[:, None] * (pos + 1.0))
    zeta = jnp.exp(log_gamma[:, None] * (c - 1.0 - pos))
    gamma_c = jnp.exp(log_gamma * c)
    bcast = lambda t: jnp.broadcast_to(t[..., None], t.shape + (LANES,))
    return decay, bcast(xi), bcast(zeta), bcast(gamma_c[:, None])


def _pad_mla_weights(w_q_up, w_kv_up):
    depth = w_q_up.shape[0]
    d_qk = MLA_NOPE + MLA_ROPE
    pad_lanes = lambda w: jnp.pad(w, ((0, 0), (0, 0), (0, 0), (0, LANES - w.shape[-1])))
    wq = pad_lanes(w_q_up.astype(BF16).reshape(depth, MLA_Q_RANK, MLA_HEADS, d_qk))
    wkv = w_kv_up.astype(BF16).reshape(depth, MLA_KV_RANK, MLA_HEADS, MLA_NOPE + MLA_V)
    wk = pad_lanes(wkv[..., :MLA_NOPE])
    wv = wkv[..., MLA_NOPE:]
    return (wq.reshape(depth, MLA_Q_RANK, MLA_HEADS * LANES), wk.reshape(depth, MLA_KV_RANK, MLA_HEADS * LANES),
            wv.reshape(depth, MLA_KV_RANK, MLA_HEADS * MLA_V))


def _tile_sizes(seq):
    pick = lambda want: want if seq % want == 0 else seq
    return dict(in_proj=pick(512), attn=pick(256), ret=pick(512),
                mlp=pick(512), mlp_ff=512)


def kernel(x, attn_norm, w_in, mla_q_norm, mla_kv_norm, mla_w_q_up, mla_w_kv_up, ret_gn_w, ret_gn_b,
           w_branch_mla, w_branch_moba, w_branch_ret, w_out, mlp_norm, w_mlp_up, w_mlp_down, final_norm):
    batch, seq, d = x.shape
    depth = w_in.shape[0]
    tiles = _tile_sizes(seq)
    mla_tab = _rope_tables(seq, MLA_ROPE, ROPE_THETA, LANES, MLA_NOPE)
    moba_tab = _rope_tables(seq, MOBA_HEAD_DIM // PARTIAL_ROPE_DIV, ROPE_THETA, MOBA_HEAD_DIM, 0)
    ret_tab = _rope_tables(seq, RET_QK_DIM, RET_THETA, RET_QK_DIM, 0)
    decay, xi, zeta, gamma_c = _retention_tables()
    rows = lambda v: v[:, None, :]
    bf16 = lambda w: w.astype(BF16)

    assert w_in.shape[1:] == (D_MODEL, IN_COLS)
    w_in_t = bf16(jnp.swapaxes(w_in, 1, 2))
    wq, wk, wv = _pad_mla_weights(mla_w_q_up, mla_w_kv_up)
    branch_w = [bf16(w) for w in (w_branch_mla, w_branch_moba, w_branch_ret, w_out)]
    mlp_w = [rows(mlp_norm), bf16(w_mlp_up), bf16(w_mlp_down)]
    x2 = x.reshape(batch * seq, d)
    for l in range(depth):
        p2, q_mla, k_mla, v_mla = _in_proj(l, x2, rows(attn_norm), w_in_t, moba_tab, ret_tab, mla_tab,
                                           rows(mla_q_norm), rows(mla_kv_norm), wq, wk, wv,
                                           seq, tiles["in_proj"])
        p3 = p2.reshape(batch, seq, IN_COLS_PADDED)
        o_mla = _mla_attn(q_mla, k_mla, v_mla, tiles["attn"])
        o_moba = _moba_attn(p3)
        o_ret = _retention(l, p3, decay, xi, zeta, gamma_c, rows(ret_gn_w), rows(ret_gn_b), tiles["ret"])
        flat = lambda o: o.reshape(batch * seq, GROUP)
        x2 = _merge_mlp(l, x2, p2, flat(o_mla), flat(o_moba), flat(o_ret), *branch_w, *mlp_w,
                        final_norm.reshape(1, -1), tiles["mlp"], tiles["mlp_ff"], final_norm=(l == depth - 1))
    return x2.reshape(batch, seq, d)
```

```python
import functools
import math

import jax
import jax.numpy as jnp
from jax import lax
from jax.experimental import pallas as pl
from jax.experimental.pallas import tpu as pltpu

F32 = jnp.float32
BF16 = jnp.bfloat16

D_MODEL = 1024
MLA_HEADS = 8
MLA_Q_RANK = 256
MLA_KV_RANK = 128
MLA_NOPE = 64
MLA_ROPE = 32
MLA_V = 64
MOBA_HEADS = 8
MOBA_HEAD_DIM = 64
MOBA_BLOCK = 256
MOBA_TOPK = 3
RET_HEADS = 4
RET_QK_DIM = 64
RET_V_DIM = 128
RET_CHUNK = 128
RET_THETA = 10000.0
ROPE_THETA = 500000.0
PARTIAL_ROPE_DIV = 4
D_FF = 4 * D_MODEL
NORM_EPS = 1e-6
GN_EPS = 1e-5
NEG_INF = -1e30
LOG2_E = 1.4426950408889634
ROPE_SPLIT = 64

LANES = 128
SUBLANES = 8
BF16_ROWS = 16
HEAD_PAIR = 2
SCORE_DEPTH = 3
TILE_PAIRS = 4
VMEM_LIMIT = 48 * 1024 * 1024

GROUP = 512
G_GATES = 0
G_MOBA_Q, G_MOBA_K, G_MOBA_V = 6, 7, 8
G_RET_QK, G_RET_V, G_RET_G = 9, 10, 11
N_GROUPS = 12
IN_COLS_PADDED = N_GROUPS * GROUP
W_KR = MLA_Q_RANK + MLA_KV_RANK
W_MIXERS = W_KR + MLA_ROPE
W_GATES = W_MIXERS + (G_RET_G + 1 - G_MOBA_Q) * GROUP
IN_COLS = W_GATES + 3 * D_MODEL


def _nt_dot(a, b):
    return lax.dot_general(a, b, (((1,), (1,)), ((), ())), preferred_element_type=F32)


def _rms(x, g):
    return x * lax.rsqrt(jnp.mean(x * x, axis=-1, keepdims=True) + NORM_EPS) * g


def _rope_lanes(y, tab_ref, half):
    return (y * tab_ref[0]
            + pltpu.roll(y, LANES - half, 1) * tab_ref[1]
            + pltpu.roll(y, half, 1) * tab_ref[2])


def _mla_heads(latent, kr, tab, qg_ref, kvg_ref, wq_ref, wk_ref, wv_ref, q_ref, k_ref, v_ref):
    cq = latent[:, :MLA_Q_RANK]
    ckv = latent[:, MLA_Q_RANK:]
    cqn = _rms(cq, qg_ref[...]).astype(BF16)
    ckvn = _rms(ckv, kvg_ref[...]).astype(BF16)
    q = jnp.dot(cqn, wq_ref[...], preferred_element_type=F32)
    k = jnp.dot(ckvn, wk_ref[...], preferred_element_type=F32)
    v = jnp.dot(ckvn, wv_ref[...], preferred_element_type=F32)
    kpe = _rope_lanes(kr, tab, MLA_ROPE // 2)
    scale = (MLA_NOPE + MLA_ROPE) ** -0.5 * LOG2_E
    for h in range(MLA_HEADS):
        sl = slice(h * LANES, (h + 1) * LANES)
        q_ref[0, h] = (_rope_lanes(q[:, sl], tab, MLA_ROPE // 2) * scale).astype(BF16)
        k_ref[0, h] = (k[:, sl] + kpe).astype(BF16)
    v_ref[0] = v.astype(BF16)


def _in_proj_kernel(x_ref, g_ref, wt_ref, moba_tab, ret_tab, mla_tab, qg_ref, kvg_ref, wq_ref, wk_ref, wv_ref,
                    p_ref, q_ref, k_ref, v_ref, h_ref, kr_ref):
    @pl.when(pl.program_id(0) == 0)
    def _():
        kr_ref[...] = jnp.zeros_like(kr_ref)
        kr_ref[MLA_NOPE:MLA_NOPE + MLA_ROPE, :] = wt_ref[W_KR:W_MIXERS, :]

    h_ref[...] = _rms(x_ref[...], g_ref[...]).astype(BF16)
    _mla_heads(_nt_dot(h_ref[...], wt_ref[:W_KR, :]), _nt_dot(h_ref[...], kr_ref[...]),
               mla_tab, qg_ref, kvg_ref, wq_ref, wk_ref, wv_ref, q_ref, k_ref, v_ref)
    lane_tiles = GROUP // LANES
    for j in range(N_GROUPS):
        first = W_GATES + j * GROUP if j < G_MOBA_Q else W_MIXERS + (j - G_MOBA_Q) * GROUP
        acc = _nt_dot(h_ref[...], wt_ref[first:first + GROUP, :])
        for r in range(lane_tiles):
            y = acc[:, r * LANES:(r + 1) * LANES]
            if j in (G_MOBA_Q, G_MOBA_K):
                y = _rope_lanes(y, moba_tab, MOBA_HEAD_DIM // PARTIAL_ROPE_DIV // 2)
            elif j == G_RET_QK:
                y = _rope_lanes(y, ret_tab, RET_QK_DIM // 2)
                if r >= lane_tiles // 2:
                    y = y * (RET_QK_DIM ** -0.5)
            p_ref[:, j * GROUP + r * LANES:j * GROUP + (r + 1) * LANES] = y.astype(BF16)


def _layer_block(layer, shape, resident=False):
    mode = dict(pipeline_mode=pl.Buffered(1)) if resident else {}
    return pl.BlockSpec((None,) + shape, lambda *_: (layer,) + (0,) * len(shape), **mode)


def _in_proj(layer, x2, g, w, moba_tab, ret_tab, mla_tab, qg, kvg, wq, wk, wv, seq, tm):
    tokens = x2.shape[0]
    s_tiles = seq // tm
    batch = tokens // seq
    tab = pl.BlockSpec((3, tm, LANES), lambda i: (0, i % s_tiles, 0))
    const = lambda shape: _layer_block(layer, shape)
    heads = pl.BlockSpec((1, MLA_HEADS, tm, LANES), lambda i: (i // s_tiles, 0, i % s_tiles, 0))
    head_shape = jax.ShapeDtypeStruct((batch, MLA_HEADS, seq, LANES), BF16)
    return pl.pallas_call(
        _in_proj_kernel,
        grid=(tokens // tm,),
        in_specs=[
            pl.BlockSpec((tm, D_MODEL), lambda i: (i, 0)),
            const((1, D_MODEL)),
            _layer_block(layer, (IN_COLS, D_MODEL), resident=True),
            tab, tab, tab,
            const((1, MLA_Q_RANK)), const((1, MLA_KV_RANK)),
            const((MLA_Q_RANK, MLA_HEADS * LANES)), const((MLA_KV_RANK, MLA_HEADS * LANES)),
            const((MLA_KV_RANK, MLA_HEADS * MLA_V)),
        ],
        out_specs=[
            pl.BlockSpec((tm, IN_COLS_PADDED), lambda i: (i, 0)),
            heads, heads,
            pl.BlockSpec((1, tm, MLA_HEADS * MLA_V), lambda i: (i // s_tiles, i % s_tiles, 0)),
        ],
        out_shape=[jax.ShapeDtypeStruct((tokens, IN_COLS_PADDED), BF16), head_shape, head_shape,
                   jax.ShapeDtypeStruct((batch, seq, MLA_HEADS * MLA_V), BF16)],
        scratch_shapes=[pltpu.VMEM((tm, D_MODEL), BF16), pltpu.VMEM((LANES, D_MODEL), BF16)],
        compiler_params=pltpu.CompilerParams(
            dimension_semantics=("arbitrary",), vmem_limit_bytes=VMEM_LIMIT),
        name="in_proj",
    )(x2, g, w, moba_tab, ret_tab, mla_tab, qg, kvg, wq, wk, wv)


_HEADS = range(HEAD_PAIR)
_ROWS = range(2)


def _fill_v_transposed(v_ref, vt_ref, t, head_dim):
    sub = lax.broadcasted_iota(jnp.int32, (BF16_ROWS, t), 0)
    ones_block = jnp.where(sub == 0, 1.0, 0.0)
    for j in range(vt_ref.shape[1]):
        vt = v_ref[0, j * t:(j + 1) * t, :].astype(F32).T
        for h in _HEADS:
            own = vt[h * head_dim:(h + 1) * head_dim]
            vt_ref[h, j] = jnp.concatenate([own, ones_block], axis=0).astype(BF16)


def _fill_causal_cap(cap_ref, t):
    key = lax.broadcasted_iota(jnp.int32, (t, t), 0)
    query = lax.broadcasted_iota(jnp.int32, (t, t), 1)
    cap_ref[...] = jnp.where(key <= query, jnp.inf, NEG_INF)


def _attend_rows(g, n_tiles, t, load_qt, load_k, vt_ref, write_out, scratch, head_dim, row_drop=None):
    causal_cap_ref, bufs, m_ref, acc_ref = scratch
    depth = bufs.shape[0]
    q_tile = [g, n_tiles - 1 - g]
    m_ref[...] = jnp.full_like(m_ref, -jnp.inf)
    acc_ref[...] = jnp.zeros_like(acc_ref)

    def locate(n):
        row = (n > q_tile[0]).astype(jnp.int32)
        return row, jnp.where(row == 0, q_tile[0] - n, n - (q_tile[0] + 1))

    def issue_scores(n, slot):
        row, kj = locate(n)
        ks, qts = load_k(kj), load_qt(row)
        for h in _HEADS:
            bufs[slot, h] = jnp.dot(ks[h], qts[h], preferred_element_type=F32)

    def softmax_step(n, slot, diagonal=False):
        row, kj = locate(n)
        chain = [HEAD_PAIR * row + h for h in _HEADS]
        s = [bufs[slot, h] for h in _HEADS]
        if diagonal:
            s = [jnp.minimum(s[h], causal_cap_ref[...]) for h in _HEADS]
        tile_max = [jnp.max(s[h], axis=0, keepdims=True) for h in _HEADS]
        shift = None
        if row_drop is not None and not diagonal:
            shift = [row_drop(row, h, kj) for h in _HEADS]
            tile_max = [jnp.maximum(tile_max[h] - shift[h], NEG_INF) for h in _HEADS]
        m_old = [m_ref[chain[h]] for h in _HEADS]
        m_new = [jnp.maximum(m_old[h], tile_max[h]) for h in _HEADS]
        alpha = [jnp.exp2(m_old[h] - m_new[h]) for h in _HEADS]
        m_sub = m_new if shift is None else [m_new[h] + shift[h] for h in _HEADS]
        p = [jnp.exp2(s[h] - m_sub[h]).astype(BF16) for h in _HEADS]
        pv = [jnp.dot(vt_ref[h, kj], p[h], preferred_element_type=F32) for h in _HEADS]
        for h in _HEADS:
            m_ref[chain[h]] = m_new[h]
            acc_ref[chain[h]] = alpha[h] * acc_ref[chain[h]] + pv[h]

    steps = n_tiles + 1
    for n in range(min(depth, steps)):
        issue_scores(n, n)
    for n in range(steps):
        softmax_step(n, n % depth, diagonal=n in (0, n_tiles))
        if n + depth < steps:
            issue_scores(n + depth, n % depth)

    for row in _ROWS:
        outs = []
        for h in _HEADS:
            acc = acc_ref[HEAD_PAIR * row + h]
            outs.append(acc[:head_dim] * (1.0 / acc[head_dim:head_dim + 1]))
        ot = jnp.concatenate(outs, axis=0)
        write_out(q_tile[row], ot.T.astype(BF16))


def _attn_scratch(seq, t, head_dim, pairs):
    chains = HEAD_PAIR * len(_ROWS)
    v_rows = head_dim + BF16_ROWS
    return [pltpu.VMEM((HEAD_PAIR, seq // t, v_rows, t), BF16),
            pltpu.VMEM((t, t), F32),
            pltpu.VMEM((pairs, SCORE_DEPTH, HEAD_PAIR, t, t), F32),
            pltpu.VMEM((pairs, chains, 1, t), F32),
            pltpu.VMEM((pairs, chains, v_rows, t), F32)]


def _pairs_per_step(n_tiles, wanted):
    assert n_tiles % 2 == 0
    return math.gcd(wanted, n_tiles // 2)


def _pair_scratch(scratch, pair):
    causal_cap_ref, bufs, m_ref, acc_ref = scratch
    return causal_cap_ref, bufs.at[pair], m_ref.at[pair], acc_ref.at[pair]


def _tile_rows(i, t):
    return pl.ds(pl.multiple_of(i * t, t), t)


def _mla_attn_kernel(q_ref, k_ref, v_ref, o_ref, qt_ref, vt_ref, *scratch, t, n_tiles):
    step = pl.program_id(2)

    @pl.when(step == 0)
    def _():
        _fill_v_transposed(v_ref, vt_ref, t, MLA_V)
        _fill_causal_cap(scratch[0], t)

    def load_k(kj):
        return [k_ref[0, h, _tile_rows(kj, t), :] for h in _HEADS]

    def write_out(qt, tile):
        o_ref[0, _tile_rows(qt, t), :] = tile

    pairs = qt_ref.shape[0]
    pair_g = [pairs * step + pair for pair in range(pairs)]
    for pair, g in enumerate(pair_g):
        for row, qi in enumerate((g, n_tiles - 1 - g)):
            for h in _HEADS:
                qt_ref[pair, row, h] = q_ref[0, h, _tile_rows(qi, t), :].astype(F32).T.astype(BF16)
    for pair, g in enumerate(pair_g):
        load_qt = lambda row, pair=pair: [qt_ref[pair, row, h] for h in _HEADS]
        _attend_rows(g, n_tiles, t, load_qt, load_k, vt_ref, write_out, _pair_scratch(scratch, pair), MLA_V)


def _mla_attn(q, k, v, t):
    batch, heads, seq, _ = q.shape
    n_tiles = seq // t
    pairs = _pairs_per_step(n_tiles, TILE_PAIRS)
    return pl.pallas_call(
        functools.partial(_mla_attn_kernel, t=t, n_tiles=n_tiles),
        grid=(batch, heads // HEAD_PAIR, n_tiles // (2 * pairs)),
        in_specs=[
            pl.BlockSpec((1, HEAD_PAIR, seq, LANES), lambda b, h, g: (b, h, 0, 0)),
            pl.BlockSpec((1, HEAD_PAIR, seq, LANES), lambda b, h, g: (b, h, 0, 0)),
            pl.BlockSpec((1, seq, LANES), lambda b, h, g: (b, 0, h)),
        ],
        out_specs=pl.BlockSpec((1, seq, LANES), lambda b, h, g: (b, 0, h)),
        out_shape=jax.ShapeDtypeStruct((batch, seq, heads * MLA_V), BF16),
        scratch_shapes=[pltpu.VMEM((pairs, len(_ROWS), HEAD_PAIR, LANES, t), BF16)]
                       + _attn_scratch(seq, t, MLA_V, pairs),
        compiler_params=pltpu.CompilerParams(
            dimension_semantics=("parallel", "parallel", "arbitrary"), vmem_limit_bytes=VMEM_LIMIT),
        name="mla_attn",
    )(q, k, v)


def _moba_kernel(q_ref, k_ref, v_ref, o_ref, kmean_ref, qh_ref, drop_ref, vt_ref, *scratch, nb):
    t = MOBA_BLOCK
    step = pl.program_id(2)

    @pl.when(step == 0)
    def _():
        _fill_v_transposed(v_ref, vt_ref, t, MOBA_HEAD_DIM)
        _fill_causal_cap(scratch[0], t)
        kmean_ref[...] = jnp.zeros_like(kmean_ref)
        for j in range(nb):
            kb = k_ref[0, j * t:(j + 1) * t, :].astype(F32)
            kmean_ref[j:j + 1, :] = jnp.mean(kb, axis=0, keepdims=True)

    lane = lax.broadcasted_iota(jnp.int32, (t, LANES), 1)
    lane_k = lax.broadcasted_iota(jnp.int32, (LANES, LANES), 1)
    blk_id = lax.broadcasted_iota(jnp.int32, (nb, t), 0)
    nb_rows = -(-nb // SUBLANES) * SUBLANES
    pairs = qh_ref.shape[0]
    pair_g = [pairs * step + pair for pair in range(pairs)]
    for pair, g in enumerate(pair_g):
        for row, qi in enumerate((g, nb - 1 - g)):
            q_pair = q_ref[0, _tile_rows(qi, t), :]
            for hh in _HEADS:
                in_head = (lane >= hh * MOBA_HEAD_DIM) & (lane < (hh + 1) * MOBA_HEAD_DIM)
                qt_gate = jnp.where(in_head, q_pair, jnp.zeros_like(q_pair)).astype(F32).T
                qh_ref[pair, row, hh] = (qt_gate * (MOBA_HEAD_DIM ** -0.5 * LOG2_E)).astype(BF16)
                in_head_k = (lane_k >= hh * MOBA_HEAD_DIM) & (lane_k < (hh + 1) * MOBA_HEAD_DIM)
                km = jnp.where(in_head_k, kmean_ref[...], 0.0)[:nb_rows]
                gate = jnp.dot(km, qt_gate, precision=lax.Precision.HIGHEST, preferred_element_type=F32)[:nb]
                left = jnp.where(blk_id < qi, gate, NEG_INF)
                keep = blk_id == qi
                for r in range(MOBA_TOPK):
                    is_max = left == jnp.max(left, axis=0, keepdims=True)
                    pick = blk_id == jnp.min(jnp.where(is_max, blk_id, nb), axis=0, keepdims=True)
                    keep = keep | (pick & (qi > r))
                    left = jnp.where(pick, -jnp.inf, left)
                drop = jnp.where(keep, 0.0, jnp.inf)
                for j in range(nb):
                    drop_ref[pair, row, hh, j] = drop[j:j + 1, :]

    def load_k(kj):
        return [k_ref[0, _tile_rows(kj, t), :]] * HEAD_PAIR

    def write_out(qt, tile):
        o_ref[0, _tile_rows(qt, t), :] = tile

    for pair, g in enumerate(pair_g):
        load_qt = lambda row, pair=pair: [qh_ref[pair, row, h] for h in _HEADS]
        row_drop = lambda row, h, kj, pair=pair: drop_ref[pair, row, h, kj]
        _attend_rows(g, nb, t, load_qt, load_k, vt_ref, write_out, _pair_scratch(scratch, pair),
                     MOBA_HEAD_DIM, row_drop=row_drop)


def _moba_attn(p3):
    batch, seq, _ = p3.shape
    t = MOBA_BLOCK
    nb = seq // t
    pairs = _pairs_per_step(nb, TILE_PAIRS)
    per_group = GROUP // LANES
    return pl.pallas_call(
        functools.partial(_moba_kernel, nb=nb),
        grid=(batch, MOBA_HEADS // HEAD_PAIR, nb // (2 * pairs)),
        in_specs=[
            pl.BlockSpec((1, seq, LANES), lambda b, h, g: (b, 0, G_MOBA_Q * per_group + h)),
            pl.BlockSpec((1, seq, LANES), lambda b, h, g: (b, 0, G_MOBA_K * per_group + h)),
            pl.BlockSpec((1, seq, LANES), lambda b, h, g: (b, 0, G_MOBA_V * per_group + h)),
        ],
        out_specs=pl.BlockSpec((1, seq, LANES), lambda b, h, g: (b, 0, h)),
        out_shape=jax.ShapeDtypeStruct((batch, seq, MOBA_HEADS * MOBA_HEAD_DIM), BF16),
        scratch_shapes=[pltpu.VMEM((LANES, LANES), F32),
                        pltpu.VMEM((pairs, len(_ROWS), HEAD_PAIR, LANES, t), BF16),
                        pltpu.VMEM((pairs, len(_ROWS), HEAD_PAIR, nb, 1, t), F32)]
                       + _attn_scratch(seq, t, MOBA_HEAD_DIM, pairs),
        compiler_params=pltpu.CompilerParams(
            dimension_semantics=("parallel", "parallel", "arbitrary"), vmem_limit_bytes=VMEM_LIMIT),
        name="moba_attn",
    )(p3, p3, p3)


def _retention_kernel(qk_ref, v_ref, g_ref, decay_ref, xi_ref, zeta_ref, gamma_ref, gnw_ref, gnb_ref,
                      o_ref, state_ref, *, chunks):
    c = RET_CHUNK

    @pl.when(pl.program_id(1) == 0)
    def _():
        state_ref[...] = jnp.zeros_like(state_ref)

    lane = lax.broadcasted_iota(jnp.int32, (c, LANES), 1)
    k_off = RET_HEADS * RET_QK_DIM
    for n in range(chunks):
        rows = slice(n * c, (n + 1) * c)
        for h in range(RET_HEADS):
            pair, half = divmod(h, HEAD_PAIR)
            in_head = (lane >= half * RET_QK_DIM) & (lane < (half + 1) * RET_QK_DIM)
            q = qk_ref[0, rows, pair * LANES:(pair + 1) * LANES]
            k = jnp.where(in_head, qk_ref[0, rows, k_off + pair * LANES:k_off + (pair + 1) * LANES].astype(F32), 0.0)
            vs = slice(h * RET_V_DIM, (h + 1) * RET_V_DIM)
            v = v_ref[0, rows, vs]
            state = state_ref[h]
            scores = _nt_dot(q, k.astype(BF16)) * decay_ref[h]
            inner = jnp.dot(scores.astype(BF16), v, preferred_element_type=F32)
            cross = jnp.dot((q.astype(F32) * xi_ref[h]).astype(BF16), state.astype(BF16),
                            preferred_element_type=F32)
            kv = lax.dot_general((k * zeta_ref[h]).astype(BF16), v, (((0,), (0,)), ((), ())),
                                 preferred_element_type=F32)
            state_ref[h] = gamma_ref[h] * state + kv
            o = inner + cross
            mu = jnp.mean(o, axis=-1, keepdims=True)
            d = o - mu
            var = jnp.mean(d * d, axis=-1, keepdims=True)
            o = d * lax.rsqrt(var + GN_EPS) * gnw_ref[:, vs] + gnb_ref[:, vs]
            g = g_ref[0, rows, vs].astype(F32)
            o_ref[0, rows, vs] = (g * jax.nn.sigmoid(g) * o).astype(BF16)


def _retention(layer, p3, decay, xi, zeta, gamma, gnw, gnb, tr):
    batch, seq, _ = p3.shape
    width = RET_HEADS * RET_V_DIM
    const = lambda shape: pl.BlockSpec(shape, lambda b, i: (0,) * len(shape))
    return pl.pallas_call(
        functools.partial(_retention_kernel, chunks=tr // RET_CHUNK),
        grid=(batch, seq // tr),
        in_specs=[
            pl.BlockSpec((1, tr, GROUP), lambda b, i: (b, i, G_RET_QK)),
            pl.BlockSpec((1, tr, GROUP), lambda b, i: (b, i, G_RET_V)),
            pl.BlockSpec((1, tr, GROUP), lambda b, i: (b, i, G_RET_G)),
            const((RET_HEADS, RET_CHUNK, RET_CHUNK)),
            const((RET_HEADS, RET_CHUNK, LANES)), const((RET_HEADS, RET_CHUNK, LANES)),
            const((RET_HEADS, 1, LANES)),
            _layer_block(layer, (1, width)), _layer_block(layer, (1, width)),
        ],
        out_specs=pl.BlockSpec((1, tr, width), lambda b, i: (b, i, 0)),
        out_shape=jax.ShapeDtypeStruct((batch, seq, width), BF16),
        scratch_shapes=[pltpu.VMEM((RET_HEADS, LANES, RET_V_DIM), F32)],
        compiler_params=pltpu.CompilerParams(
            dimension_semantics=("parallel", "arbitrary"), vmem_limit_bytes=VMEM_LIMIT),
        name="retention",
    )(p3, p3, p3, decay, xi, zeta, gamma, gnw, gnb)


def _merge_mlp_kernel(x_ref, gates_ref, oa_ref, ob_ref, oc_ref, wa_ref, wb_ref, wc_ref, wo_ref,
                      g_ref, wu_ref, wd_ref, fg_ref, o_ref, h_ref, *, tf, final_norm):
    merged = None
    for idx, (o_in, w) in enumerate(((oa_ref, wa_ref), (ob_ref, wb_ref), (oc_ref, wc_ref))):
        gate = gates_ref[:, idx * D_MODEL:(idx + 1) * D_MODEL].astype(F32)
        term = jax.nn.sigmoid(gate) * jnp.dot(o_in[...], w[...], preferred_element_type=F32)
        merged = term if merged is None else merged + term
    x = x_ref[...] + jnp.dot(merged.astype(BF16), wo_ref[...], preferred_element_type=F32)

    h_ref[...] = _rms(x, g_ref[...]).astype(BF16)
    acc = x
    for f in range(D_FF // tf):
        cols = slice(f * tf, (f + 1) * tf)
        u = jnp.maximum(jnp.dot(h_ref[...], wu_ref[:, cols], preferred_element_type=F32), 0.0)
        acc = acc + jnp.dot((u * u).astype(BF16), wd_ref[cols, :], preferred_element_type=F32)
    o_ref[...] = _rms(acc, fg_ref[...]) if final_norm else acc


def _merge_mlp(layer, x2, p2, oa, ob, oc, wa, wb, wc, wo, g, wu, wd, fg, tm, tf, final_norm):
    tokens = x2.shape[0]
    row = lambda w: pl.BlockSpec((tm, w), lambda i: (i, 0))
    resident = lambda shape: _layer_block(layer, shape, resident=True)
    return pl.pallas_call(
        functools.partial(_merge_mlp_kernel, tf=tf, final_norm=final_norm),
        grid=(tokens // tm,),
        in_specs=[row(D_MODEL), row(3 * D_MODEL), row(GROUP), row(GROUP), row(GROUP),
                  resident((GROUP, D_MODEL)), resident((GROUP, D_MODEL)), resident((GROUP, D_MODEL)),
                  resident((D_MODEL, D_MODEL)),
                  _layer_block(layer, (1, D_MODEL)), resident((D_MODEL, D_FF)), resident((D_FF, D_MODEL)),
                  pl.BlockSpec((1, D_MODEL), lambda i: (0, 0))],
        out_specs=row(D_MODEL),
        out_shape=jax.ShapeDtypeStruct((tokens, D_MODEL), F32),
        scratch_shapes=[pltpu.VMEM((tm, D_MODEL), BF16)],
        compiler_params=pltpu.CompilerParams(
            dimension_semantics=("parallel",), vmem_limit_bytes=VMEM_LIMIT),
        name="merge_mlp",
    )(x2, p2, oa, ob, oc, wa, wb, wc, wo, g, wu, wd, fg)


def _rope_tables(seq, dim, theta, period, offset):
    half = dim // 2
    lane = jnp.arange(LANES) % period - offset
    in_lo, in_hi = (lane >= 0) & (lane < half), (lane >= half) & (lane < dim)
    freq = jnp.clip(jnp.where(in_hi, lane - half, lane), 0, half - 1).astype(F32)
    inv = 1.0 / (theta ** (2.0 * freq / dim))
    split = math.gcd(ROPE_SPLIT, seq)
    lo = jnp.arange(split, dtype=F32)[None, :, None] * inv
    hi = (jnp.arange(seq // split, dtype=F32) * split)[:, None, None] * inv
    cos = (jnp.cos(hi) * jnp.cos(lo) - jnp.sin(hi) * jnp.sin(lo)).reshape(seq, LANES)
    sin = (jnp.sin(hi) * jnp.cos(lo) + jnp.cos(hi) * jnp.sin(lo)).reshape(seq, LANES)
    return jnp.stack([jnp.where(in_lo | in_hi, cos, 1.0), jnp.where(in_lo, -sin, 0.0),
                      jnp.where(in_hi, sin, 0.0)])


def _retention_tables():
    c = RET_CHUNK
    log_gamma = jnp.log(1.0 - 2.0 ** (-5.0 - jnp.arange(RET_HEADS, dtype=F32)))
    pos = jnp.arange(c, dtype=F32)
    diff = pos[:, None] - pos[None, :]
    decay = jnp.where(diff >= 0, jnp.exp(log_gamma[:, None, None] * diff), 0.0)
    xi = jnp.exp(log_gamma[:, None] * (pos + 1.0))
    zeta = jnp.exp(log_gamma[:, None] * (c - 1.0 - pos))
    gamma_c = jnp.exp(log_gamma * c)
    bcast = lambda t: jnp.broadcast_to(t[..., None], t.shape + (LANES,))
    return decay, bcast(xi), bcast(zeta), bcast(gamma_c[:, None])


def _pad_mla_weights(w_q_up, w_kv_up):
    depth = w_q_up.shape[0]
    d_qk = MLA_NOPE + MLA_ROPE
    pad_lanes = lambda w: jnp.pad(w, ((0, 0), (0, 0), (0, 0), (0, LANES - w.shape[-1])))
    wq = pad_lanes(w_q_up.astype(BF16).reshape(depth, MLA_Q_RANK, MLA_HEADS, d_qk))
    wkv = w_kv_up.astype(BF16).reshape(depth, MLA_KV_RANK, MLA_HEADS, MLA_NOPE + MLA_V)
    wk = pad_lanes(wkv[..., :MLA_NOPE])
    wv = wkv[..., MLA_NOPE:]
    return (wq.reshape(depth, MLA_Q_RANK, MLA_HEADS * LANES), wk.reshape(depth, MLA_KV_RANK, MLA_HEADS * LANES),
            wv.reshape(depth, MLA_KV_RANK, MLA_HEADS * MLA_V))


def _tile_sizes(seq):
    pick = lambda want: want if seq % want == 0 else seq
    return dict(in_proj=pick(512), attn=pick(256), ret=pick(512),
                mlp=pick(512), mlp_ff=512)


def kernel(x, attn_norm, w_in, mla_q_norm, mla_kv_norm, mla_w_q_up, mla_w_kv_up, ret_gn_w, ret_gn_b,
           w_branch_mla, w_branch_moba, w_branch_ret, w_out, mlp_norm, w_mlp_up, w_mlp_down, final_norm):
    batch, seq, d = x.shape
    depth = w_in.shape[0]
    tiles = _tile_sizes(seq)
    mla_tab = _rope_tables(seq, MLA_ROPE, ROPE_THETA, LANES, MLA_NOPE)
    moba_tab = _rope_tables(seq, MOBA_HEAD_DIM // PARTIAL_ROPE_DIV, ROPE_THETA, MOBA_HEAD_DIM, 0)
    ret_tab = _rope_tables(seq, RET_QK_DIM, RET_THETA, RET_QK_DIM, 0)
    decay, xi, zeta, gamma_c = _retention_tables()
    rows = lambda v: v[:, None, :]
    bf16 = lambda w: w.astype(BF16)

    assert w_in.shape[1:] == (D_MODEL, IN_COLS)
    w_in_t = bf16(jnp.swapaxes(w_in, 1, 2))
    wq, wk, wv = _pad_mla_weights(mla_w_q_up, mla_w_kv_up)
    branch_w = [bf16(w) for w in (w_branch_mla, w_branch_moba, w_branch_ret, w_out)]
    mlp_w = [rows(mlp_norm), bf16(w_mlp_up), bf16(w_mlp_down)]
    x2 = x.reshape(batch * seq, d)
    for l in range(depth):
        p2, q_mla, k_mla, v_mla = _in_proj(l, x2, rows(attn_norm), w_in_t, moba_tab, ret_tab, mla_tab,
                                           rows(mla_q_norm), rows(mla_kv_norm), wq, wk, wv,
                                           seq, tiles["in_proj"])
        p3 = p2.reshape(batch, seq, IN_COLS_PADDED)
        o_mla = _mla_attn(q_mla, k_mla, v_mla, tiles["attn"])
        o_moba = _moba_attn(p3)
        o_ret = _retention(l, p3, decay, xi, zeta, gamma_c, rows(ret_gn_w), rows(ret_gn_b), tiles["ret"])
        flat = lambda o: o.reshape(batch * seq, GROUP)
        x2 = _merge_mlp(l, x2, p2, flat(o_mla), flat(o_moba), flat(o_ret), *branch_w, *mlp_w,
                        final_norm.reshape(1, -1), tiles["mlp"], tiles["mlp_ff"], final_norm=(l == depth - 1))
    return x2.reshape(batch, seq, d)
```

```python
import functools
import math

import jax
import jax.numpy as jnp
from jax import lax
from jax.experimental import pallas as pl
from jax.experimental.pallas import tpu as pltpu

F32 = jnp.float32
BF16 = jnp.bfloat16

D_MODEL = 1024
MLA_HEADS = 8
MLA_Q_RANK = 256
MLA_KV_RANK = 128
MLA_NOPE = 64
MLA_ROPE = 32
MLA_V = 64
MOBA_HEADS = 8
MOBA_HEAD_DIM = 64
MOBA_BLOCK = 256
MOBA_TOPK = 3
RET_HEADS = 4
RET_QK_DIM = 64
RET_V_DIM = 128
RET_CHUNK = 128
RET_THETA = 10000.0
ROPE_THETA = 500000.0
PARTIAL_ROPE_DIV = 4
D_FF = 4 * D_MODEL
NORM_EPS = 1e-6
GN_EPS = 1e-5
NEG_INF = -1e30
LOG2_E = 1.4426950408889634
ROPE_SPLIT = 64

LANES = 128
SUBLANES = 8
BF16_ROWS = 16
HEAD_PAIR = 2
SCORE_DEPTH = 3
TILE_PAIRS = 4
VMEM_LIMIT = 48 * 1024 * 1024

GROUP = 512
G_GATES = 0
G_MOBA_Q, G_MOBA_K, G_MOBA_V = 6, 7, 8
G_RET_QK, G_RET_V, G_RET_G = 9, 10, 11
N_GROUPS = 12
IN_COLS_PADDED = N_GROUPS * GROUP
W_KR = MLA_Q_RANK + MLA_KV_RANK
W_MIXERS = W_KR + MLA_ROPE
W_GATES = W_MIXERS + (G_RET_G + 1 - G_MOBA_Q) * GROUP
IN_COLS = W_GATES + 3 * D_MODEL


def _nt_dot(a, b):
    return lax.dot_general(a, b, (((1,), (1,)), ((), ())), preferred_element_type=F32)


def _rms(x, g):
    return x * lax.rsqrt(jnp.mean(x * x, axis=-1, keepdims=True) + NORM_EPS) * g


def _rope_lanes(y, tab_ref, half):
    return (y * tab_ref[0]
            + pltpu.roll(y, LANES - half, 1) * tab_ref[1]
            + pltpu.roll(y, half, 1) * tab_ref[2])


def _mla_heads(latent, kr, tab, qg_ref, kvg_ref, wq_ref, wk_ref, wv_ref, q_ref, k_ref, v_ref):
    cq = latent[:, :MLA_Q_RANK]
    ckv = latent[:, MLA_Q_RANK:]
    cqn = _rms(cq, qg_ref[...]).astype(BF16)
    ckvn = _rms(ckv, kvg_ref[...]).astype(BF16)
    q = jnp.dot(cqn, wq_ref[...], preferred_element_type=F32)
    k = jnp.dot(ckvn, wk_ref[...], preferred_element_type=F32)
    v = jnp.dot(ckvn, wv_ref[...], preferred_element_type=F32)
    kpe = _rope_lanes(kr, tab, MLA_ROPE // 2)
    scale = (MLA_NOPE + MLA_ROPE) ** -0.5 * LOG2_E
    for h in range(MLA_HEADS):
        sl = slice(h * LANES, (h + 1) * LANES)
        q_ref[0, h] = (_rope_lanes(q[:, sl], tab, MLA_ROPE // 2) * scale).astype(BF16)
        k_ref[0, h] = (k[:, sl] + kpe).astype(BF16)
    v_ref[0] = v.astype(BF16)


def _in_proj_kernel(x_ref, g_ref, wt_ref, moba_tab, ret_tab, mla_tab, qg_ref, kvg_ref, wq_ref, wk_ref, wv_ref,
                    p_ref, q_ref, k_ref, v_ref, h_ref, kr_ref):
    @pl.when(pl.program_id(0) == 0)
    def _():
        kr_ref[...] = jnp.zeros_like(kr_ref)
        kr_ref[MLA_NOPE:MLA_NOPE + MLA_ROPE, :] = wt_ref[W_KR:W_MIXERS, :]

    h_ref[...] = _rms(x_ref[...], g_ref[...]).astype(BF16)
    _mla_heads(_nt_dot(h_ref[...], wt_ref[:W_KR, :]), _nt_dot(h_ref[...], kr_ref[...]),
               mla_tab, qg_ref, kvg_ref, wq_ref, wk_ref, wv_ref, q_ref, k_ref, v_ref)
    lane_tiles = GROUP // LANES
    for j in range(N_GROUPS):
        first = W_GATES + j * GROUP if j < G_MOBA_Q else W_MIXERS + (j - G_MOBA_Q) * GROUP
        acc = _nt_dot(h_ref[...], wt_ref[first:first + GROUP, :])
        for r in range(lane_tiles):
            y = acc[:, r * LANES:(r + 1) * LANES]
            if j in (G_MOBA_Q, G_MOBA_K):
                y = _rope_lanes(y, moba_tab, MOBA_HEAD_DIM // PARTIAL_ROPE_DIV // 2)
            elif j == G_RET_QK:
                y = _rope_lanes(y, ret_tab, RET_QK_DIM // 2)
                if r >= lane_tiles // 2:
                    y = y * (RET_QK_DIM ** -0.5)
            p_ref[:, j * GROUP + r * LANES:j * GROUP + (r + 1) * LANES] = y.astype(BF16)


def _layer_block(layer, shape, resident=False):
    mode = dict(pipeline_mode=pl.Buffered(1)) if resident else {}
    return pl.BlockSpec((None,) + shape, lambda *_: (layer,) + (0,) * len(shape), **mode)


def _in_proj(layer, x2, g, w, moba_tab, ret_tab, mla_tab, qg, kvg, wq, wk, wv, seq, tm):
    tokens = x2.shape[0]
    s_tiles = seq // tm
    batch = tokens // seq
    tab = pl.BlockSpec((3, tm, LANES), lambda i: (0, i % s_tiles, 0))
    const = lambda shape: _layer_block(layer, shape)
    heads = pl.BlockSpec((1, MLA_HEADS, tm, LANES), lambda i: (i // s_tiles, 0, i % s_tiles, 0))
    head_shape = jax.ShapeDtypeStruct((batch, MLA_HEADS, seq, LANES), BF16)
    return pl.pallas_call(
        _in_proj_kernel,
        grid=(tokens // tm,),
        in_specs=[
            pl.BlockSpec((tm, D_MODEL), lambda i: (i, 0)),
            const((1, D_MODEL)),
            _layer_block(layer, (IN_COLS, D_MODEL), resident=True),
            tab, tab, tab,
            const((1, MLA_Q_RANK)), const((1, MLA_KV_RANK)),
            const((MLA_Q_RANK, MLA_HEADS * LANES)), const((MLA_KV_RANK, MLA_HEADS * LANES)),
            const((MLA_KV_RANK, MLA_HEADS * MLA_V)),
        ],
        out_specs=[
            pl.BlockSpec((tm, IN_COLS_PADDED), lambda i: (i, 0)),
            heads, heads,
            pl.BlockSpec((1, tm, MLA_HEADS * MLA_V), lambda i: (i // s_tiles, i % s_tiles, 0)),
        ],
        out_shape=[jax.ShapeDtypeStruct((tokens, IN_COLS_PADDED), BF16), head_shape, head_shape,
                   jax.ShapeDtypeStruct((batch, seq, MLA_HEADS * MLA_V), BF16)],
        scratch_shapes=[pltpu.VMEM((tm, D_MODEL), BF16), pltpu.VMEM((LANES, D_MODEL), BF16)],
        compiler_params=pltpu.CompilerParams(
            dimension_semantics=("arbitrary",), vmem_limit_bytes=VMEM_LIMIT),
        name="in_proj",
    )(x2, g, w, moba_tab, ret_tab, mla_tab, qg, kvg, wq, wk, wv)


_HEADS = range(HEAD_PAIR)
_ROWS = range(2)


def _fill_v_transposed(v_ref, vt_ref, t, head_dim):
    sub = lax.broadcasted_iota(jnp.int32, (BF16_ROWS, t), 0)
    ones_block = jnp.where(sub == 0, 1.0, 0.0)
    for j in range(vt_ref.shape[1]):
        vt = v_ref[0, j * t:(j + 1) * t, :].astype(F32).T
        for h in _HEADS:
            own = vt[h * head_dim:(h + 1) * head_dim]
            vt_ref[h, j] = jnp.concatenate([own, ones_block], axis=0).astype(BF16)


def _fill_causal_cap(cap_ref, t):
    key = lax.broadcasted_iota(jnp.int32, (t, t), 0)
    query = lax.broadcasted_iota(jnp.int32, (t, t), 1)
    cap_ref[...] = jnp.where(key <= query, jnp.inf, NEG_INF)


def _attend_rows(g, n_tiles, t, load_qt, load_k, vt_ref, write_out, scratch, head_dim, row_drop=None):
    causal_cap_ref, bufs, m_ref, acc_ref = scratch
    depth = bufs.shape[0]
    q_tile = [g, n_tiles - 1 - g]
    m_ref[...] = jnp.full_like(m_ref, -jnp.inf)
    acc_ref[...] = jnp.zeros_like(acc_ref)

    def locate(n):
        row = (n > g).astype(jnp.int32)
        return row, jnp.where(row == 0, g - n, n - (g + 1))

    def issue_scores(n, slot):
        row, kj = locate(n)
        ks, qts = load_k(kj), load_qt(row)
        for h in _HEADS:
            bufs[slot, h] = jnp.dot(ks[h], qts[h], preferred_element_type=F32)

    def softmax_step(n, slot, diagonal=False):
        row, kj = locate(n)
        chain = [HEAD_PAIR * row + h for h in _HEADS]
        s = [bufs[slot, h] for h in _HEADS]
        if diagonal:
            s = [jnp.minimum(s[h], causal_cap_ref[...]) for h in _HEADS]
        tile_max = [jnp.max(s[h], axis=0, keepdims=True) for h in _HEADS]
        shift = None
        if row_drop is not None and not diagonal:
            shift = [row_drop(row, h, kj) for h in _HEADS]
            tile_max = [jnp.maximum(tile_max[h] - shift[h], NEG_INF) for h in _HEADS]
        m_old = [m_ref[chain[h]] for h in _HEADS]
        m_new = [jnp.maximum(m_old[h], tile_max[h]) for h in _HEADS]
        alpha = [jnp.exp2(m_old[h] - m_new[h]) for h in _HEADS]
        m_sub = m_new if shift is None else [m_new[h] + shift[h] for h in _HEADS]
        p = [jnp.exp2(s[h] - m_sub[h]).astype(BF16) for h in _HEADS]
        pv = [jnp.dot(vt_ref[h, kj], p[h], preferred_element_type=F32) for h in _HEADS]
        for h in _HEADS:
            m_ref[chain[h]] = m_new[h]
            acc_ref[chain[h]] = alpha[h] * acc_ref[chain[h]] + pv[h]

    steps = n_tiles + 1
    for n in range(min(depth, steps)):
        issue_scores(n, n)
    for n in range(steps):
        softmax_step(n, n % depth, diagonal=n in (0, n_tiles))
        if n + depth < steps:
            issue_scores(n + depth, n % depth)

    for row in _ROWS:
        outs = []
        for h in _HEADS:
            acc = acc_ref[HEAD_PAIR * row + h]
            outs.append(acc[:head_dim] * (1.0 / acc[head_dim:head_dim + 1]))
        ot = jnp.concatenate(outs, axis=0)
        write_out(q_tile[row], ot.T.astype(BF16))


def _attn_scratch(seq, t, head_dim, pairs):
    chains = HEAD_PAIR * len(_ROWS)
    v_rows = head_dim + BF16_ROWS
    return [pltpu.VMEM((HEAD_PAIR, seq // t, v_rows, t), BF16),
            pltpu.VMEM((t, t), F32),
            pltpu.VMEM((pairs, SCORE_DEPTH, HEAD_PAIR, t, t), F32),
            pltpu.VMEM((pairs, chains, 1, t), F32),
            pltpu.VMEM((pairs, chains, v_rows, t), F32)]


def _pairs_per_step(n_tiles, wanted):
    assert n_tiles % 2 == 0
    return math.gcd(wanted, n_tiles // 2)


def _pair_scratch(scratch, pair):
    causal_cap_ref, bufs, m_ref, acc_ref = scratch
    return causal_cap_ref, bufs.at[pair], m_ref.at[pair], acc_ref.at[pair]


def _tile_rows(i, t):
    return pl.ds(pl.multiple_of(i * t, t), t)


def _step_pairs(pairs):
    return [pairs * pl.program_id(2) + pair for pair in range(pairs)]


def _mla_attn_kernel(q_ref, k_ref, v_ref, o_ref, qt_ref, vt_ref, *scratch, t, n_tiles):
    @pl.when(pl.program_id(2) == 0)
    def _():
        _fill_v_transposed(v_ref, vt_ref, t, MLA_V)
        _fill_causal_cap(scratch[0], t)

    def load_k(kj):
        return [k_ref[0, h, _tile_rows(kj, t), :] for h in _HEADS]

    def write_out(qt, tile):
        o_ref[0, _tile_rows(qt, t), :] = tile

    pairs = qt_ref.shape[0]
    pair_g = _step_pairs(pairs)
    def prepare(pair):
        g = pair_g[pair]
        for row, qi in enumerate((g, n_tiles - 1 - g)):
            for h in _HEADS:
                qt_ref[pair, row, h] = q_ref[0, h, _tile_rows(qi, t), :].astype(F32).T.astype(BF16)

    prepare(0)
    for pair, g in enumerate(pair_g):
        if pair + 1 < pairs:
            prepare(pair + 1)
        load_qt = lambda row, pair=pair: [qt_ref[pair, row, h] for h in _HEADS]
        _attend_rows(g, n_tiles, t, load_qt, load_k, vt_ref, write_out, _pair_scratch(scratch, pair), MLA_V)


def _mla_attn(q, k, v, t):
    batch, heads, seq, _ = q.shape
    n_tiles = seq // t
    pairs = _pairs_per_step(n_tiles, TILE_PAIRS)
    return pl.pallas_call(
        functools.partial(_mla_attn_kernel, t=t, n_tiles=n_tiles),
        grid=(batch, heads // HEAD_PAIR, n_tiles // (2 * pairs)),
        in_specs=[
            pl.BlockSpec((1, HEAD_PAIR, seq, LANES), lambda b, h, g: (b, h, 0, 0)),
            pl.BlockSpec((1, HEAD_PAIR, seq, LANES), lambda b, h, g: (b, h, 0, 0)),
            pl.BlockSpec((1, seq, LANES), lambda b, h, g: (b, 0, h)),
        ],
        out_specs=pl.BlockSpec((1, seq, LANES), lambda b, h, g: (b, 0, h)),
        out_shape=jax.ShapeDtypeStruct((batch, seq, heads * MLA_V), BF16),
        scratch_shapes=[pltpu.VMEM((pairs, len(_ROWS), HEAD_PAIR, LANES, t), BF16)]
                       + _attn_scratch(seq, t, MLA_V, pairs),
        compiler_params=pltpu.CompilerParams(
            dimension_semantics=("parallel", "parallel", "arbitrary"), vmem_limit_bytes=VMEM_LIMIT),
        name="mla_attn",
    )(q, k, v)


def _moba_kernel(q_ref, k_ref, v_ref, o_ref, kmean_ref, km3_ref, qh_ref, drop_ref, vt_ref, *scratch, nb):
    t = MOBA_BLOCK
    part = km3_ref.shape[0] // (3 * HEAD_PAIR)

    @pl.when(pl.program_id(2) == 0)
    def _():
        _fill_v_transposed(v_ref, vt_ref, t, MOBA_HEAD_DIM)
        _fill_causal_cap(scratch[0], t)
        kmean_ref[...] = jnp.zeros_like(kmean_ref)
        for j in range(nb):
            kb = k_ref[0, j * t:(j + 1) * t, :].astype(F32)
            kmean_ref[j:j + 1, :] = jnp.mean(kb, axis=0, keepdims=True)
        lane_k = lax.broadcasted_iota(jnp.int32, (part, LANES), 1)
        for hh in _HEADS:
            in_head_k = (lane_k >= hh * MOBA_HEAD_DIM) & (lane_k < (hh + 1) * MOBA_HEAD_DIM)
            rest = jnp.where(in_head_k, kmean_ref[:part], 0.0)
            for s in range(3):
                piece = rest.astype(BF16)
                km3_ref[(3 * hh + s) * part:(3 * hh + s + 1) * part, :] = piece
                rest = rest - piece.astype(F32)

    sub = lax.broadcasted_iota(jnp.int32, (LANES, t), 0)
    blk_id = lax.broadcasted_iota(jnp.int32, (nb, t), 0)
    pairs = qh_ref.shape[0]
    pair_g = _step_pairs(pairs)

    def prepare(pair):
        g = pair_g[pair]
        for row, qi in enumerate((g, nb - 1 - g)):
            q_t = q_ref[0, _tile_rows(qi, t), :].astype(F32).T
            gates = jnp.dot(km3_ref[...], q_t.astype(BF16), preferred_element_type=F32)
            for hh in _HEADS:
                in_head = (sub >= hh * MOBA_HEAD_DIM) & (sub < (hh + 1) * MOBA_HEAD_DIM)
                qh_ref[pair, row, hh] = (jnp.where(in_head, q_t, 0.0)
                                         * (MOBA_HEAD_DIM ** -0.5 * LOG2_E)).astype(BF16)
                g3 = [gates[(3 * hh + s) * part:(3 * hh + s + 1) * part] for s in range(3)]
                gate = ((g3[0] + g3[1]) + g3[2])[:nb]
                left = jnp.where(blk_id < qi, gate, NEG_INF)
                keep = blk_id == qi
                for r in range(MOBA_TOPK):
                    is_max = left == jnp.max(left, axis=0, keepdims=True)
                    pick = blk_id == jnp.min(jnp.where(is_max, blk_id, nb), axis=0, keepdims=True)
                    keep = keep | (pick & (qi > r))
                    left = jnp.where(pick, -jnp.inf, left)
                drop = jnp.where(keep, 0.0, jnp.inf)
                for j in range(nb):
                    drop_ref[pair, row, hh, j] = drop[j:j + 1, :]

    def load_k(kj):
        return [k_ref[0, _tile_rows(kj, t), :]] * HEAD_PAIR

    def write_out(qt, tile):
        o_ref[0, _tile_rows(qt, t), :] = tile

    prepare(0)
    for pair, g in enumerate(pair_g):
        if pair + 1 < pairs:
            prepare(pair + 1)
        load_qt = lambda row, pair=pair: [qh_ref[pair, row, h] for h in _HEADS]
        row_drop = lambda row, h, kj, pair=pair: drop_ref[pair, row, h, kj]
        _attend_rows(g, nb, t, load_qt, load_k, vt_ref, write_out, _pair_scratch(scratch, pair),
                     MOBA_HEAD_DIM, row_drop=row_drop)


def _moba_attn(p3):
    batch, seq, _ = p3.shape
    t = MOBA_BLOCK
    nb = seq // t
    pairs = _pairs_per_step(nb, TILE_PAIRS)
    mean_rows = -(-nb // BF16_ROWS) * BF16_ROWS
    assert mean_rows <= LANES
    per_group = GROUP // LANES
    return pl.pallas_call(
        functools.partial(_moba_kernel, nb=nb),
        grid=(batch, MOBA_HEADS // HEAD_PAIR, nb // (2 * pairs)),
        in_specs=[
            pl.BlockSpec((1, seq, LANES), lambda b, h, g: (b, 0, G_MOBA_Q * per_group + h)),
            pl.BlockSpec((1, seq, LANES), lambda b, h, g: (b, 0, G_MOBA_K * per_group + h)),
            pl.BlockSpec((1, seq, LANES), lambda b, h, g: (b, 0, G_MOBA_V * per_group + h)),
        ],
        out_specs=pl.BlockSpec((1, seq, LANES), lambda b, h, g: (b, 0, h)),
        out_shape=jax.ShapeDtypeStruct((batch, seq, MOBA_HEADS * MOBA_HEAD_DIM), BF16),
        scratch_shapes=[pltpu.VMEM((LANES, LANES), F32),
                        pltpu.VMEM((3 * HEAD_PAIR * mean_rows, LANES), BF16),
                        pltpu.VMEM((pairs, len(_ROWS), HEAD_PAIR, LANES, t), BF16),
                        pltpu.VMEM((pairs, len(_ROWS), HEAD_PAIR, nb, 1, t), F32)]
                       + _attn_scratch(seq, t, MOBA_HEAD_DIM, pairs),
        compiler_params=pltpu.CompilerParams(
            dimension_semantics=("parallel", "parallel", "arbitrary"), vmem_limit_bytes=VMEM_LIMIT),
        name="moba_attn",
    )(p3, p3, p3)


def _retention_kernel(qk_ref, v_ref, g_ref, decay_ref, xi_ref, zeta_ref, gamma_ref, gnw_ref, gnb_ref,
                      o_ref, state_ref, *, chunks):
    c = RET_CHUNK

    @pl.when(pl.program_id(1) == 0)
    def _():
        state_ref[...] = jnp.zeros_like(state_ref)

    lane = lax.broadcasted_iota(jnp.int32, (c, LANES), 1)
    k_off = RET_HEADS * RET_QK_DIM
    for n in range(chunks):
        rows = slice(n * c, (n + 1) * c)
        for h in range(RET_HEADS):
            pair, half = divmod(h, HEAD_PAIR)
            in_head = (lane >= half * RET_QK_DIM) & (lane < (half + 1) * RET_QK_DIM)
            q = qk_ref[0, rows, pair * LANES:(pair + 1) * LANES]
            k = jnp.where(in_head, qk_ref[0, rows, k_off + pair * LANES:k_off + (pair + 1) * LANES].astype(F32), 0.0)
            vs = slice(h * RET_V_DIM, (h + 1) * RET_V_DIM)
            v = v_ref[0, rows, vs]
            state = state_ref[h]
            scores = _nt_dot(q, k.astype(BF16)) * decay_ref[h]
            inner = jnp.dot(scores.astype(BF16), v, preferred_element_type=F32)
            cross = jnp.dot((q.astype(F32) * xi_ref[h]).astype(BF16), state.astype(BF16),
                            preferred_element_type=F32)
            kv = lax.dot_general((k * zeta_ref[h]).astype(BF16), v, (((0,), (0,)), ((), ())),
                                 preferred_element_type=F32)
            state_ref[h] = gamma_ref[h] * state + kv
            o = inner + cross
            mu = jnp.mean(o, axis=-1, keepdims=True)
            d = o - mu
            var = jnp.mean(d * d, axis=-1, keepdims=True)
            o = d * lax.rsqrt(var + GN_EPS) * gnw_ref[:, vs] + gnb_ref[:, vs]
            g = g_ref[0, rows, vs].astype(F32)
            o_ref[0, rows, vs] = (g * jax.nn.sigmoid(g) * o).astype(BF16)


def _retention(layer, p3, decay, xi, zeta, gamma, gnw, gnb, tr):
    batch, seq, _ = p3.shape
    width = RET_HEADS * RET_V_DIM
    const = lambda shape: pl.BlockSpec(shape, lambda b, i: (0,) * len(shape))
    return pl.pallas_call(
        functools.partial(_retention_kernel, chunks=tr // RET_CHUNK),
        grid=(batch, seq // tr),
        in_specs=[
            pl.BlockSpec((1, tr, GROUP), lambda b, i: (b, i, G_RET_QK)),
            pl.BlockSpec((1, tr, GROUP), lambda b, i: (b, i, G_RET_V)),
            pl.BlockSpec((1, tr, GROUP), lambda b, i: (b, i, G_RET_G)),
            const((RET_HEADS, RET_CHUNK, RET_CHUNK)),
            const((RET_HEADS, RET_CHUNK, LANES)), const((RET_HEADS, RET_CHUNK, LANES)),
            const((RET_HEADS, 1, LANES)),
            _layer_block(layer, (1, width)), _layer_block(layer, (1, width)),
        ],
        out_specs=pl.BlockSpec((1, tr, width), lambda b, i: (b, i, 0)),
        out_shape=jax.ShapeDtypeStruct((batch, seq, width), BF16),
        scratch_shapes=[pltpu.VMEM((RET_HEADS, LANES, RET_V_DIM), F32)],
        compiler_params=pltpu.CompilerParams(
            dimension_semantics=("parallel", "arbitrary"), vmem_limit_bytes=VMEM_LIMIT),
        name="retention",
    )(p3, p3, p3, decay, xi, zeta, gamma, gnw, gnb)


def _merge_mlp_kernel(x_ref, gates_ref, oa_ref, ob_ref, oc_ref, wa_ref, wb_ref, wc_ref, wo_ref,
                      g_ref, wu_ref, wd_ref, fg_ref, o_ref, h_ref, *, tf, final_norm):
    merged = None
    for idx, (o_in, w) in enumerate(((oa_ref, wa_ref), (ob_ref, wb_ref), (oc_ref, wc_ref))):
        gate = gates_ref[:, idx * D_MODEL:(idx + 1) * D_MODEL].astype(F32)
        term = jax.nn.sigmoid(gate) * jnp.dot(o_in[...], w[...], preferred_element_type=F32)
        merged = term if merged is None else merged + term
    x = x_ref[...] + jnp.dot(merged.astype(BF16), wo_ref[...], preferred_element_type=F32)

    h_ref[...] = _rms(x, g_ref[...]).astype(BF16)
    acc = x
    for f in range(D_FF // tf):
        cols = slice(f * tf, (f + 1) * tf)
        u = jnp.maximum(jnp.dot(h_ref[...], wu_ref[:, cols], preferred_element_type=F32), 0.0)
        acc = acc + jnp.dot((u * u).astype(BF16), wd_ref[cols, :], preferred_element_type=F32)
    o_ref[...] = _rms(acc, fg_ref[...]) if final_norm else acc


def _merge_mlp(layer, x2, p2, oa, ob, oc, wa, wb, wc, wo, g, wu, wd, fg, tm, tf, final_norm):
    tokens = x2.shape[0]
    row = lambda w: pl.BlockSpec((tm, w), lambda i: (i, 0))
    resident = lambda shape: _layer_block(layer, shape, resident=True)
    return pl.pallas_call(
        functools.partial(_merge_mlp_kernel, tf=tf, final_norm=final_norm),
        grid=(tokens // tm,),
        in_specs=[row(D_MODEL), row(3 * D_MODEL), row(GROUP), row(GROUP), row(GROUP),
                  resident((GROUP, D_MODEL)), resident((GROUP, D_MODEL)), resident((GROUP, D_MODEL)),
                  resident((D_MODEL, D_MODEL)),
                  _layer_block(layer, (1, D_MODEL)), resident((D_MODEL, D_FF)), resident((D_FF, D_MODEL)),
                  pl.BlockSpec((1, D_MODEL), lambda i: (0, 0))],
        out_specs=row(D_MODEL),
        out_shape=jax.ShapeDtypeStruct((tokens, D_MODEL), F32),
        scratch_shapes=[pltpu.VMEM((tm, D_MODEL), BF16)],
        compiler_params=pltpu.CompilerParams(
            dimension_semantics=("parallel",), vmem_limit_bytes=VMEM_LIMIT),
        name="merge_mlp",
    )(x2, p2, oa, ob, oc, wa, wb, wc, wo, g, wu, wd, fg)


def _rope_tables(seq, dim, theta, period, offset):
    half = dim // 2
    lane = jnp.arange(LANES) % period - offset
    in_lo, in_hi = (lane >= 0) & (lane < half), (lane >= half) & (lane < dim)
    freq = jnp.clip(jnp.where(in_hi, lane - half, lane), 0, half - 1).astype(F32)
    inv = 1.0 / (theta ** (2.0 * freq / dim))
    split = math.gcd(ROPE_SPLIT, seq)
    lo = jnp.arange(split, dtype=F32)[None, :, None] * inv
    hi = (jnp.arange(seq // split, dtype=F32) * split)[:, None, None] * inv
    cos = (jnp.cos(hi) * jnp.cos(lo) - jnp.sin(hi) * jnp.sin(lo)).reshape(seq, LANES)
    sin = (jnp.sin(hi) * jnp.cos(lo) + jnp.cos(hi) * jnp.sin(lo)).reshape(seq, LANES)
    return jnp.stack([jnp.where(in_lo | in_hi, cos, 1.0), jnp.where(in_lo, -sin, 0.0),
                      jnp.where(in_hi, sin, 0.0)])


def _retention_tables():
    c = RET_CHUNK
    log_gamma = jnp.log(1.0 - 2.0 ** (-5.0 - jnp.arange(RET_HEADS, dtype=F32)))
    pos = jnp.arange(c, dtype=F32)
    diff = pos[:, None] - pos[None, :]
    decay = jnp.where(diff >= 0, jnp.exp(log_gamma[:, None, None] * diff), 0.0)
    xi = jnp.exp(log_gamma[:, None] * (pos + 1.0))
    zeta = jnp.exp(log_gamma[:, None] * (c - 1.0 - pos))
    gamma_c = jnp.exp(log_gamma * c)
    bcast = lambda t: jnp.broadcast_to(t[..., None], t.shape + (LANES,))
    return decay, bcast(xi), bcast(zeta), bcast(gamma_c[:, None])


def _pad_mla_weights(w_q_up, w_kv_up):
    depth = w_q_up.shape[0]
    d_qk = MLA_NOPE + MLA_ROPE
    pad_lanes = lambda w: jnp.pad(w, ((0, 0), (0, 0), (0, 0), (0, LANES - w.shape[-1])))
    wq = pad_lanes(w_q_up.astype(BF16).reshape(depth, MLA_Q_RANK, MLA_HEADS, d_qk))
    wkv = w_kv_up.astype(BF16).reshape(depth, MLA_KV_RANK, MLA_HEADS, MLA_NOPE + MLA_V)
    wk = pad_lanes(wkv[..., :MLA_NOPE])
    wv = wkv[..., MLA_NOPE:]
    return (wq.reshape(depth, MLA_Q_RANK, MLA_HEADS * LANES), wk.reshape(depth, MLA_KV_RANK, MLA_HEADS * LANES),
            wv.reshape(depth, MLA_KV_RANK, MLA_HEADS * MLA_V))


def _tile_sizes(seq):
    pick = lambda want: want if seq % want == 0 else seq
    return dict(in_proj=pick(512), attn=pick(256), ret=pick(512),
                mlp=pick(512), mlp_ff=512)


def kernel(x, attn_norm, w_in, mla_q_norm, mla_kv_norm, mla_w_q_up, mla_w_kv_up, ret_gn_w, ret_gn_b,
           w_branch_mla, w_branch_moba, w_branch_ret, w_out, mlp_norm, w_mlp_up, w_mlp_down, final_norm):
    batch, seq, d = x.shape
    depth = w_in.shape[0]
    tiles = _tile_sizes(seq)
    mla_tab = _rope_tables(seq, MLA_ROPE, ROPE_THETA, LANES, MLA_NOPE)
    moba_tab = _rope_tables(seq, MOBA_HEAD_DIM // PARTIAL_ROPE_DIV, ROPE_THETA, MOBA_HEAD_DIM, 0)
    ret_tab = _rope_tables(seq, RET_QK_DIM, RET_THETA, RET_QK_DIM, 0)
    decay, xi, zeta, gamma_c = _retention_tables()
    rows = lambda v: v[:, None, :]
    bf16 = lambda w: w.astype(BF16)

    assert w_in.shape[1:] == (D_MODEL, IN_COLS)
    w_in_t = bf16(jnp.swapaxes(w_in, 1, 2))
    wq, wk, wv = _pad_mla_weights(mla_w_q_up, mla_w_kv_up)
    branch_w = [bf16(w) for w in (w_branch_mla, w_branch_moba, w_branch_ret, w_out)]
    mlp_w = [rows(mlp_norm), bf16(w_mlp_up), bf16(w_mlp_down)]
    x2 = x.reshape(batch * seq, d)
    for l in range(depth):
        p2, q_mla, k_mla, v_mla = _in_proj(l, x2, rows(attn_norm), w_in_t, moba_tab, ret_tab, mla_tab,
                                           rows(mla_q_norm), rows(mla_kv_norm), wq, wk, wv,
                                           seq, tiles["in_proj"])
        p3 = p2.reshape(batch, seq, IN_COLS_PADDED)
        o_mla = _mla_attn(q_mla, k_mla, v_mla, tiles["attn"])
        o_moba = _moba_attn(p3)
        o_ret = _retention(l, p3, decay, xi, zeta, gamma_c, rows(ret_gn_w), rows(ret_gn_b), tiles["ret"])
        flat = lambda o: o.reshape(batch * seq, GROUP)
        x2 = _merge_mlp(l, x2, p2, flat(o_mla), flat(o_moba), flat(o_ret), *branch_w, *mlp_w,
                        final_norm.reshape(1, -1), tiles["mlp"], tiles["mlp_ff"], final_norm=(l == depth - 1))
    return x2.reshape(batch, seq, d)
```

```python
import functools
import math

import jax
import jax.numpy as jnp
from jax import lax
from jax.experimental import pallas as pl
from jax.experimental.pallas import tpu as pltpu

F32 = jnp.float32
BF16 = jnp.bfloat16

D_MODEL = 1024
MLA_HEADS = 8
MLA_Q_RANK = 256
MLA_KV_RANK = 128
MLA_NOPE = 64
MLA_ROPE = 32
MLA_V = 64
MOBA_HEADS = 8
MOBA_HEAD_DIM = 64
MOBA_BLOCK = 256
MOBA_TOPK = 3
RET_HEADS = 4
RET_QK_DIM = 64
RET_V_DIM = 128
RET_CHUNK = 128
RET_THETA = 10000.0
ROPE_THETA = 500000.0
PARTIAL_ROPE_DIV = 4
D_FF = 4 * D_MODEL
NORM_EPS = 1e-6
GN_EPS = 1e-5
NEG_INF = -1e30
LOG2_E = 1.4426950408889634
ROPE_SPLIT = 64

LANES = 128
SUBLANES = 8
BF16_ROWS = 16
HEAD_PAIR = 2
SCORE_DEPTH = 3
TILE_PAIRS = 4
VMEM_LIMIT = 48 * 1024 * 1024

GROUP = 512
G_GATES = 0
G_MOBA_Q, G_MOBA_K, G_MOBA_V = 6, 7, 8
G_RET_QK, G_RET_V, G_RET_G = 9, 10, 11
N_GROUPS = 12
IN_COLS_PADDED = N_GROUPS * GROUP
W_KR = MLA_Q_RANK + MLA_KV_RANK
W_MIXERS = W_KR + MLA_ROPE
W_GATES = W_MIXERS + (G_RET_G + 1 - G_MOBA_Q) * GROUP
IN_COLS = W_GATES + 3 * D_MODEL


def _nt_dot(a, b):
    return lax.dot_general(a, b, (((1,), (1,)), ((), ())), preferred_element_type=F32)


def _rms(x, g):
    return x * lax.rsqrt(jnp.mean(x * x, axis=-1, keepdims=True) + NORM_EPS) * g


def _rope_lanes(y, tab_ref, half):
    return (y * tab_ref[0]
            + pltpu.roll(y, LANES - half, 1) * tab_ref[1]
            + pltpu.roll(y, half, 1) * tab_ref[2])


def _mla_heads(latent, kr, tab, qg_ref, kvg_ref, wq_ref, wk_ref, wv_ref, q_ref, k_ref, v_ref):
    cq = latent[:, :MLA_Q_RANK]
    ckv = latent[:, MLA_Q_RANK:]
    cqn = _rms(cq, qg_ref[...]).astype(BF16)
    ckvn = _rms(ckv, kvg_ref[...]).astype(BF16)
    q = jnp.dot(cqn, wq_ref[...], preferred_element_type=F32)
    k = jnp.dot(ckvn, wk_ref[...], preferred_element_type=F32)
    v = jnp.dot(ckvn, wv_ref[...], preferred_element_type=F32)
    kpe = _rope_lanes(kr, tab, MLA_ROPE // 2)
    scale = (MLA_NOPE + MLA_ROPE) ** -0.5 * LOG2_E
    for h in range(MLA_HEADS):
        sl = slice(h * LANES, (h + 1) * LANES)
        q_ref[0, h] = (_rope_lanes(q[:, sl], tab, MLA_ROPE // 2) * scale).astype(BF16)
        k_ref[0, h] = (k[:, sl] + kpe).astype(BF16)
    v_ref[0] = v.astype(BF16)


def _in_proj_kernel(x_ref, g_ref, wt_ref, moba_tab, ret_tab, mla_tab, qg_ref, kvg_ref, wq_ref, wk_ref, wv_ref,
                    p_ref, q_ref, k_ref, v_ref, h_ref, wl_ref):
    @pl.when(pl.program_id(0) == 0)
    def _():
        wl_ref[...] = jnp.zeros_like(wl_ref)
        wl_ref[:W_KR, :] = wt_ref[:W_KR, :]
        wl_ref[W_KR + MLA_NOPE:W_KR + MLA_NOPE + MLA_ROPE, :] = wt_ref[W_KR:W_MIXERS, :]

    h_ref[...] = _rms(x_ref[...], g_ref[...]).astype(BF16)
    latent = _nt_dot(h_ref[...], wl_ref[...])
    _mla_heads(latent[:, :W_KR], latent[:, W_KR:],
               mla_tab, qg_ref, kvg_ref, wq_ref, wk_ref, wv_ref, q_ref, k_ref, v_ref)
    lane_tiles = GROUP // LANES
    for j in range(N_GROUPS):
        first = W_GATES + j * GROUP if j < G_MOBA_Q else W_MIXERS + (j - G_MOBA_Q) * GROUP
        acc = _nt_dot(h_ref[...], wt_ref[first:first + GROUP, :])
        for r in range(lane_tiles):
            y = acc[:, r * LANES:(r + 1) * LANES]
            if j in (G_MOBA_Q, G_MOBA_K):
                y = _rope_lanes(y, moba_tab, MOBA_HEAD_DIM // PARTIAL_ROPE_DIV // 2)
            elif j == G_RET_QK:
                y = _rope_lanes(y, ret_tab, RET_QK_DIM // 2)
                if r >= lane_tiles // 2:
                    y = y * (RET_QK_DIM ** -0.5)
            p_ref[:, j * GROUP + r * LANES:j * GROUP + (r + 1) * LANES] = y.astype(BF16)


def _layer_block(layer, shape, resident=False):
    mode = dict(pipeline_mode=pl.Buffered(1)) if resident else {}
    return pl.BlockSpec((None,) + shape, lambda *_: (layer,) + (0,) * len(shape), **mode)


def _in_proj(layer, x2, g, w, moba_tab, ret_tab, mla_tab, qg, kvg, wq, wk, wv, seq, tm):
    tokens = x2.shape[0]
    s_tiles = seq // tm
    batch = tokens // seq
    tab = pl.BlockSpec((3, tm, LANES), lambda i: (0, i % s_tiles, 0))
    const = lambda shape: _layer_block(layer, shape)
    heads = pl.BlockSpec((1, MLA_HEADS, tm, LANES), lambda i: (i // s_tiles, 0, i % s_tiles, 0))
    head_shape = jax.ShapeDtypeStruct((batch, MLA_HEADS, seq, LANES), BF16)
    return pl.pallas_call(
        _in_proj_kernel,
        grid=(tokens // tm,),
        in_specs=[
            pl.BlockSpec((tm, D_MODEL), lambda i: (i, 0)),
            const((1, D_MODEL)),
            _layer_block(layer, (IN_COLS, D_MODEL), resident=True),
            tab, tab, tab,
            const((1, MLA_Q_RANK)), const((1, MLA_KV_RANK)),
            const((MLA_Q_RANK, MLA_HEADS * LANES)), const((MLA_KV_RANK, MLA_HEADS * LANES)),
            const((MLA_KV_RANK, MLA_HEADS * MLA_V)),
        ],
        out_specs=[
            pl.BlockSpec((tm, IN_COLS_PADDED), lambda i: (i, 0)),
            heads, heads,
            pl.BlockSpec((1, tm, MLA_HEADS * MLA_V), lambda i: (i // s_tiles, i % s_tiles, 0)),
        ],
        out_shape=[jax.ShapeDtypeStruct((tokens, IN_COLS_PADDED), BF16), head_shape, head_shape,
                   jax.ShapeDtypeStruct((batch, seq, MLA_HEADS * MLA_V), BF16)],
        scratch_shapes=[pltpu.VMEM((tm, D_MODEL), BF16), pltpu.VMEM((GROUP, D_MODEL), BF16)],
        compiler_params=pltpu.CompilerParams(
            dimension_semantics=("arbitrary",), vmem_limit_bytes=VMEM_LIMIT),
        name="in_proj",
    )(x2, g, w, moba_tab, ret_tab, mla_tab, qg, kvg, wq, wk, wv)


_HEADS = range(HEAD_PAIR)
_ROWS = range(2)


def _fill_v_transposed(v_ref, vt_ref, t, head_dim):
    sub = lax.broadcasted_iota(jnp.int32, (BF16_ROWS, t), 0)
    ones_block = jnp.where(sub == 0, 1.0, 0.0)
    for j in range(vt_ref.shape[1]):
        vt = v_ref[0, j * t:(j + 1) * t, :].astype(F32).T
        for h in _HEADS:
            own = vt[h * head_dim:(h + 1) * head_dim]
            vt_ref[h, j] = jnp.concatenate([own, ones_block], axis=0).astype(BF16)


def _fill_causal_cap(cap_ref, t):
    key = lax.broadcasted_iota(jnp.int32, (t, t), 0)
    query = lax.broadcasted_iota(jnp.int32, (t, t), 1)
    cap_ref[...] = jnp.where(key <= query, jnp.inf, NEG_INF)


def _attend_rows(g, n_tiles, t, load_qt, load_k, vt_ref, write_out, scratch, head_dim, row_drop=None):
    causal_cap_ref, bufs, m_ref, acc_ref = scratch
    depth = bufs.shape[0]
    q_tile = [g, n_tiles - 1 - g]
    m_ref[...] = jnp.full_like(m_ref, -jnp.inf)
    acc_ref[...] = jnp.zeros_like(acc_ref)

    def locate(n):
        row = (n > g).astype(jnp.int32)
        return row, jnp.where(row == 0, g - n, n - (g + 1))

    def issue_scores(n, slot):
        row, kj = locate(n)
        ks, qts = load_k(kj), load_qt(row)
        for h in _HEADS:
            bufs[slot, h] = jnp.dot(ks[h], qts[h], preferred_element_type=F32)

    def softmax_step(n, slot, diagonal=False):
        row, kj = locate(n)
        chain = [HEAD_PAIR * row + h for h in _HEADS]
        s = [bufs[slot, h] for h in _HEADS]
        if diagonal:
            s = [jnp.minimum(s[h], causal_cap_ref[...]) for h in _HEADS]
        tile_max = [jnp.max(s[h], axis=0, keepdims=True) for h in _HEADS]
        shift = None
        if row_drop is not None and not diagonal:
            shift = [row_drop(row, h, kj) for h in _HEADS]
            tile_max = [jnp.maximum(tile_max[h] - shift[h], NEG_INF) for h in _HEADS]
        m_old = [m_ref[chain[h]] for h in _HEADS]
        m_new = [jnp.maximum(m_old[h], tile_max[h]) for h in _HEADS]
        alpha = [jnp.exp2(m_old[h] - m_new[h]) for h in _HEADS]
        m_sub = m_new if shift is None else [m_new[h] + shift[h] for h in _HEADS]
        p = [jnp.exp2(s[h] - m_sub[h]).astype(BF16) for h in _HEADS]
        pv = [jnp.dot(vt_ref[h, kj], p[h], preferred_element_type=F32) for h in _HEADS]
        for h in _HEADS:
            m_ref[chain[h]] = m_new[h]
            acc_ref[chain[h]] = alpha[h] * acc_ref[chain[h]] + pv[h]

    steps = n_tiles + 1
    for n in range(min(depth, steps)):
        issue_scores(n, n)
    for n in range(steps):
        softmax_step(n, n % depth, diagonal=n in (0, n_tiles))
        if n + depth < steps:
            issue_scores(n + depth, n % depth)

    for row in _ROWS:
        outs = []
        for h in _HEADS:
            acc = acc_ref[HEAD_PAIR * row + h]
            outs.append(acc[:head_dim] * (1.0 / acc[head_dim:head_dim + 1]))
        ot = jnp.concatenate(outs, axis=0)
        write_out(q_tile[row], ot.T.astype(BF16))


def _attn_scratch(seq, t, head_dim, pairs):
    chains = HEAD_PAIR * len(_ROWS)
    v_rows = head_dim + BF16_ROWS
    return [pltpu.VMEM((HEAD_PAIR, seq // t, v_rows, t), BF16),
            pltpu.VMEM((t, t), F32),
            pltpu.VMEM((pairs, SCORE_DEPTH, HEAD_PAIR, t, t), F32),
            pltpu.VMEM((pairs, chains, 1, t), F32),
            pltpu.VMEM((pairs, chains, v_rows, t), F32)]


def _pairs_per_step(n_tiles, wanted):
    assert n_tiles % 2 == 0
    return math.gcd(wanted, n_tiles // 2)


def _pair_scratch(scratch, pair):
    causal_cap_ref, bufs, m_ref, acc_ref = scratch
    return causal_cap_ref, bufs.at[pair], m_ref.at[pair], acc_ref.at[pair]


def _tile_rows(i, t):
    return pl.ds(pl.multiple_of(i * t, t), t)


def _step_pairs(pairs):
    return [pairs * pl.program_id(2) + pair for pair in range(pairs)]


def _mla_attn_kernel(q_ref, k_ref, v_ref, o_ref, qt_ref, vt_ref, *scratch, t, n_tiles):
    @pl.when(pl.program_id(2) == 0)
    def _():
        _fill_v_transposed(v_ref, vt_ref, t, MLA_V)
        _fill_causal_cap(scratch[0], t)

    def load_k(kj):
        return [k_ref[0, h, _tile_rows(kj, t), :] for h in _HEADS]

    def write_out(qt, tile):
        o_ref[0, _tile_rows(qt, t), :] = tile

    pairs = qt_ref.shape[0]
    pair_g = _step_pairs(pairs)
    def prepare(pair):
        g = pair_g[pair]
        for row, qi in enumerate((g, n_tiles - 1 - g)):
            for h in _HEADS:
                qt_ref[pair, row, h] = q_ref[0, h, _tile_rows(qi, t), :].astype(F32).T.astype(BF16)

    prepare(0)
    for pair, g in enumerate(pair_g):
        if pair + 1 < pairs:
            prepare(pair + 1)
        load_qt = lambda row, pair=pair: [qt_ref[pair, row, h] for h in _HEADS]
        _attend_rows(g, n_tiles, t, load_qt, load_k, vt_ref, write_out, _pair_scratch(scratch, pair), MLA_V)


def _mla_attn(q, k, v, t):
    batch, heads, seq, _ = q.shape
    n_tiles = seq // t
    pairs = _pairs_per_step(n_tiles, TILE_PAIRS)
    return pl.pallas_call(
        functools.partial(_mla_attn_kernel, t=t, n_tiles=n_tiles),
        grid=(batch, heads // HEAD_PAIR, n_tiles // (2 * pairs)),
        in_specs=[
            pl.BlockSpec((1, HEAD_PAIR, seq, LANES), lambda b, h, g: (b, h, 0, 0)),
            pl.BlockSpec((1, HEAD_PAIR, seq, LANES), lambda b, h, g: (b, h, 0, 0)),
            pl.BlockSpec((1, seq, LANES), lambda b, h, g: (b, 0, h)),
        ],
        out_specs=pl.BlockSpec((1, seq, LANES), lambda b, h, g: (b, 0, h)),
        out_shape=jax.ShapeDtypeStruct((batch, seq, heads * MLA_V), BF16),
        scratch_shapes=[pltpu.VMEM((pairs, len(_ROWS), HEAD_PAIR, LANES, t), BF16)]
                       + _attn_scratch(seq, t, MLA_V, pairs),
        compiler_params=pltpu.CompilerParams(
            dimension_semantics=("parallel", "parallel", "arbitrary"), vmem_limit_bytes=VMEM_LIMIT),
        name="mla_attn",
    )(q, k, v)


def _moba_kernel(q_ref, k_ref, v_ref, o_ref, kmean_ref, km3_ref, qh_ref, drop_ref, vt_ref, *scratch, nb):
    t = MOBA_BLOCK
    part = km3_ref.shape[0] // (3 * HEAD_PAIR)

    @pl.when(pl.program_id(2) == 0)
    def _():
        _fill_v_transposed(v_ref, vt_ref, t, MOBA_HEAD_DIM)
        _fill_causal_cap(scratch[0], t)
        kmean_ref[...] = jnp.zeros_like(kmean_ref)
        for j in range(nb):
            kb = k_ref[0, j * t:(j + 1) * t, :].astype(F32)
            kmean_ref[j:j + 1, :] = jnp.mean(kb, axis=0, keepdims=True)
        lane_k = lax.broadcasted_iota(jnp.int32, (part, LANES), 1)
        for hh in _HEADS:
            in_head_k = (lane_k >= hh * MOBA_HEAD_DIM) & (lane_k < (hh + 1) * MOBA_HEAD_DIM)
            rest = jnp.where(in_head_k, kmean_ref[:part], 0.0)
            for s in range(3):
                piece = rest.astype(BF16)
                km3_ref[(3 * hh + s) * part:(3 * hh + s + 1) * part, :] = piece
                rest = rest - piece.astype(F32)

    sub = lax.broadcasted_iota(jnp.int32, (LANES, t), 0)
    blk_id = lax.broadcasted_iota(jnp.int32, (nb, t), 0)
    pairs = qh_ref.shape[0]
    pair_g = _step_pairs(pairs)

    def prepare(pair):
        g = pair_g[pair]
        for row, qi in enumerate((g, nb - 1 - g)):
            q_t = q_ref[0, _tile_rows(qi, t), :].astype(F32).T
            gates = jnp.dot(km3_ref[...], q_t.astype(BF16), preferred_element_type=F32)
            for hh in _HEADS:
                in_head = (sub >= hh * MOBA_HEAD_DIM) & (sub < (hh + 1) * MOBA_HEAD_DIM)
                qh_ref[pair, row, hh] = (jnp.where(in_head, q_t, 0.0)
                                         * (MOBA_HEAD_DIM ** -0.5 * LOG2_E)).astype(BF16)
                g3 = [gates[(3 * hh + s) * part:(3 * hh + s + 1) * part] for s in range(3)]
                gate = ((g3[0] + g3[1]) + g3[2])[:nb]
                left = jnp.where(blk_id < qi, gate, NEG_INF)
                keep = blk_id == qi
                for r in range(MOBA_TOPK):
                    is_max = left == jnp.max(left, axis=0, keepdims=True)
                    pick = blk_id == jnp.min(jnp.where(is_max, blk_id, nb), axis=0, keepdims=True)
                    keep = keep | (pick & (qi > r))
                    left = jnp.where(pick, -jnp.inf, left)
                drop = jnp.where(keep, 0.0, jnp.inf)
                for j in range(nb):
                    drop_ref[pair, row, hh, j] = drop[j:j + 1, :]

    def load_k(kj):
        return [k_ref[0, _tile_rows(kj, t), :]] * HEAD_PAIR

    def write_out(qt, tile):
        o_ref[0, _tile_rows(qt, t), :] = tile

    prepare(0)
    for pair, g in enumerate(pair_g):
        if pair + 1 < pairs:
            prepare(pair + 1)
        load_qt = lambda row, pair=pair: [qh_ref[pair, row, h] for h in _HEADS]
        row_drop = lambda row, h, kj, pair=pair: drop_ref[pair, row, h, kj]
        _attend_rows(g, nb, t, load_qt, load_k, vt_ref, write_out, _pair_scratch(scratch, pair),
                     MOBA_HEAD_DIM, row_drop=row_drop)


def _moba_attn(p3):
    batch, seq, _ = p3.shape
    t = MOBA_BLOCK
    nb = seq // t
    pairs = _pairs_per_step(nb, TILE_PAIRS)
    mean_rows = -(-nb // BF16_ROWS) * BF16_ROWS
    assert mean_rows <= LANES
    per_group = GROUP // LANES
    return pl.pallas_call(
        functools.partial(_moba_kernel, nb=nb),
        grid=(batch, MOBA_HEADS // HEAD_PAIR, nb // (2 * pairs)),
        in_specs=[
            pl.BlockSpec((1, seq, LANES), lambda b, h, g: (b, 0, G_MOBA_Q * per_group + h)),
            pl.BlockSpec((1, seq, LANES), lambda b, h, g: (b, 0, G_MOBA_K * per_group + h)),
            pl.BlockSpec((1, seq, LANES), lambda b, h, g: (b, 0, G_MOBA_V * per_group + h)),
        ],
        out_specs=pl.BlockSpec((1, seq, LANES), lambda b, h, g: (b, 0, h)),
        out_shape=jax.ShapeDtypeStruct((batch, seq, MOBA_HEADS * MOBA_HEAD_DIM), BF16),
        scratch_shapes=[pltpu.VMEM((LANES, LANES), F32),
                        pltpu.VMEM((3 * HEAD_PAIR * mean_rows, LANES), BF16),
                        pltpu.VMEM((pairs, len(_ROWS), HEAD_PAIR, LANES, t), BF16),
                        pltpu.VMEM((pairs, len(_ROWS), HEAD_PAIR, nb, 1, t), F32)]
                       + _attn_scratch(seq, t, MOBA_HEAD_DIM, pairs),
        compiler_params=pltpu.CompilerParams(
            dimension_semantics=("parallel", "parallel", "arbitrary"), vmem_limit_bytes=VMEM_LIMIT),
        name="moba_attn",
    )(p3, p3, p3)


def _retention_kernel(qk_ref, v_ref, g_ref, decay_ref, xi_ref, zeta_ref, gamma_ref, gnw_ref, gnb_ref,
                      o_ref, state_ref, *, chunks):
    c = RET_CHUNK

    @pl.when(pl.program_id(1) == 0)
    def _():
        state_ref[...] = jnp.zeros_like(state_ref)

    lane = lax.broadcasted_iota(jnp.int32, (c, LANES), 1)
    k_off = RET_HEADS * RET_QK_DIM
    for n in range(chunks):
        rows = slice(n * c, (n + 1) * c)
        for h in range(RET_HEADS):
            pair, half = divmod(h, HEAD_PAIR)
            in_head = (lane >= half * RET_QK_DIM) & (lane < (half + 1) * RET_QK_DIM)
            q = qk_ref[0, rows, pair * LANES:(pair + 1) * LANES]
            k = jnp.where(in_head, qk_ref[0, rows, k_off + pair * LANES:k_off + (pair + 1) * LANES].astype(F32), 0.0)
            vs = slice(h * RET_V_DIM, (h + 1) * RET_V_DIM)
            v = v_ref[0, rows, vs]
            state = state_ref[h]
            scores = _nt_dot(q, k.astype(BF16)) * decay_ref[h]
            inner = jnp.dot(scores.astype(BF16), v, preferred_element_type=F32)
            cross = jnp.dot((q.astype(F32) * xi_ref[h]).astype(BF16), state.astype(BF16),
                            preferred_element_type=F32)
            kv = lax.dot_general((k * zeta_ref[h]).astype(BF16), v, (((0,), (0,)), ((), ())),
                                 preferred_element_type=F32)
            state_ref[h] = gamma_ref[h] * state + kv
            o = inner + cross
            mu = jnp.mean(o, axis=-1, keepdims=True)
            d = o - mu
            var = jnp.mean(d * d, axis=-1, keepdims=True)
            o = d * lax.rsqrt(var + GN_EPS) * gnw_ref[:, vs] + gnb_ref[:, vs]
            g = g_ref[0, rows, vs].astype(F32)
            o_ref[0, rows, vs] = (g * jax.nn.sigmoid(g) * o).astype(BF16)


def _retention(layer, p3, decay, xi, zeta, gamma, gnw, gnb, tr):
    batch, seq, _ = p3.shape
    width = RET_HEADS * RET_V_DIM
    const = lambda shape: pl.BlockSpec(shape, lambda b, i: (0,) * len(shape))
    return pl.pallas_call(
        functools.partial(_retention_kernel, chunks=tr // RET_CHUNK),
        grid=(batch, seq // tr),
        in_specs=[
            pl.BlockSpec((1, tr, GROUP), lambda b, i: (b, i, G_RET_QK)),
            pl.BlockSpec((1, tr, GROUP), lambda b, i: (b, i, G_RET_V)),
            pl.BlockSpec((1, tr, GROUP), lambda b, i: (b, i, G_RET_G)),
            const((RET_HEADS, RET_CHUNK, RET_CHUNK)),
            const((RET_HEADS, RET_CHUNK, LANES)), const((RET_HEADS, RET_CHUNK, LANES)),
            const((RET_HEADS, 1, LANES)),
            _layer_block(layer, (1, width)), _layer_block(layer, (1, width)),
        ],
        out_specs=pl.BlockSpec((1, tr, width), lambda b, i: (b, i, 0)),
        out_shape=jax.ShapeDtypeStruct((batch, seq, width), BF16),
        scratch_shapes=[pltpu.VMEM((RET_HEADS, LANES, RET_V_DIM), F32)],
        compiler_params=pltpu.CompilerParams(
            dimension_semantics=("parallel", "arbitrary"), vmem_limit_bytes=VMEM_LIMIT),
        name="retention",
    )(p3, p3, p3, decay, xi, zeta, gamma, gnw, gnb)


def _merge_mlp_kernel(x_ref, gates_ref, oa_ref, ob_ref, oc_ref, wa_ref, wb_ref, wc_ref, wo_ref,
                      g_ref, wu_ref, wd_ref, fg_ref, o_ref, h_ref, *, tf, final_norm):
    merged = None
    for idx, (o_in, w) in enumerate(((oa_ref, wa_ref), (ob_ref, wb_ref), (oc_ref, wc_ref))):
        gate = gates_ref[:, idx * D_MODEL:(idx + 1) * D_MODEL].astype(F32)
        term = jax.nn.sigmoid(gate) * jnp.dot(o_in[...], w[...], preferred_element_type=F32)
        merged = term if merged is None else merged + term
    x = x_ref[...] + jnp.dot(merged.astype(BF16), wo_ref[...], preferred_element_type=F32)

    h_ref[...] = _rms(x, g_ref[...]).astype(BF16)
    acc = x
    for f in range(D_FF // tf):
        cols = slice(f * tf, (f + 1) * tf)
        u = jnp.maximum(jnp.dot(h_ref[...], wu_ref[:, cols], preferred_element_type=F32), 0.0)
        acc = acc + jnp.dot((u * u).astype(BF16), wd_ref[cols, :], preferred_element_type=F32)
    o_ref[...] = _rms(acc, fg_ref[...]) if final_norm else acc


def _merge_mlp(layer, x2, p2, oa, ob, oc, wa, wb, wc, wo, g, wu, wd, fg, tm, tf, final_norm):
    tokens = x2.shape[0]
    row = lambda w: pl.BlockSpec((tm, w), lambda i: (i, 0))
    resident = lambda shape: _layer_block(layer, shape, resident=True)
    return pl.pallas_call(
        functools.partial(_merge_mlp_kernel, tf=tf, final_norm=final_norm),
        grid=(tokens // tm,),
        in_specs=[row(D_MODEL), row(3 * D_MODEL), row(GROUP), row(GROUP), row(GROUP),
                  resident((GROUP, D_MODEL)), resident((GROUP, D_MODEL)), resident((GROUP, D_MODEL)),
                  resident((D_MODEL, D_MODEL)),
                  _layer_block(layer, (1, D_MODEL)), resident((D_MODEL, D_FF)), resident((D_FF, D_MODEL)),
                  pl.BlockSpec((1, D_MODEL), lambda i: (0, 0))],
        out_specs=row(D_MODEL),
        out_shape=jax.ShapeDtypeStruct((tokens, D_MODEL), F32),
        scratch_shapes=[pltpu.VMEM((tm, D_MODEL), BF16)],
        compiler_params=pltpu.CompilerParams(
            dimension_semantics=("parallel",), vmem_limit_bytes=VMEM_LIMIT),
        name="merge_mlp",
    )(x2, p2, oa, ob, oc, wa, wb, wc, wo, g, wu, wd, fg)


def _rope_tables(seq, dim, theta, period, offset):
    half = dim // 2
    lane = jnp.arange(LANES) % period - offset
    in_lo, in_hi = (lane >= 0) & (lane < half), (lane >= half) & (lane < dim)
    freq = jnp.clip(jnp.where(in_hi, lane - half, lane), 0, half - 1).astype(F32)
    inv = 1.0 / (theta ** (2.0 * freq / dim))
    split = math.gcd(ROPE_SPLIT, seq)
    lo = jnp.arange(split, dtype=F32)[None, :, None] * inv
    hi = (jnp.arange(seq // split, dtype=F32) * split)[:, None, None] * inv
    cos = (jnp.cos(hi) * jnp.cos(lo) - jnp.sin(hi) * jnp.sin(lo)).reshape(seq, LANES)
    sin = (jnp.sin(hi) * jnp.cos(lo) + jnp.cos(hi) * jnp.sin(lo)).reshape(seq, LANES)
    return jnp.stack([jnp.where(in_lo | in_hi, cos, 1.0), jnp.where(in_lo, -sin, 0.0),
                      jnp.where(in_hi, sin, 0.0)])


def _retention_tables():
    c = RET_CHUNK
    log_gamma = jnp.log(1.0 - 2.0 ** (-5.0 - jnp.arange(RET_HEADS, dtype=F32)))
    pos = jnp.arange(c, dtype=F32)
    diff = pos[:, None] - pos[None, :]
    decay = jnp.where(diff >= 0, jnp.exp(log_gamma[:, None, None] * diff), 0.0)
    xi = jnp.exp(log_gamma[:, None] * (pos + 1.0))
    zeta = jnp.exp(log_gamma[:, None] * (c - 1.0 - pos))
    gamma_c = jnp.exp(log_gamma * c)
    bcast = lambda t: jnp.broadcast_to(t[..., None], t.shape + (LANES,))
    return decay, bcast(xi), bcast(zeta), bcast(gamma_c[:, None])


def _pad_mla_weights(w_q_up, w_kv_up):
    depth = w_q_up.shape[0]
    d_qk = MLA_NOPE + MLA_ROPE
    pad_lanes = lambda w: jnp.pad(w, ((0, 0), (0, 0), (0, 0), (0, LANES - w.shape[-1])))
    wq = pad_lanes(w_q_up.astype(BF16).reshape(depth, MLA_Q_RANK, MLA_HEADS, d_qk))
    wkv = w_kv_up.astype(BF16).reshape(depth, MLA_KV_RANK, MLA_HEADS, MLA_NOPE + MLA_V)
    wk = pad_lanes(wkv[..., :MLA_NOPE])
    wv = wkv[..., MLA_NOPE:]
    return (wq.reshape(depth, MLA_Q_RANK, MLA_HEADS * LANES), wk.reshape(depth, MLA_KV_RANK, MLA_HEADS * LANES),
            wv.reshape(depth, MLA_KV_RANK, MLA_HEADS * MLA_V))


def _tile_sizes(seq):
    pick = lambda want: want if seq % want == 0 else seq
    return dict(in_proj=pick(512), attn=pick(256), ret=pick(512),
                mlp=pick(512), mlp_ff=512)


def kernel(x, attn_norm, w_in, mla_q_norm, mla_kv_norm, mla_w_q_up, mla_w_kv_up, ret_gn_w, ret_gn_b,
           w_branch_mla, w_branch_moba, w_branch_ret, w_out, mlp_norm, w_mlp_up, w_mlp_down, final_norm):
    batch, seq, d = x.shape
    depth = w_in.shape[0]
    tiles = _tile_sizes(seq)
    mla_tab = _rope_tables(seq, MLA_ROPE, ROPE_THETA, LANES, MLA_NOPE)
    moba_tab = _rope_tables(seq, MOBA_HEAD_DIM // PARTIAL_ROPE_DIV, ROPE_THETA, MOBA_HEAD_DIM, 0)
    ret_tab = _rope_tables(seq, RET_QK_DIM, RET_THETA, RET_QK_DIM, 0)
    decay, xi, zeta, gamma_c = _retention_tables()
    rows = lambda v: v[:, None, :]
    bf16 = lambda w: w.astype(BF16)

    assert w_in.shape[1:] == (D_MODEL, IN_COLS)
    w_in_t = bf16(jnp.swapaxes(w_in, 1, 2))
    wq, wk, wv = _pad_mla_weights(mla_w_q_up, mla_w_kv_up)
    branch_w = [bf16(w) for w in (w_branch_mla, w_branch_moba, w_branch_ret, w_out)]
    mlp_w = [rows(mlp_norm), bf16(w_mlp_up), bf16(w_mlp_down)]
    x2 = x.reshape(batch * seq, d)
    for l in range(depth):
        p2, q_mla, k_mla, v_mla = _in_proj(l, x2, rows(attn_norm), w_in_t, moba_tab, ret_tab, mla_tab,
                                           rows(mla_q_norm), rows(mla_kv_norm), wq, wk, wv,
                                           seq, tiles["in_proj"])
        p3 = p2.reshape(batch, seq, IN_COLS_PADDED)
        o_mla = _mla_attn(q_mla, k_mla, v_mla, tiles["attn"])
        o_moba = _moba_attn(p3)
        o_ret = _retention(l, p3, decay, xi, zeta, gamma_c, rows(ret_gn_w), rows(ret_gn_b), tiles["ret"])
        flat = lambda o: o.reshape(batch * seq, GROUP)
        x2 = _merge_mlp(l, x2, p2, flat(o_mla), flat(o_moba), flat(o_ret), *branch_w, *mlp_w,
                        final_norm.reshape(1, -1), tiles["mlp"], tiles["mlp_ff"], final_norm=(l == depth - 1))
    return x2.reshape(batch, seq, d)
```
